```python
import jax
import jax.numpy as jnp
from jax import lax
import numpy as np

D_MODEL = 2048
BATCH = 32
SEQ = 256
DEPTH = 1
DEC_BATCH = 4
DEC_SEQ = 2048
PAST_LEN = 256

GRID_W = 64
D_MIX = D_MODEL
D_A = D_MIX // 2
D_B = D_MIX - D_A
HEAD_A = 64
H_A = D_A // HEAD_A
CHUNK = 128
H_B = 8
HEAD_B = D_B // H_B
LORA_W = 64
LORA_A = 64
N_SHIFT = 4 * D_A + 2 * LORA_W + 2 * LORA_A
P_IN = N_SHIFT + 2 * D_B
PK_HEADS = 8
D_KEY = 256
HALF_KEY = D_KEY // 2
N_KEYS = 128
N_EXPERTS = N_KEYS * N_KEYS
TOPK = 16
TOKEN_BLOCK = 128
EPS = 1e-6
GN_EPS = 64e-5

kernel_name = "hybrid_rwkv7_chunkgmlp_peer_diffusion_step"

F32 = jnp.float32


def rmsnorm(x, g):
    xf = x.astype(F32)
    y = xf * lax.rsqrt(jnp.mean(xf * xf, axis=-1, keepdims=True) + EPS)
    return (y * g.astype(F32)).astype(x.dtype)


def layernorm(x, w, b):
    xf = x.astype(F32)
    mu = jnp.mean(xf, axis=-1, keepdims=True)
    var = jnp.mean(jnp.square(xf - mu), axis=-1, keepdims=True)
    y = (xf - mu) * lax.rsqrt(var + EPS)
    return (y * w.astype(F32) + b.astype(F32)).astype(x.dtype)


def seq_shift(z):
    zp = jnp.pad(z, ((0, 0), (1, 1), (0, 0)))
    return 0.5 * (zp[:, :-2] + zp[:, 2:])


def grid_shift(z):
    b, t, ch = z.shape
    rows = t // GRID_W
    g = z.reshape(b, rows, GRID_W, ch)
    gp = jnp.pad(g, ((0, 0), (1, 1), (1, 1), (0, 0)))
    nb = 0.25 * (gp[:, :-2, 1:-1] + gp[:, 2:, 1:-1] + gp[:, 1:-1, :-2] + gp[:, 1:-1, 2:])
    return nb.reshape(b, t, ch)


def rwkv7_scan(r, w, k, v, kk, a, s0, reverse):
    def step(s, xs):
        r_t, w_t, k_t, v_t, kk_t, a_t = xs
        sa = jnp.einsum('bhvk,bhk->bhv', s, -kk_t)
        s = (s * w_t[:, :, None, :]
             + sa[..., None] * (kk_t * a_t)[:, :, None, :]
             + v_t[..., None] * k_t[:, :, None, :])
        y = jnp.einsum('bhvk,bhk->bhv', s, r_t)
        return s, y
    xs = tuple(jnp.moveaxis(u, 1, 0) for u in (r, w, k, v, kk, a))
    s_final, ys = lax.scan(step, s0.astype(F32), xs, reverse=reverse)
    return jnp.moveaxis(ys, 0, 1), s_final


def token_mix(h, shift_fn, s0_f, s0_b, p):
    b, t, _ = h.shape
    z = jnp.einsum('btd,dp->btp', h, p['w_in'])
    za, zb = z[..., :N_SHIFT], z[..., N_SHIFT:]
    za = za + p['mu_shift'] * (shift_fn(za) - za)
    r, k, v, g = (za[..., i * D_A:(i + 1) * D_A] for i in range(4))
    lw = za[..., 4 * D_A:4 * D_A + 2 * LORA_W].reshape(b, t, 2, LORA_W)
    la = za[..., 4 * D_A + 2 * LORA_W:].reshape(b, t, 2, LORA_A)
    wl = (p['w0'] + jnp.einsum('btdr,drc->btdc', jnp.tanh(lw), p['w2'])).astype(F32)
    decay = jnp.exp(-jnp.exp(-jax.nn.softplus(-wl) - 0.5))
    a = jax.nn.sigmoid((p['a0'] + jnp.einsum('btdr,drc->btdc', la, p['a2'])).astype(F32))
    kk = (k * p['k_k']).astype(F32).reshape(b, t, H_A, HEAD_A)
    kk = kk / jnp.maximum(jnp.linalg.norm(kk, axis=-1, keepdims=True), 1e-12)
    kd = k.astype(F32)[:, :, None] * (1.0 + (a - 1.0) * p['k_a'].astype(F32))
    hs = lambda u: u.reshape(u.shape[:-1] + (H_A, HEAD_A))
    rh, vh, kdh, dh, ah = hs(r.astype(F32)), hs(v.astype(F32)), hs(kd), hs(decay), hs(a)
    y_fwd, s_f = rwkv7_scan(rh, dh[:, :, 0], kdh[:, :, 0], vh, kk, ah[:, :, 0], s0_f, False)
    y_bwd, s_b = rwkv7_scan(rh, dh[:, :, 1], kdh[:, :, 1], vh, kk, ah[:, :, 1], s0_b, True)
    y = y_fwd + y_bwd
    mu = jnp.mean(y, axis=-1, keepdims=True)
    var = jnp.mean(jnp.square(y - mu), axis=-1, keepdims=True)
    yn = (y - mu) * lax.rsqrt(var + GN_EPS)
    yn = yn * hs(p['ln_x_w'].astype(F32)) + hs(p['ln_x_b'].astype(F32))
    r_k = p['r_k'].astype(F32)
    bonus = jnp.sum(jnp.sum(rh[:, :, None] * kdh * r_k, axis=-1, keepdims=True) * vh[:, :, None], axis=2)
    y_a = ((yn + bonus).reshape(b, t, D_A) * jax.nn.sigmoid(g.astype(F32))).astype(h.dtype)
    u = jax.nn.gelu(zb[..., :D_B])
    vg = layernorm(jax.nn.gelu(zb[..., D_B:]), p['ln_v_w'], p['ln_v_b'])
    vc = vg.reshape(b, t // CHUNK, CHUNK, H_B, HEAD_B)
    sp = jnp.einsum('hpq,bnqhc->bnphc', p['w_s'], vc) + p['b_s'].T[:, :, None]
    y_g = u * sp.reshape(b, t, D_B)
    o = jnp.einsum('btc,cd->btd', jnp.concatenate([y_a, y_g], axis=-1), p['w_out'])
    return o, s_f.astype(h.dtype), s_b.astype(h.dtype)


def peer(h, p):
    b, t, d = h.shape
    n = b * t
    x = h.reshape(n, d)
    q = jnp.einsum('nd,dq->nq', x, p['w_query']).reshape(n, PK_HEADS, 2, HALF_KEY)
    s1 = jnp.einsum('nhc,hkc->nhk', q[:, :, 0], p['sub_keys'][0])
    s2 = jnp.einsum('nhc,hkc->nhk', q[:, :, 1], p['sub_keys'][1])
    v1, i1 = lax.top_k(s1, TOPK)
    v2, i2 = lax.top_k(s2, TOPK)
    cand = (v1[..., :, None] + v2[..., None, :]).reshape(n, PK_HEADS, TOPK * TOPK)
    cidx = (i1[..., :, None] * N_KEYS + i2[..., None, :]).reshape(n, PK_HEADS, TOPK * TOPK)
    top_s, pos = lax.top_k(cand, TOPK)
    idx = jnp.take_along_axis(cidx, pos, axis=-1)
    gates = jax.nn.softmax(top_s.astype(F32), axis=-1).astype(h.dtype)
    eu, ev = p['expert_u'], p['expert_v']

    def block(args):
        xb, ib, gb = args
        act = jax.nn.gelu(jnp.einsum('nd,nhkd->nhk', xb, eu[ib]))
        return jnp.einsum('nhk,nhkd->nd', act * gb, ev[ib])

    nb = n // TOKEN_BLOCK
    out = lax.map(block, (x.reshape(nb, TOKEN_BLOCK, d),
                          idx.reshape(nb, TOKEN_BLOCK, PK_HEADS, TOPK),
                          gates.reshape(nb, TOKEN_BLOCK, PK_HEADS, TOPK)))
    return out.reshape(b, t, d)


def layer(x, mod, shift_fn, s0_f, s0_b, p):
    sh1, sc1, gt1, sh2, sc2, gt2 = (m[:, None, :] for m in jnp.split(mod, 6, axis=-1))
    h = rmsnorm(x, p['g_pre1']) * (1.0 + sc1) + sh1
    o, s_f, s_b = token_mix(h, shift_fn, s0_f, s0_b, p)
    x = x + gt1 * rmsnorm(o, p['g_post1'])
    h = rmsnorm(x, p['g_pre2']) * (1.0 + sc2) + sh2
    x = x + gt2 * rmsnorm(peer(h, p), p['g_post2'])
    return x, s_f, s_b


def setup_inputs(seed: int = 0) -> dict:
    key = jax.random.key(seed)
    ks = jax.random.split(key, 40)
    nrm = lambda i, shape, s: jax.random.normal(ks[i], shape, F32) * s
    D = D_MODEL
    return {
        'x_prompt': nrm(0, (BATCH, SEQ, D), 1.0),
        'x_sample': nrm(1, (DEC_BATCH, DEC_SEQ, D), 1.0),
        'c': nrm(2, (DEC_BATCH, D), 1.0),
        'state_fwd': nrm(3, (DEC_BATCH, DEPTH, H_A, HEAD_A, HEAD_A), 0.5),
        'state_bwd': nrm(4, (DEC_BATCH, DEPTH, H_A, HEAD_A, HEAD_A), 0.5),
        'c_ctx': nrm(5, (D,), 1.0),
        'w_ada': nrm(6, (DEPTH, D, 6 * D), 0.5 * D ** -0.5),
        'b_ada': nrm(7, (DEPTH, 6 * D), 0.02),
        'g_pre1': 1.0 + nrm(8, (DEPTH, D), 0.02),
        'g_post1': 1.0 + nrm(9, (DEPTH, D), 0.02),
        'g_pre2': 1.0 + nrm(10, (DEPTH, D), 0.02),
        'g_post2': 1.0 + nrm(11, (DEPTH, D), 0.02),
        'w_in': nrm(12, (DEPTH, D, P_IN), D ** -0.5),
        'mu_shift': jax.random.uniform(ks[13], (DEPTH, N_SHIFT), F32),
        'w0': nrm(14, (DEPTH, 2, D_A), 0.5),
        'w2': nrm(15, (DEPTH, 2, LORA_W, D_A), 0.1),
        'a0': nrm(16, (DEPTH, 2, D_A), 0.3),
        'a2': nrm(17, (DEPTH, 2, LORA_A, D_A), 0.5 * LORA_A ** -0.5),
        'k_k': 0.85 + nrm(18, (DEPTH, D_A), 0.05),
        'k_a': 1.0 + nrm(19, (DEPTH, D_A), 0.05),
        'r_k': nrm(20, (DEPTH, H_A, HEAD_A), 0.1),
        'ln_x_w': 1.0 + nrm(21, (DEPTH, D_A), 0.02),
        'ln_x_b': nrm(22, (DEPTH, D_A), 0.02),
        'ln_v_w': 1.0 + nrm(23, (DEPTH, D_B), 0.02),
        'ln_v_b': nrm(24, (DEPTH, D_B), 0.02),
        'w_s': nrm(25, (DEPTH, H_B, CHUNK, CHUNK), CHUNK ** -0.5),
        'b_s': 1.0 + nrm(26, (DEPTH, H_B, CHUNK), 0.02),
        'w_out': nrm(27, (DEPTH, D_MIX, D), D_MIX ** -0.5),
        'w_query': nrm(28, (DEPTH, D, PK_HEADS * D_KEY), D ** -0.5),
        'sub_keys': nrm(29, (DEPTH, 2, PK_HEADS, N_KEYS, HALF_KEY), HALF_KEY ** -0.5),
        'expert_u': nrm(30, (DEPTH, N_EXPERTS, D), D ** -0.5),
        'expert_v': nrm(31, (DEPTH, N_EXPERTS, D), D ** -0.5),
    }


def reference(x_prompt, x_sample, c, state_fwd, state_bwd, c_ctx, w_ada, b_ada, g_pre1, g_post1,
              g_pre2, g_post2, w_in, mu_shift, w0, w2, a0, a2, k_k, k_a, r_k, ln_x_w, ln_x_b,
              ln_v_w, ln_v_b, w_s, b_s, w_out, w_query, sub_keys, expert_u, expert_v):
    xp, xs = x_prompt, x_sample
    new_f, new_b = [], []
    for l in range(DEPTH):
        p = {'g_pre1': g_pre1[l], 'g_post1': g_post1[l], 'g_pre2': g_pre2[l], 'g_post2': g_post2[l],
             'w_in': w_in[l], 'mu_shift': mu_shift[l], 'w0': w0[l], 'w2': w2[l], 'a0': a0[l],
             'a2': a2[l], 'k_k': k_k[l], 'k_a': k_a[l], 'r_k': r_k[l], 'ln_x_w': ln_x_w[l],
             'ln_x_b': ln_x_b[l], 'ln_v_w': ln_v_w[l], 'ln_v_b': ln_v_b[l], 'w_s': w_s[l],
             'b_s': b_s[l], 'w_out': w_out[l], 'w_query': w_query[l], 'sub_keys': sub_keys[l],
             'expert_u': expert_u[l], 'expert_v': expert_v[l]}
        mod_ctx = (jax.nn.silu(c_ctx) @ w_ada[l] + b_ada[l])[None]
        mod_lat = jax.nn.silu(c) @ w_ada[l] + b_ada[l]
        zeros = jnp.zeros((xp.shape[0], H_A, HEAD_A, HEAD_A), xp.dtype)
        xp, s_f, s_b = layer(xp, mod_ctx, seq_shift, zeros, zeros, p)
        new_f.append(s_f)
        new_b.append(s_b)
        xs, _, _ = layer(xs, mod_lat, grid_shift, state_fwd[:, l], state_bwd[:, l], p)
    new_state_fwd = jnp.stack(new_f, axis=1)
    new_state_bwd = jnp.stack(new_b, axis=1)
    return (xp, xs, new_state_fwd, new_state_bwd)
```

```python
import functools

import jax
import jax.numpy as jnp
from jax import lax
from jax.experimental import pallas as pl
from jax.experimental.pallas import tpu as pltpu

F32 = jnp.float32
BF16 = jnp.bfloat16
I32 = jnp.int32

EPS = 1e-6
GN_EPS = 64e-5
HEAD_A = 64
LANES = 128
GRID_W = 64
GMLP_CHUNK = 128
PK_HEADS = 8
N_KEYS = 128
TOPK = 16
SCAN_CHUNK = 64
DECAY_SCALE = 0.6065306597126334
VMEM_LIMIT = 48 * 1024 * 1024


def _params(sem):
    return pltpu.CompilerParams(dimension_semantics=sem, vmem_limit_bytes=VMEM_LIMIT)


def _sigmoid(x):
    return 1.0 / (1.0 + jnp.exp(-x))


def _gelu(x):
    return 0.5 * x * (1.0 + jnp.tanh(0.7978845608028654 * (x + 0.044715 * (x * x * x))))


def _dot(a, b):
    return jnp.dot(a.astype(BF16), b.astype(BF16), preferred_element_type=F32)


def _dot_nt(a, b):
    return lax.dot_general(a.astype(BF16), b.astype(BF16), (((1,), (1,)), ((), ())),
                           preferred_element_type=F32)


def _dot_tn(a, b):
    return lax.dot_general(a.astype(BF16), b.astype(BF16), (((0,), (0,)), ((), ())),
                           preferred_element_type=F32)


def _dot_hi(a, b):
    return jnp.dot(a, b, precision=lax.Precision.HIGHEST, preferred_element_type=F32)


def _head_sum(x, first_head):
    s_a = jnp.sum(jnp.where(first_head, x, 0.0), axis=-1, keepdims=True)
    s_b = jnp.sum(jnp.where(first_head, 0.0, x), axis=-1, keepdims=True)
    return jnp.where(first_head, s_a, s_b)


def _mod_kernel(c_ref, w_ref, b_ref, o_ref):
    c = c_ref[...]
    o_ref[...] = _dot(c * _sigmoid(c), w_ref[...]) + b_ref[...]


def _modulation(cvec, w_ada, b_ada):
    rows, d = cvec.shape
    n = w_ada.shape[1]
    tn = 1024
    return pl.pallas_call(
        _mod_kernel,
        out_shape=jax.ShapeDtypeStruct((rows, n), F32),
        grid=(n // tn,),
        in_specs=[pl.BlockSpec((rows, d), lambda j: (0, 0)),
                  pl.BlockSpec((d, tn), lambda j: (0, j)),
                  pl.BlockSpec((1, tn), lambda j: (0, j))],
        out_specs=pl.BlockSpec((rows, tn), lambda j: (0, j)),
        compiler_params=_params(("parallel",)),
        name="adaln_mod",
    )(cvec, w_ada, b_ada)


def _in_proj_kernel(x_ref, mod_ref, g_ref, w_ref, o_ref, h_ref):
    @pl.when(pl.program_id(1) == 0)
    def _():
        x = x_ref[...]
        y = x * lax.rsqrt(jnp.mean(x * x, axis=-1, keepdims=True) + EPS) * g_ref[...]
        h_ref[...] = (y * (1.0 + mod_ref[0, 1:2, :]) + mod_ref[0, 0:1, :]).astype(BF16)

    o_ref[...] = jnp.dot(h_ref[...], w_ref[...], preferred_element_type=F32)


def _in_proj(x, modm, g_pre, w_bf, group0, rows_per_group):
    n, d = x.shape
    p = w_bf.shape[1]
    tm, tn = 512, 1280
    grp = lambda i, j: (group0 + (i * tm) // rows_per_group, 0, 0)
    return pl.pallas_call(
        _in_proj_kernel,
        out_shape=jax.ShapeDtypeStruct((n, p), F32),
        grid=(n // tm, p // tn),
        in_specs=[pl.BlockSpec((tm, d), lambda i, j: (i, 0)),
                  pl.BlockSpec((1, 6, d), grp),
                  pl.BlockSpec((1, d), lambda i, j: (0, 0)),
                  pl.BlockSpec((d, tn), lambda i, j: (0, j))],
        out_specs=pl.BlockSpec((tm, tn), lambda i, j: (i, j)),
        scratch_shapes=[pltpu.VMEM((tm, d), BF16)],
        compiler_params=_params(("parallel", "arbitrary")),
        name="in_proj",
    )(x, modm, g_pre, w_bf)


def _shift_kernel(z_ref, mu_ref, o_ref, *, grid_mode, period):
    z = z_ref[...]
    rows = z.shape[0]
    t = lax.broadcasted_iota(I32, (rows, 1), 0) % period
    prev = jnp.where(t % (GRID_W if grid_mode else period) != 0, pltpu.roll(z, 1, 0), 0.0)
    nxt = jnp.where(t % (GRID_W if grid_mode else period) != (GRID_W if grid_mode else period) - 1,
                    pltpu.roll(z, rows - 1, 0), 0.0)
    if grid_mode:
        up = jnp.where(t >= GRID_W, pltpu.roll(z, GRID_W, 0), 0.0)
        down = jnp.where(t < period - GRID_W, pltpu.roll(z, rows - GRID_W, 0), 0.0)
        nb = 0.25 * (up + down + prev + nxt)
    else:
        nb = 0.5 * (prev + nxt)
    o_ref[...] = z + mu_ref[...] * (nb - z)


def _shift_mix(z, mu, n_shift, lora_block, grid_mode, period):
    n = z.shape[0]
    tr, tc = 2048, 256
    main_blocks = (n_shift // tc) - 1
    col = lambda i, j: (i, jnp.where(j < main_blocks, j, lora_block))
    return pl.pallas_call(
        functools.partial(_shift_kernel, grid_mode=grid_mode, period=period),
        out_shape=jax.ShapeDtypeStruct((n, n_shift), F32),
        grid=(n // tr, n_shift // tc),
        in_specs=[pl.BlockSpec((tr, tc), col),
                  pl.BlockSpec((1, tc), lambda i, j: (0, j))],
        out_specs=pl.BlockSpec((tr, tc), lambda i, j: (i, j)),
        compiler_params=_params(("parallel", "parallel")),
        name="token_shift",
    )(z, mu)


def _scan_chunk(r, k, v, lora, d, s_ref, w0_ref, w2_ref, a0_ref, a2_ref, kk_ref, ka_ref, forward):
    c = r.shape[0]
    c2 = 2 * c
    first_head = lax.broadcasted_iota(I32, (1, LANES), 1) < HEAD_A
    dir_rows = (lax.broadcasted_iota(I32, (LANES, 1), 0) // HEAD_A) == d
    w2 = jnp.where(dir_rows, w2_ref[...], 0.0)
    a2 = jnp.where(dir_rows, a2_ref[...], 0.0)
    wl = w0_ref[d:d + 1, :] + _dot_hi(jnp.tanh(lora[:, :LANES]), w2)
    logw = -DECAY_SCALE * _sigmoid(wl)
    a = _sigmoid(a0_ref[d:d + 1, :] + _dot_hi(lora[:, LANES:], a2))
    kkr = k * kk_ref[...]
    kk = kkr / jnp.maximum(jnp.sqrt(_head_sum(kkr * kkr, first_head)), 1e-12)
    kd = k * (1.0 + (a - 1.0) * ka_ref[...])

    ti = lax.broadcasted_iota(I32, (c, c), 0)
    tj = lax.broadcasted_iota(I32, (c, c), 1)
    cum = jnp.where((tj <= ti) if forward else (tj >= ti), 1.0, 0.0)
    cl = _dot_hi(cum, logw)
    total = cl[c - 1:c, :] if forward else cl[0:1, :]
    g = jnp.exp(cl)
    gi = jnp.exp(-cl)
    kp = kk * jnp.exp(cl - logw)
    bt = kk * a * gi
    kt = kd * gi
    rt = r * g

    def stack(x):
        return jnp.concatenate([jnp.where(first_head, x, 0.0), jnp.where(first_head, 0.0, x)],
                               axis=0).astype(BF16)

    lhs = jnp.concatenate([stack(kp), stack(rt)], axis=0)
    rhs = jnp.concatenate([stack(bt), stack(kt)], axis=0)
    v2 = stack(v)
    res = _dot_nt(lhs, rhs)
    row = lax.broadcasted_iota(I32, (c2, c2), 0)
    col = lax.broadcasted_iota(I32, (c2, c2), 1)
    strict = (row > col) if forward else (row < col)
    incl = (row >= col) if forward else (row <= col)
    ab = jnp.where(strict, res[:c2, :c2], 0.0)
    ak = jnp.where(strict, res[:c2, c2:], 0.0)
    gb_gk = jnp.concatenate([jnp.where(incl, res[c2:, :c2], 0.0), jnp.where(incl, res[c2:, c2:], 0.0)],
                            axis=1).astype(BF16)

    eye = jnp.where(row == col, 1.0, 0.0)
    tinv = eye - ab
    pw = -ab
    steps = c.bit_length() - 2
    for _ in range(steps):
        pw = _dot(pw, pw)
        tinv = tinv + _dot(tinv, pw)

    s2 = s_ref[...]
    pr = _dot_nt(lhs, s2)
    u0 = -pr[:c2, :] - _dot(ak, v2)
    u2 = _dot(tinv, u0)
    uv = jnp.concatenate([u2.astype(BF16), v2], axis=0)
    y2 = pr[c2:, :] + _dot(gb_gk, uv)
    s_ref[...] = (s2 + _dot_tn(uv, rhs)) * jnp.exp(total)
    return y2[:c, :] + y2[c:, :]


def _scan_kernel(rf, kf, vf, lf, rb, kb, vb, lb, w0, w2, a0, a2, kk, ka, s0f, s0b,
                 yf, yb, sf, sb, s2f, s2b, *, zero_init):
    c = pl.program_id(2)

    @pl.when(c == 0)
    def _():
        if zero_init:
            s2f[...] = jnp.zeros_like(s2f)
            s2b[...] = jnp.zeros_like(s2b)
        else:
            s2f[...] = s0f[0, 0]
            s2b[...] = s0b[0, 0]

    yf[0] = _scan_chunk(rf[0], kf[0], vf[0], lf[0], 0, s2f, w0, w2, a0, a2, kk, ka, True)
    yb[0] = _scan_chunk(rb[0], kb[0], vb[0], lb[0], 1, s2b, w0, w2, a0, a2, kk, ka, False)

    @pl.when(c == pl.num_programs(2) - 1)
    def _():
        sf[0, 0] = s2f[...]
        sb[0, 0] = s2b[...]


def _rwkv_scan(zs, w0, w2r, a0, a2r, k_k, k_a, s0f, s0b, zero_init):
    b, t, _ = zs.shape
    d_a = w0.shape[1]
    pairs = d_a // LANES
    c = SCAN_CHUNK
    nc = t // c
    lora_blk = (4 * d_a) // (2 * LANES)
    fw = lambda off: (lambda i, p, j: (i, j, off + p))
    bw = lambda off: (lambda i, p, j: (i, nc - 1 - j, off + p))
    par = lambda i, p, j: (0, p)
    st = lambda i, p, j: (i, p, 0, 0)
    blk = (1, c, LANES)
    lblk = (1, c, 2 * LANES)
    sblk = (1, 1, LANES, LANES)
    yshape = jax.ShapeDtypeStruct((b, t, d_a), F32)
    sshape = jax.ShapeDtypeStruct((b, pairs, LANES, LANES), F32)
    return pl.pallas_call(
        functools.partial(_scan_kernel, zero_init=zero_init),
        out_shape=(yshape, yshape, sshape, sshape),
        grid=(b, pairs, nc),
        in_specs=[pl.BlockSpec(blk, fw(0)), pl.BlockSpec(blk, fw(pairs)), pl.BlockSpec(blk, fw(2 * pairs)),
                  pl.BlockSpec(lblk, lambda i, p, j: (i, j, lora_blk)),
                  pl.BlockSpec(blk, bw(0)), pl.BlockSpec(blk, bw(pairs)), pl.BlockSpec(blk, bw(2 * pairs)),
                  pl.BlockSpec(lblk, lambda i, p, j: (i, nc - 1 - j, lora_blk)),
                  pl.BlockSpec((2, LANES), par), pl.BlockSpec((LANES, LANES), par),
                  pl.BlockSpec((2, LANES), par), pl.BlockSpec((LANES, LANES), par),
                  pl.BlockSpec((1, LANES), par), pl.BlockSpec((1, LANES), par),
                  pl.BlockSpec(sblk, st), pl.BlockSpec(sblk, st)],
        out_specs=(pl.BlockSpec(blk, fw(0)), pl.BlockSpec(blk, bw(0)),
                   pl.BlockSpec(sblk, st), pl.BlockSpec(sblk, st)),
        scratch_shapes=[pltpu.VMEM((LANES, LANES), F32), pltpu.VMEM((LANES, LANES), F32)],
        compiler_params=_params(("parallel", "parallel", "arbitrary")),
        name="rwkv7_scan",
    )(zs, zs, zs, zs, zs, zs, zs, zs, w0, w2r, a0, a2r, k_k, k_a, s0f, s0b)


def _post_kernel(yf, yb, r, k, v, g, lora, u, vg, a0, a2, ka, rk, lxw, lxb, lvw, lvb, ws, bst, o_ref):
    tm = yf.shape[0]
    d_a = yf.shape[1]
    first_head = lax.broadcasted_iota(I32, (1, LANES), 1) < HEAD_A
    dir_row = lax.broadcasted_iota(I32, (LANES, 1), 0) // HEAD_A
    la = lora[:, LANES:]
    a_sum = jnp.zeros((tm, d_a), F32)
    for d in range(2):
        a_sum = a_sum + _sigmoid(a0[d:d + 1, :] + _dot_hi(la, jnp.where(dir_row == d, a2[...], 0.0)))
    rkk = r[...] * k[...] * (2.0 + (a_sum - 2.0) * ka[...]) * rk[...]
    y = yf[...] + yb[...]
    inv = 1.0 / HEAD_A
    for j in range(d_a // LANES):
        sl = slice(j * LANES, (j + 1) * LANES)
        yj = y[:, sl]
        mu = _head_sum(yj, first_head) * inv
        dl = yj - mu
        var = _head_sum(dl * dl, first_head) * inv
        yn = dl * lax.rsqrt(var + GN_EPS) * lxw[:, sl] + lxb[:, sl]
        bonus = _head_sum(rkk[:, sl], first_head) * v[:, sl]
        o_ref[:, sl] = ((yn + bonus) * _sigmoid(g[:, sl])).astype(BF16)

    uu = _gelu(u[...])
    vv = _gelu(vg[...])
    mu = jnp.mean(vv, axis=-1, keepdims=True)
    dv = vv - mu
    vn = dv * lax.rsqrt(jnp.mean(dv * dv, axis=-1, keepdims=True) + EPS) * lvw[...] + lvb[...]
    for ch in range(tm // GMLP_CHUNK):
        rows = slice(ch * GMLP_CHUNK, (ch + 1) * GMLP_CHUNK)
        for h in range(ws.shape[0]):
            cols = slice(h * LANES, (h + 1) * LANES)
            sp = _dot(ws[h], vn[rows, cols]) + bst[:, h:h + 1]
            o_ref[rows, d_a + h * LANES:d_a + (h + 1) * LANES] = (uu[rows, cols] * sp).astype(BF16)


def _post_mix(yf, yb, zs, z, a0, a2r, k_a, r_k, lxw, lxb, lvw, lvb, w_s, b_st):
    n, d_a = yf.shape
    tm = 256
    wide = lambda j: (lambda i: (i, j))
    full = lambda shape: pl.BlockSpec(shape, lambda i: (0,) * len(shape))
    lora_blk = (4 * d_a) // (2 * LANES)
    return pl.pallas_call(
        _post_kernel,
        out_shape=jax.ShapeDtypeStruct((n, 2 * d_a), BF16),
        grid=(n // tm,),
        in_specs=[pl.BlockSpec((tm, d_a), wide(0)), pl.BlockSpec((tm, d_a), wide(0)),
                  pl.BlockSpec((tm, d_a), wide(0)), pl.BlockSpec((tm, d_a), wide(1)),
                  pl.BlockSpec((tm, d_a), wide(2)), pl.BlockSpec((tm, d_a), wide(3)),
                  pl.BlockSpec((tm, 2 * LANES), wide(lora_blk)),
                  pl.BlockSpec((tm, d_a), wide(4)), pl.BlockSpec((tm, d_a), wide(5)),
                  full((2, d_a)), full((LANES, d_a)), full((1, d_a)), full((1, d_a)),
                  full((1, d_a)), full((1, d_a)), full((1, d_a)), full((1, d_a)),
                  full(w_s.shape), full(b_st.shape)],
        out_specs=pl.BlockSpec((tm, 2 * d_a), wide(0)),
        compiler_params=_params(("parallel",)),
        name="mix_post",
    )(yf, yb, zs, zs, zs, zs, zs, z, z, a0, a2r, k_a, r_k, lxw, lxb, lvw, lvb, w_s, b_st)


def _rms(x, g):
    return x * lax.rsqrt(jnp.mean(x * x, axis=-1, keepdims=True) + EPS) * g


def _out_proj_kernel(a_ref, w_ref, x_ref, mod_ref, gpost_ref, gpre_ref, x1_ref, h2_ref):
    o = jnp.dot(a_ref[...], w_ref[...], preferred_element_type=F32)
    x1 = x_ref[...] + mod_ref[0, 2:3, :] * _rms(o, gpost_ref[...])
    x1_ref[...] = x1
    h2_ref[...] = _rms(x1, gpre_ref[...]) * (1.0 + mod_ref[0, 4:5, :]) + mod_ref[0, 3:4, :]


def _out_proj(yag, w_bf, x, modm, g_post, g_pre2, group0, rows_per_group):
    n, d = x.shape
    tm = 256
    grp = lambda i: (group0 + (i * tm) // rows_per_group, 0, 0)
    row = lambda i: (i, 0)
    fix = lambda i: (0, 0)
    shp = jax.ShapeDtypeStruct((n, d), F32)
    return pl.pallas_call(
        _out_proj_kernel,
        out_shape=(shp, shp),
        grid=(n // tm,),
        in_specs=[pl.BlockSpec((tm, yag.shape[1]), row), pl.BlockSpec(w_bf.shape, fix),
                  pl.BlockSpec((tm, d), row), pl.BlockSpec((1, 6, d), grp),
                  pl.BlockSpec((1, d), fix), pl.BlockSpec((1, d), fix)],
        out_specs=(pl.BlockSpec((tm, d), row), pl.BlockSpec((tm, d), row)),
        compiler_params=_params(("parallel",)),
        name="out_proj",
    )(yag, w_bf, x, modm, g_post, g_pre2)


def _query_kernel(h_ref, wq_ref, sk_ref, o_ref):
    q = jnp.dot(h_ref[...].astype(BF16), wq_ref[...], preferred_element_type=F32)
    for g in range(sk_ref.shape[0]):
        o_ref[g * N_KEYS:(g + 1) * N_KEYS, :] = _dot_nt(sk_ref[g], q[:, g * LANES:(g + 1) * LANES])


def _query_scores(h2, wq_bf, sk):
    n, d = h2.shape
    tm = 256
    groups = sk.shape[0]
    return pl.pallas_call(
        _query_kernel,
        out_shape=jax.ShapeDtypeStruct((groups * N_KEYS, n), F32),
        grid=(n // tm,),
        in_specs=[pl.BlockSpec((tm, d), lambda i: (i, 0)),
                  pl.BlockSpec(wq_bf.shape, lambda i: (0, 0)),
                  pl.BlockSpec(sk.shape, lambda i: (0, 0, 0))],
        out_specs=pl.BlockSpec((groups * N_KEYS, tm), lambda i: (0, i)),
        compiler_params=_params(("parallel",)),
        name="peer_query",
    )(h2, wq_bf, sk)


def _top16(s):
    rows = s.shape[0]
    iota = lax.broadcasted_iota(I32, s.shape, 0)
    vals, idxs = [], []
    for _ in range(TOPK):
        m = jnp.max(s, axis=0, keepdims=True)
        i = jnp.min(jnp.where(s == m, iota, rows), axis=0, keepdims=True)
        vals.append(m)
        idxs.append(i)
        s = jnp.where(iota == i, -jnp.inf, s)
    return jnp.concatenate(vals, axis=0), jnp.concatenate(idxs, axis=0)


def _topk_kernel(s_ref, idx_ref, gate_ref):
    def head(h, carry):
        base = pl.multiple_of(h * (2 * N_KEYS), 2 * N_KEYS)
        v1, i1 = _top16(s_ref[pl.ds(base, N_KEYS), :])
        v2, i2 = _top16(s_ref[pl.ds(base + N_KEYS, N_KEYS), :])
        cand = jnp.concatenate([v1[i:i + 1, :] + v2 for i in range(TOPK)], axis=0)
        top_s, pos = _top16(cand)
        pi = pos // TOPK
        pj = pos % TOPK
        e1 = jnp.zeros_like(pos)
        e2 = jnp.zeros_like(pos)
        for i in range(TOPK):
            e1 = e1 + jnp.where(pi == i, i1[i:i + 1, :], 0)
            e2 = e2 + jnp.where(pj == i, i2[i:i + 1, :], 0)
        ex = jnp.exp(top_s - top_s[0:1, :])
        out = pl.multiple_of(h * TOPK, TOPK)
        idx_ref[pl.ds(out, TOPK), :] = e1 * N_KEYS + e2
        gate_ref[pl.ds(out, TOPK), :] = ex / jnp.sum(ex, axis=0, keepdims=True)
        return carry

    lax.fori_loop(0, PK_HEADS, head, 0)


def _peer_topk(scores):
    rows, n = scores.shape
    tt = 256
    out_rows = PK_HEADS * TOPK
    return pl.pallas_call(
        _topk_kernel,
        out_shape=(jax.ShapeDtypeStruct((out_rows, n), I32), jax.ShapeDtypeStruct((out_rows, n), F32)),
        grid=(n // tt,),
        in_specs=[pl.BlockSpec((rows, tt), lambda i: (0, i))],
        out_specs=(pl.BlockSpec((out_rows, tt), lambda i: (0, i)),
                   pl.BlockSpec((out_rows, tt), lambda i: (0, i))),
        compiler_params=_params(("parallel",)),
        name="peer_topk",
    )(scores)


PEER_TOKENS = 16


def _peer_kernel(idx_ref, x_ref, g_ref, eu_ref, ev_ref, o_ref, ubuf, vbuf, sem):
    n_sel = ubuf.shape[1]

    def issue(t, slot):
        def body(e, carry):
            row = idx_ref[t, e]
            pltpu.make_async_copy(eu_ref.at[pl.ds(row, 1), :], ubuf.at[slot, pl.ds(e, 1), :],
                                  sem.at[0, slot]).start()
            pltpu.make_async_copy(ev_ref.at[pl.ds(row, 1), :], vbuf.at[slot, pl.ds(e, 1), :],
                                  sem.at[1, slot]).start()
            return carry
        lax.fori_loop(0, n_sel, body, 0, unroll=8)

    def wait(slot):
        pltpu.make_async_copy(eu_ref.at[pl.ds(0, n_sel), :], ubuf.at[slot], sem.at[0, slot]).wait()
        pltpu.make_async_copy(ev_ref.at[pl.ds(0, n_sel), :], vbuf.at[slot], sem.at[1, slot]).wait()

    issue(0, 0)
    gates = g_ref[...].T
    for t in range(PEER_TOKENS):
        slot = t % 2
        if t + 1 < PEER_TOKENS:
            issue(t + 1, 1 - slot)
        wait(slot)
        act = jnp.sum(ubuf[slot] * x_ref[t:t + 1, :], axis=-1, keepdims=True)
        coef = _gelu(act) * gates[:, t:t + 1]
        o_ref[t:t + 1, :] = jnp.sum(coef * vbuf[slot], axis=0, keepdims=True)


def _peer_experts(idx, x, gates, eu, ev):
    n, d = x.shape
    n_sel = idx.shape[1]
    tt = PEER_TOKENS
    return pl.pallas_call(
        _peer_kernel,
        out_shape=jax.ShapeDtypeStruct((n, d), F32),
        grid=(n // tt,),
        in_specs=[pl.BlockSpec((tt, n_sel), lambda i: (i, 0), memory_space=pltpu.SMEM),
                  pl.BlockSpec((tt, d), lambda i: (i, 0)),
                  pl.BlockSpec((tt, n_sel), lambda i: (i, 0)),
                  pl.BlockSpec(memory_space=pl.ANY),
                  pl.BlockSpec(memory_space=pl.ANY)],
        out_specs=pl.BlockSpec((tt, d), lambda i: (i, 0)),
        scratch_shapes=[pltpu.VMEM((2, n_sel, d), F32), pltpu.VMEM((2, n_sel, d), F32),
                        pltpu.SemaphoreType.DMA((2, 2))],
        compiler_params=pltpu.CompilerParams(dimension_semantics=("arbitrary",),
                                             vmem_limit_bytes=VMEM_LIMIT, disable_bounds_checks=True),
        name="peer_experts",
    )(idx, x, gates, eu, ev)


def _final_kernel(x1_ref, p_ref, mod_ref, g_ref, o_ref):
    o_ref[...] = x1_ref[...] + mod_ref[0, 5:6, :] * _rms(p_ref[...], g_ref[...])


def _final(x1, peer, modm, g_post2, group0, rows_per_group):
    n, d = x1.shape
    tm = 512
    row = lambda i: (i, 0)
    return pl.pallas_call(
        _final_kernel,
        out_shape=jax.ShapeDtypeStruct((n, d), F32),
        grid=(n // tm,),
        in_specs=[pl.BlockSpec((tm, d), row), pl.BlockSpec((tm, d), row),
                  pl.BlockSpec((1, 6, d), lambda i: (group0 + (i * tm) // rows_per_group, 0, 0)),
                  pl.BlockSpec((1, d), lambda i: (0, 0))],
        out_specs=pl.BlockSpec((tm, d), row),
        compiler_params=_params(("parallel",)),
        name="final_residual",
    )(x1, peer, modm, g_post2)


def _block_diag_state(s):
    b, h = s.shape[:2]
    s = s.reshape(b, h // 2, 2, HEAD_A, HEAD_A)
    z = jnp.zeros_like(s[:, :, 0])
    return jnp.concatenate([jnp.concatenate([s[:, :, 0], z], axis=-1),
                            jnp.concatenate([z, s[:, :, 1]], axis=-1)], axis=-2)


def _head_states(s2):
    b, p = s2.shape[:2]
    return jnp.stack([s2[:, :, :HEAD_A, :HEAD_A], s2[:, :, HEAD_A:, HEAD_A:]],
                     axis=2).reshape(b, 2 * p, HEAD_A, HEAD_A)


def _layer(x3, modm, group0, grid_mode, s0f, s0b, w):
    b, t, d = x3.shape
    n = b * t
    x = x3.reshape(n, d)
    rows_per_group = t if grid_mode else n
    z = _in_proj(x, modm, w["g_pre1"], w["w_in"], group0, rows_per_group)
    zs = _shift_mix(z, w["mu_shift"], w["n_shift"], w["lora_block"], grid_mode, t)
    zero_init = s0f is None
    if zero_init:
        s0f = s0b = jnp.zeros((b, w["w0"].shape[1] // LANES, LANES, LANES), F32)
    yf, yb, sf, sb = _rwkv_scan(zs.reshape(b, t, -1), w["w0"], w["w2"], w["a0"], w["a2"], w["k_k"], w["k_a"],
                                s0f, s0b, zero_init)
    d_a = yf.shape[-1]
    yag = _post_mix(yf.reshape(n, d_a), yb.reshape(n, d_a), zs, z, w["a0"], w["a2"], w["k_a"], w["r_k"],
                    w["ln_x_w"], w["ln_x_b"], w["ln_v_w"], w["ln_v_b"], w["w_s"], w["b_st"])
    x1, h2 = _out_proj(yag, w["w_out"], x, modm, w["g_post1"], w["g_pre2"], group0, rows_per_group)
    scores = _query_scores(h2, w["w_query"], w["sub_keys"])
    idx_t, gates_t = _peer_topk(scores)
    peer = _peer_experts(idx_t.T, h2, gates_t.T, w["expert_u"], w["expert_v"])
    out = _final(x1, peer, modm, w["g_post2"], group0, rows_per_group)
    return out.reshape(b, t, d), sf, sb


def kernel(x_prompt, x_sample, c, state_fwd, state_bwd, c_ctx, w_ada, b_ada, g_pre1, g_post1, g_pre2, g_post2,
           w_in, mu_shift, w0, w2, a0, a2, k_k, k_a, r_k, ln_x_w, ln_x_b, ln_v_w, ln_v_b, w_s, b_s, w_out,
           w_query, sub_keys, expert_u, expert_v):
    depth = w_in.shape[0]
    d = x_prompt.shape[-1]
    d_a = w0.shape[-1]
    n_shift = mu_shift.shape[-1]
    dec_b = x_sample.shape[0]
    cvec = jnp.concatenate([c_ctx[None], c, jnp.zeros((8 - 1 - dec_b, d), F32)], axis=0)
    xp, xs = x_prompt, x_sample
    new_f, new_b = [], []
    for l in range(depth):
        wl_in = w_in[l]
        row = lambda a: a[l].reshape(1, -1)
        w = {
            "w_in": jnp.concatenate([wl_in[:, :4 * d_a], wl_in[:, n_shift:], wl_in[:, 4 * d_a:n_shift]],
                                    axis=1).astype(BF16),
            "n_shift": n_shift,
            "lora_block": (wl_in.shape[1] - (n_shift - 4 * d_a)) // (2 * LANES),
            "mu_shift": row(mu_shift),
            "g_pre1": row(g_pre1), "g_post1": row(g_post1), "g_pre2": row(g_pre2), "g_post2": row(g_post2),
            "w0": w0[l], "w2": w2[l].reshape(-1, d_a), "a0": a0[l], "a2": a2[l].reshape(-1, d_a),
            "k_k": row(k_k), "k_a": row(k_a), "r_k": row(r_k),
            "ln_x_w": row(ln_x_w), "ln_x_b": row(ln_x_b), "ln_v_w": row(ln_v_w), "ln_v_b": row(ln_v_b),
            "w_s": w_s[l].astype(BF16), "b_st": b_s[l].T,
            "w_out": w_out[l].astype(BF16), "w_query": w_query[l].astype(BF16),
            "sub_keys": jnp.swapaxes(sub_keys[l], 0, 1).reshape(-1, N_KEYS, sub_keys.shape[-1]).astype(BF16),
            "expert_u": expert_u[l], "expert_v": expert_v[l],
        }
        modm = _modulation(cvec, w_ada[l], b_ada[l].reshape(1, -1)).reshape(8, 6, d)
        xp, sf, sb = _layer(xp, modm, 0, False, None, None, w)
        new_f.append(_head_states(sf))
        new_b.append(_head_states(sb))
        xs, _, _ = _layer(xs, modm, 1, True, _block_diag_state(state_fwd[:, l]),
                          _block_diag_state(state_bwd[:, l]), w)
    return (xp, xs, jnp.stack(new_f, axis=1), jnp.stack(new_b, axis=1))
```

```python
import functools

import jax
import jax.numpy as jnp
from jax import lax
from jax.experimental import pallas as pl
from jax.experimental.pallas import tpu as pltpu
from jax.experimental.pallas import tpu_sc as plsc

F32 = jnp.float32
BF16 = jnp.bfloat16
I32 = jnp.int32

EPS = 1e-6
GN_EPS = 64e-5
HEAD_A = 64
LANES = 128
GRID_W = 64
GMLP_CHUNK = 128
PK_HEADS = 8
N_KEYS = 128
TOPK = 16
SCAN_CHUNK = 64
SCAN_PAIRS = 8
DECAY_SCALE = 0.6065306597126334
VMEM_LIMIT = 48 * 1024 * 1024


def _params(sem):
    return pltpu.CompilerParams(dimension_semantics=sem, vmem_limit_bytes=VMEM_LIMIT)


def _sigmoid(x):
    return 1.0 / (1.0 + jnp.exp(-x))


def _gelu(x):
    return 0.5 * x * (1.0 + jnp.tanh(0.7978845608028654 * (x + 0.044715 * (x * x * x))))


def _dot(a, b):
    return jnp.dot(a.astype(BF16), b.astype(BF16), preferred_element_type=F32)


def _dot_nt(a, b):
    return lax.dot_general(a.astype(BF16), b.astype(BF16), (((1,), (1,)), ((), ())),
                           preferred_element_type=F32)


def _dot_tn(a, b):
    return lax.dot_general(a.astype(BF16), b.astype(BF16), (((0,), (0,)), ((), ())),
                           preferred_element_type=F32)


def _split(x):
    hi = x.astype(BF16)
    return hi, (x - hi.astype(F32)).astype(BF16)


def _dot_x3(a, b):
    a_hi, a_lo = _split(a)
    b_hi, b_lo = _split(b)
    dot = functools.partial(jnp.dot, preferred_element_type=F32)
    return dot(a_hi, b_hi) + (dot(a_lo, b_hi) + dot(a_hi, b_lo))


def _dot_split_rhs(a_bf, b):
    b_hi, b_mid = _split(b)
    b_lo = (b - b_hi.astype(F32) - b_mid.astype(F32)).astype(BF16)
    dot = functools.partial(jnp.dot, preferred_element_type=F32)
    return dot(a_bf, b_hi) + (dot(a_bf, b_mid) + dot(a_bf, b_lo))


def _head_sum(x, first_head):
    s_a = jnp.sum(jnp.where(first_head, x, 0.0), axis=-1, keepdims=True)
    s_b = jnp.sum(jnp.where(first_head, 0.0, x), axis=-1, keepdims=True)
    return jnp.where(first_head, s_a, s_b)


def _mod_kernel(c_ref, w_ref, b_ref, o_ref):
    c = c_ref[...]
    o_ref[...] = _dot(c * _sigmoid(c), w_ref[...]) + b_ref[...]


def _modulation(cvec, w_ada, b_ada):
    rows, d = cvec.shape
    n = w_ada.shape[1]
    tn = 1024
    return pl.pallas_call(
        _mod_kernel,
        out_shape=jax.ShapeDtypeStruct((rows, n), F32),
        grid=(n // tn,),
        in_specs=[pl.BlockSpec((rows, d), lambda j: (0, 0)),
                  pl.BlockSpec((d, tn), lambda j: (0, j)),
                  pl.BlockSpec((1, tn), lambda j: (0, j))],
        out_specs=pl.BlockSpec((rows, tn), lambda j: (0, j)),
        compiler_params=_params(("parallel",)),
        name="adaln_mod",
    )(cvec, w_ada, b_ada)


def _in_proj_kernel(x_ref, mod_ref, g_ref, w_ref, o_ref, h_ref):
    @pl.when(pl.program_id(1) == 0)
    def _():
        x = x_ref[...]
        y = x * lax.rsqrt(jnp.mean(x * x, axis=-1, keepdims=True) + EPS) * g_ref[...]
        h_ref[...] = (y * (1.0 + mod_ref[0, 1:2, :]) + mod_ref[0, 0:1, :]).astype(BF16)

    o_ref[...] = jnp.dot(h_ref[...], w_ref[...], preferred_element_type=F32)


def _in_proj(x, modm, g_pre, w_bf, group0, rows_per_group):
    n, d = x.shape
    p = w_bf.shape[1]
    tm, tn = 512, 1280
    grp = lambda i, j: (group0 + (i * tm) // rows_per_group, 0, 0)
    return pl.pallas_call(
        _in_proj_kernel,
        out_shape=jax.ShapeDtypeStruct((n, p), F32),
        grid=(n // tm, p // tn),
        in_specs=[pl.BlockSpec((tm, d), lambda i, j: (i, 0)),
                  pl.BlockSpec((1, 6, d), grp),
                  pl.BlockSpec((1, d), lambda i, j: (0, 0)),
                  pl.BlockSpec((d, tn), lambda i, j: (0, j))],
        out_specs=pl.BlockSpec((tm, tn), lambda i, j: (i, j)),
        scratch_shapes=[pltpu.VMEM((tm, d), BF16)],
        compiler_params=_params(("parallel", "arbitrary")),
        name="in_proj",
    )(x, modm, g_pre, w_bf)


def _shift_kernel(z_ref, mu_ref, o_ref, *, grid_mode, period):
    z = z_ref[...]
    rows = z.shape[0]
    t = lax.broadcasted_iota(I32, (rows, 1), 0) % period
    prev = jnp.where(t % (GRID_W if grid_mode else period) != 0, pltpu.roll(z, 1, 0), 0.0)
    nxt = jnp.where(t % (GRID_W if grid_mode else period) != (GRID_W if grid_mode else period) - 1,
                    pltpu.roll(z, rows - 1, 0), 0.0)
    if grid_mode:
        up = jnp.where(t >= GRID_W, pltpu.roll(z, GRID_W, 0), 0.0)
        down = jnp.where(t < period - GRID_W, pltpu.roll(z, rows - GRID_W, 0), 0.0)
        nb = 0.25 * (up + down + prev + nxt)
    else:
        nb = 0.5 * (prev + nxt)
    o_ref[...] = z + mu_ref[...] * (nb - z)


def _shift_mix(z, mu, n_shift, lora_block, grid_mode, period):
    n = z.shape[0]
    tr, tc = 2048, 256
    main_blocks = (n_shift // tc) - 1
    col = lambda i, j: (i, jnp.where(j < main_blocks, j, lora_block))
    return pl.pallas_call(
        functools.partial(_shift_kernel, grid_mode=grid_mode, period=period),
        out_shape=jax.ShapeDtypeStruct((n, n_shift), F32),
        grid=(n // tr, n_shift // tc),
        in_specs=[pl.BlockSpec((tr, tc), col),
                  pl.BlockSpec((1, tc), lambda i, j: (0, j))],
        out_specs=pl.BlockSpec((tr, tc), lambda i, j: (i, j)),
        compiler_params=_params(("parallel", "parallel")),
        name="token_shift",
    )(z, mu)


def _scan_chunks(chains):
    c = chains[0][0].shape[0]
    c2 = 2 * c
    n = len(chains)
    fwd = [ch[9] for ch in chains]
    first_head = lax.broadcasted_iota(I32, (1, LANES), 1) < HEAD_A
    row = lax.broadcasted_iota(I32, (c2, c2), 0)
    col = lax.broadcasted_iota(I32, (c2, c2), 1)
    eye = jnp.where(row == col, 1.0, 0.0)

    def stack(x):
        return jnp.concatenate([jnp.where(first_head, x, 0.0), jnp.where(first_head, 0.0, x)],
                               axis=0).astype(BF16)

    lhs, rhs, v2, total = [], [], [], []
    for r, k, v, logw, cl, a, k_k, k_a, _, forward in chains:
        kkr = k * k_k
        kk = kkr / jnp.maximum(jnp.sqrt(_head_sum(kkr * kkr, first_head)), 1e-12)
        kd = k * (1.0 + (a - 1.0) * k_a)
        gi = jnp.exp(-cl)
        lhs.append(jnp.concatenate([stack(kk * jnp.exp(cl - logw)), stack(r * jnp.exp(cl))], axis=0))
        rhs.append(jnp.concatenate([stack(kk * a * gi), stack(kd * gi)], axis=0))
        v2.append(stack(v))
        total.append(cl[c - 1:c, :] if forward else cl[0:1, :])

    res = [_dot_nt(lhs[i], rhs[i]) for i in range(n)]
    pr = [_dot_nt(lhs[i], chains[i][8]) for i in range(n)]
    strict = [(row > col) if f else (row < col) for f in fwd]
    incl = [(row >= col) if f else (row <= col) for f in fwd]
    ab = [jnp.where(strict[i], res[i][:c2, :c2], 0.0) for i in range(n)]
    ak = [jnp.where(strict[i], res[i][:c2, c2:], 0.0) for i in range(n)]
    gb_gk = [jnp.concatenate([jnp.where(incl[i], res[i][c2:, :c2], 0.0),
                              jnp.where(incl[i], res[i][c2:, c2:], 0.0)], axis=1).astype(BF16) for i in range(n)]
    akv = [_dot(ak[i], v2[i]) for i in range(n)]

    tinv = [eye - ab[i] for i in range(n)]
    pw = [-ab[i] for i in range(n)]
    for _ in range(c.bit_length() - 2):
        pw = [_dot(pw[i], pw[i]) for i in range(n)]
        tinv = [tinv[i] + _dot(tinv[i], pw[i]) for i in range(n)]

    u2 = [_dot(tinv[i], -pr[i][:c2, :] - akv[i]) for i in range(n)]
    uv = [jnp.concatenate([u2[i].astype(BF16), v2[i]], axis=0) for i in range(n)]
    y2 = [pr[i][c2:, :] + _dot(gb_gk[i], uv[i]) for i in range(n)]
    s_new = [(chains[i][8] + _dot_tn(uv[i], rhs[i])) * jnp.exp(total[i]) for i in range(n)]
    return [(y2[i][:c, :] + y2[i][c:, :], s_new[i]) for i in range(n)]


def _scan_direction(r_ref, k_ref, v_ref, l_ref, d, s_ref, w0, w2, a0, a2, kk, ka, forward):
    lora = l_ref[0]
    c = lora.shape[0]
    dir_rows = (lax.broadcasted_iota(I32, (LANES, 1), 0) // HEAD_A) == d
    wl = w0[d:d + 1, :] + _dot_x3(jnp.tanh(lora[:, :LANES]), jnp.where(dir_rows, w2[...], 0.0))
    logw = -DECAY_SCALE * _sigmoid(wl)
    a = _sigmoid(a0[d:d + 1, :] + _dot_x3(lora[:, LANES:], jnp.where(dir_rows, a2[...], 0.0)))
    ti = lax.broadcasted_iota(I32, (c, c), 0)
    tj = lax.broadcasted_iota(I32, (c, c), 1)
    cum = jnp.where((tj <= ti) if forward else (tj >= ti), 1.0, 0.0).astype(BF16)
    cl = _dot_split_rhs(cum, logw)
    chains = []
    for p in range(s_ref.shape[0]):
        sl = slice(p * LANES, (p + 1) * LANES)
        chains.append((r_ref[0, :, sl], k_ref[0, :, sl], v_ref[0, :, sl], logw[:, sl], cl[:, sl],
                       a[:, sl], kk[:, sl], ka[:, sl], s_ref[p], forward))
    return chains


def _scan_store(out, y_ref, s_ref):
    for p, (y, s_new) in enumerate(out):
        y_ref[0, :, p * LANES:(p + 1) * LANES] = y
        s_ref[p] = s_new


def _scan_kernel(rf, kf, vf, lf, rb, kb, vb, lb, w0, w2, a0, a2, kk, ka, s0f, s0b,
                 yf, yb, sf, sb, s2f, s2b, *, zero_init):
    c = pl.program_id(2)

    @pl.when(c == 0)
    def _():
        if zero_init:
            s2f[...] = jnp.zeros_like(s2f)
            s2b[...] = jnp.zeros_like(s2b)
        else:
            s2f[...] = s0f[0]
            s2b[...] = s0b[0]

    chains_f = _scan_direction(rf, kf, vf, lf, 0, s2f, w0, w2, a0, a2, kk, ka, True)
    chains_b = _scan_direction(rb, kb, vb, lb, 1, s2b, w0, w2, a0, a2, kk, ka, False)
    out = _scan_chunks(chains_f + chains_b)
    _scan_store(out[:len(chains_f)], yf, s2f)
    _scan_store(out[len(chains_f):], yb, s2b)

    @pl.when(c == pl.num_programs(2) - 1)
    def _():
        sf[0] = s2f[...]
        sb[0] = s2b[...]


def _rwkv_scan(zs, w0, w2r, a0, a2r, k_k, k_a, s0f, s0b, zero_init):
    b, t, _ = zs.shape
    d_a = w0.shape[1]
    pp = SCAN_PAIRS
    wide = pp * LANES
    groups = d_a // wide
    c = SCAN_CHUNK
    nc = t // c
    lora_blk = (4 * d_a) // (2 * LANES)
    fw = lambda off: (lambda i, q, j: (i, j, off + q))
    bw = lambda off: (lambda i, q, j: (i, nc - 1 - j, off + q))
    par = lambda i, q, j: (0, q)
    st = lambda i, q, j: (i, q, 0, 0)
    blk = (1, c, wide)
    lblk = (1, c, 2 * LANES)
    sblk = (1, pp, LANES, LANES)
    yshape = jax.ShapeDtypeStruct((b, t, d_a), F32)
    sshape = jax.ShapeDtypeStruct((b, d_a // LANES, LANES, LANES), F32)
    return pl.pallas_call(
        functools.partial(_scan_kernel, zero_init=zero_init),
        out_shape=(yshape, yshape, sshape, sshape),
        grid=(b, groups, nc),
        in_specs=[pl.BlockSpec(blk, fw(0)), pl.BlockSpec(blk, fw(groups)), pl.BlockSpec(blk, fw(2 * groups)),
                  pl.BlockSpec(lblk, lambda i, q, j: (i, j, lora_blk)),
                  pl.BlockSpec(blk, bw(0)), pl.BlockSpec(blk, bw(groups)), pl.BlockSpec(blk, bw(2 * groups)),
                  pl.BlockSpec(lblk, lambda i, q, j: (i, nc - 1 - j, lora_blk)),
                  pl.BlockSpec((2, wide), par), pl.BlockSpec((LANES, wide), par),
                  pl.BlockSpec((2, wide), par), pl.BlockSpec((LANES, wide), par),
                  pl.BlockSpec((1, wide), par), pl.BlockSpec((1, wide), par),
                  pl.BlockSpec(sblk, st), pl.BlockSpec(sblk, st)],
        out_specs=(pl.BlockSpec(blk, fw(0)), pl.BlockSpec(blk, bw(0)),
                   pl.BlockSpec(sblk, st), pl.BlockSpec(sblk, st)),
        scratch_shapes=[pltpu.VMEM((pp, LANES, LANES), F32), pltpu.VMEM((pp, LANES, LANES), F32)],
        compiler_params=_params(("parallel", "parallel", "arbitrary")),
        name="rwkv7_scan",
    )(zs, zs, zs, zs, zs, zs, zs, zs, w0, w2r, a0, a2r, k_k, k_a, s0f, s0b)


def _post_kernel(yf, yb, r, k, v, g, lora, u, vg, a0, a2, ka, rk, lxw, lxb, lvw, lvb, ws, bst, o_ref):
    tm = yf.shape[0]
    d_a = yf.shape[1]
    first_head = lax.broadcasted_iota(I32, (1, LANES), 1) < HEAD_A
    dir_row = lax.broadcasted_iota(I32, (LANES, 1), 0) // HEAD_A
    la = lora[:, LANES:]
    a_sum = jnp.zeros((tm, d_a), F32)
    for d in range(2):
        a_sum = a_sum + _sigmoid(a0[d:d + 1, :] + _dot_x3(la, jnp.where(dir_row == d, a2[...], 0.0)))
    rkk = r[...] * k[...] * (2.0 + (a_sum - 2.0) * ka[...]) * rk[...]
    y = yf[...] + yb[...]
    inv = 1.0 / HEAD_A
    for j in range(d_a // LANES):
        sl = slice(j * LANES, (j + 1) * LANES)
        yj = y[:, sl]
        mu = _head_sum(yj, first_head) * inv
        dl = yj - mu
        var = _head_sum(dl * dl, first_head) * inv
        yn = dl * lax.rsqrt(var + GN_EPS) * lxw[:, sl] + lxb[:, sl]
        bonus = _head_sum(rkk[:, sl], first_head) * v[:, sl]
        o_ref[:, sl] = ((yn + bonus) * _sigmoid(g[:, sl])).astype(BF16)

    uu = _gelu(u[...])
    vv = _gelu(vg[...])
    mu = jnp.mean(vv, axis=-1, keepdims=True)
    dv = vv - mu
    vn = dv * lax.rsqrt(jnp.mean(dv * dv, axis=-1, keepdims=True) + EPS) * lvw[...] + lvb[...]
    for ch in range(tm // GMLP_CHUNK):
        rows = slice(ch * GMLP_CHUNK, (ch + 1) * GMLP_CHUNK)
        for h in range(ws.shape[0]):
            cols = slice(h * LANES, (h + 1) * LANES)
            sp = _dot(ws[h], vn[rows, cols]) + bst[:, h:h + 1]
            o_ref[rows, d_a + h * LANES:d_a + (h + 1) * LANES] = (uu[rows, cols] * sp).astype(BF16)


def _post_mix(yf, yb, zs, z, a0, a2r, k_a, r_k, lxw, lxb, lvw, lvb, w_s, b_st):
    n, d_a = yf.shape
    tm = 256
    wide = lambda j: (lambda i: (i, j))
    full = lambda shape: pl.BlockSpec(shape, lambda i: (0,) * len(shape))
    lora_blk = (4 * d_a) // (2 * LANES)
    return pl.pallas_call(
        _post_kernel,
        out_shape=jax.ShapeDtypeStruct((n, 2 * d_a), BF16),
        grid=(n // tm,),
        in_specs=[pl.BlockSpec((tm, d_a), wide(0)), pl.BlockSpec((tm, d_a), wide(0)),
                  pl.BlockSpec((tm, d_a), wide(0)), pl.BlockSpec((tm, d_a), wide(1)),
                  pl.BlockSpec((tm, d_a), wide(2)), pl.BlockSpec((tm, d_a), wide(3)),
                  pl.BlockSpec((tm, 2 * LANES), wide(lora_blk)),
                  pl.BlockSpec((tm, d_a), wide(4)), pl.BlockSpec((tm, d_a), wide(5)),
                  full((2, d_a)), full((LANES, d_a)), full((1, d_a)), full((1, d_a)),
                  full((1, d_a)), full((1, d_a)), full((1, d_a)), full((1, d_a)),
                  full(w_s.shape), full(b_st.shape)],
        out_specs=pl.BlockSpec((tm, 2 * d_a), wide(0)),
        compiler_params=_params(("parallel",)),
        name="mix_post",
    )(yf, yb, zs, zs, zs, zs, zs, z, z, a0, a2r, k_a, r_k, lxw, lxb, lvw, lvb, w_s, b_st)


def _rms(x, g):
    return x * lax.rsqrt(jnp.mean(x * x, axis=-1, keepdims=True) + EPS) * g


def _out_proj_kernel(a_ref, w_ref, x_ref, mod_ref, gpost_ref, gpre_ref, x1_ref, h2_ref):
    o = jnp.dot(a_ref[...], w_ref[...], preferred_element_type=F32)
    x1 = x_ref[...] + mod_ref[0, 2:3, :] * _rms(o, gpost_ref[...])
    x1_ref[...] = x1
    h2_ref[...] = _rms(x1, gpre_ref[...]) * (1.0 + mod_ref[0, 4:5, :]) + mod_ref[0, 3:4, :]


def _out_proj(yag, w_bf, x, modm, g_post, g_pre2, group0, rows_per_group):
    n, d = x.shape
    tm = 256
    grp = lambda i: (group0 + (i * tm) // rows_per_group, 0, 0)
    row = lambda i: (i, 0)
    fix = lambda i: (0, 0)
    shp = jax.ShapeDtypeStruct((n, d), F32)
    return pl.pallas_call(
        _out_proj_kernel,
        out_shape=(shp, shp),
        grid=(n // tm,),
        in_specs=[pl.BlockSpec((tm, yag.shape[1]), row), pl.BlockSpec(w_bf.shape, fix),
                  pl.BlockSpec((tm, d), row), pl.BlockSpec((1, 6, d), grp),
                  pl.BlockSpec((1, d), fix), pl.BlockSpec((1, d), fix)],
        out_specs=(pl.BlockSpec((tm, d), row), pl.BlockSpec((tm, d), row)),
        compiler_params=_params(("parallel",)),
        name="out_proj",
    )(yag, w_bf, x, modm, g_post, g_pre2)


def _query_kernel(h_ref, wq_ref, sk_ref, o_ref):
    q = jnp.dot(h_ref[...].astype(BF16), wq_ref[...], preferred_element_type=F32)
    for g in range(sk_ref.shape[0]):
        o_ref[g * N_KEYS:(g + 1) * N_KEYS, :] = _dot_nt(sk_ref[g], q[:, g * LANES:(g + 1) * LANES])


def _query_scores(h2, wq_bf, sk):
    n, d = h2.shape
    tm = 256
    groups = sk.shape[0]
    return pl.pallas_call(
        _query_kernel,
        out_shape=jax.ShapeDtypeStruct((groups * N_KEYS, n), F32),
        grid=(n // tm,),
        in_specs=[pl.BlockSpec((tm, d), lambda i: (i, 0)),
                  pl.BlockSpec(wq_bf.shape, lambda i: (0, 0)),
                  pl.BlockSpec(sk.shape, lambda i: (0, 0, 0))],
        out_specs=pl.BlockSpec((groups * N_KEYS, tm), lambda i: (0, i)),
        compiler_params=_params(("parallel",)),
        name="peer_query",
    )(h2, wq_bf, sk)


def _top16(s):
    rows = s.shape[0]
    iota = lax.broadcasted_iota(I32, s.shape, 0)
    vals, idxs = [], []
    for _ in range(TOPK):
        m = jnp.max(s, axis=0, keepdims=True)
        i = jnp.min(jnp.where(s == m, iota, rows), axis=0, keepdims=True)
        vals.append(m)
        idxs.append(i)
        s = jnp.where(iota == i, -jnp.inf, s)
    return jnp.concatenate(vals, axis=0), jnp.concatenate(idxs, axis=0)


def _topk_kernel(s_ref, idx_ref, gate_ref):
    def head(h, carry):
        base = pl.multiple_of(h * (2 * N_KEYS), 2 * N_KEYS)
        v1, i1 = _top16(s_ref[pl.ds(base, N_KEYS), :])
        v2, i2 = _top16(s_ref[pl.ds(base + N_KEYS, N_KEYS), :])
        cand = jnp.concatenate([v1[i:i + 1, :] + v2 for i in range(TOPK)], axis=0)
        top_s, pos = _top16(cand)
        pi = pos // TOPK
        pj = pos % TOPK
        e1 = jnp.zeros_like(pos)
        e2 = jnp.zeros_like(pos)
        for i in range(TOPK):
            e1 = e1 + jnp.where(pi == i, i1[i:i + 1, :], 0)
            e2 = e2 + jnp.where(pj == i, i2[i:i + 1, :], 0)
        ex = jnp.exp(top_s - top_s[0:1, :])
        out = pl.multiple_of(h * TOPK, TOPK)
        idx_ref[pl.ds(out, TOPK), :] = e1 * N_KEYS + e2
        gate_ref[pl.ds(out, TOPK), :] = ex / jnp.sum(ex, axis=0, keepdims=True)
        return carry

    lax.fori_loop(0, PK_HEADS, head, 0)


def _peer_topk(scores):
    rows, n = scores.shape
    tt = 256
    out_rows = PK_HEADS * TOPK
    return pl.pallas_call(
        _topk_kernel,
        out_shape=(jax.ShapeDtypeStruct((out_rows, n), I32), jax.ShapeDtypeStruct((out_rows, n), F32)),
        grid=(n // tt,),
        in_specs=[pl.BlockSpec((rows, tt), lambda i: (0, i))],
        out_specs=(pl.BlockSpec((out_rows, tt), lambda i: (0, i)),
                   pl.BlockSpec((out_rows, tt), lambda i: (0, i))),
        compiler_params=_params(("parallel",)),
        name="peer_topk",
    )(scores)


SC_CORES = 2
SC_SUBCORES = 16
SC_LANES = 16
PEER_GROUP = 16
PEER_BLOCK = 32


def _peer_experts(idx, x, gates, eu, ev):
    n, d = x.shape
    n_sel = idx.shape[1]
    per_w = n // (SC_CORES * SC_SUBCORES)
    L, G, TB = SC_LANES, PEER_GROUP, PEER_BLOCK
    n_groups = n_sel // G
    n_ch = d // L
    mesh = plsc.VectorSubcoreMesh(core_axis_name="c", subcore_axis_name="s")

    @functools.partial(
        pl.kernel, mesh=mesh, out_type=jax.ShapeDtypeStruct((n, d), F32),
        compiler_params=pltpu.CompilerParams(needs_layout_passes=False),
        scratch_types=[pltpu.VMEM((TB, n_sel), I32), pltpu.VMEM((TB, n_sel), F32),
                       pltpu.VMEM((d,), F32), pltpu.VMEM((d,), F32),
                       pltpu.VMEM((2, G, d), F32), pltpu.VMEM((G, 2 * L), F32), pltpu.VMEM((2 * L,), F32),
                       pltpu.SemaphoreType.DMA((2,))],
        name="peer_experts_sc")
    def k(idx_hbm, x_hbm, g_hbm, eu_hbm, ev_hbm, o_hbm, idx_v, gate_v, x_v, o_v, buf, acc_v, coef_v, sem):
        base = (lax.axis_index("s") * SC_CORES + lax.axis_index("c")) * per_w
        lane = lax.iota(I32, L)
        zero = jnp.zeros((L,), F32)

        def gather(table, t, g, slot):
            return pltpu.make_async_copy(table.at[idx_v.at[t, pl.ds(g * G, G)]], buf.at[slot], sem.at[slot])

        @pl.loop(0, per_w // TB)
        def _(blk):
            tok0 = base + blk * TB
            pltpu.sync_copy(idx_hbm.at[pl.ds(tok0, TB)], idx_v)
            pltpu.sync_copy(g_hbm.at[pl.ds(tok0, TB)], gate_v)

            @pl.loop(0, TB)
            def _(t):
                tok = tok0 + t
                pltpu.sync_copy(x_hbm.at[tok], x_v)
                gather(eu_hbm, t, 0, 0).start()

                @plsc.parallel_loop(0, n_ch)
                def _(ch):
                    o_v[pl.ds(ch * L, L)] = zero

                for g in range(n_groups):
                    gather(ev_hbm, t, g, 1).start()
                    gather(eu_hbm, t, g, 0).wait()

                    def u_body(ch, accs):
                        xv = x_v[pl.ds(ch * L, L)]
                        return tuple(accs[e] + buf[0, e, pl.ds(ch * L, L)] * xv for e in range(G))

                    accs = plsc.parallel_loop(0, n_ch, carry=tuple(zero for _ in range(G)))(u_body)
                    for e in range(G):
                        acc_v[e, pl.ds(L, L)] = accs[e]
                    act = zero
                    for l in range(L):
                        act = act + plsc.load_gather(acc_v, [lane, jnp.full((L,), L + l, I32)])
                    y = 0.7978845608028654 * (act + 0.044715 * (act * act * act))
                    coef_v[pl.ds(L, L)] = act / (1.0 + jnp.exp(-2.0 * y)) * gate_v[t, pl.ds(g * G, G)]
                    if g + 1 < n_groups:
                        gather(eu_hbm, t, g + 1, 0).start()
                    gather(ev_hbm, t, g, 1).wait()
                    cs = [plsc.load_gather(coef_v, [jnp.full((L,), L + e, I32)]) for e in range(G)]

                    @plsc.parallel_loop(0, n_ch)
                    def _(ch):
                        o = o_v[pl.ds(ch * L, L)]
                        for e in range(G):
                            o = o + cs[e] * buf[1, e, pl.ds(ch * L, L)]
                        o_v[pl.ds(ch * L, L)] = o

                pltpu.sync_copy(o_v, o_hbm.at[tok])

    return k(idx, x, gates, eu, ev)


def _final_kernel(x1_ref, p_ref, mod_ref, g_ref, o_ref):
    o_ref[...] = x1_ref[...] + mod_ref[0, 5:6, :] * _rms(p_ref[...], g_ref[...])


def _final(x1, peer, modm, g_post2, group0, rows_per_group):
    n, d = x1.shape
    tm = 512
    row = lambda i: (i, 0)
    return pl.pallas_call(
        _final_kernel,
        out_shape=jax.ShapeDtypeStruct((n, d), F32),
        grid=(n // tm,),
        in_specs=[pl.BlockSpec((tm, d), row), pl.BlockSpec((tm, d), row),
                  pl.BlockSpec((1, 6, d), lambda i: (group0 + (i * tm) // rows_per_group, 0, 0)),
                  pl.BlockSpec((1, d), lambda i: (0, 0))],
        out_specs=pl.BlockSpec((tm, d), row),
        compiler_params=_params(("parallel",)),
        name="final_residual",
    )(x1, peer, modm, g_post2)


def _block_diag_state(s):
    b, h = s.shape[:2]
    s = s.reshape(b, h // 2, 2, HEAD_A, HEAD_A)
    z = jnp.zeros_like(s[:, :, 0])
    return jnp.concatenate([jnp.concatenate([s[:, :, 0], z], axis=-1),
                            jnp.concatenate([z, s[:, :, 1]], axis=-1)], axis=-2)


def _head_states(s2):
    b, p = s2.shape[:2]
    return jnp.stack([s2[:, :, :HEAD_A, :HEAD_A], s2[:, :, HEAD_A:, HEAD_A:]],
                     axis=2).reshape(b, 2 * p, HEAD_A, HEAD_A)


def _layer(x3, modm, group0, grid_mode, s0f, s0b, w):
    b, t, d = x3.shape
    n = b * t
    x = x3.reshape(n, d)
    rows_per_group = t if grid_mode else n
    z = _in_proj(x, modm, w["g_pre1"], w["w_in"], group0, rows_per_group)
    zs = _shift_mix(z, w["mu_shift"], w["n_shift"], w["lora_block"], grid_mode, t)
    zero_init = s0f is None
    if zero_init:
        s0f = s0b = jnp.zeros((b, w["w0"].shape[1] // LANES, LANES, LANES), F32)
    yf, yb, sf, sb = _rwkv_scan(zs.reshape(b, t, -1), w["w0"], w["w2"], w["a0"], w["a2"], w["k_k"], w["k_a"],
                                s0f, s0b, zero_init)
    d_a = yf.shape[-1]
    yag = _post_mix(yf.reshape(n, d_a), yb.reshape(n, d_a), zs, z, w["a0"], w["a2"], w["k_a"], w["r_k"],
                    w["ln_x_w"], w["ln_x_b"], w["ln_v_w"], w["ln_v_b"], w["w_s"], w["b_st"])
    x1, h2 = _out_proj(yag, w["w_out"], x, modm, w["g_post1"], w["g_pre2"], group0, rows_per_group)
    scores = _query_scores(h2, w["w_query"], w["sub_keys"])
    idx_t, gates_t = _peer_topk(scores)
    peer = _peer_experts(idx_t.T, h2, gates_t.T, w["expert_u"], w["expert_v"])
    out = _final(x1, peer, modm, w["g_post2"], group0, rows_per_group)
    return out.reshape(b, t, d), sf, sb


def kernel(x_prompt, x_sample, c, state_fwd, state_bwd, c_ctx, w_ada, b_ada, g_pre1, g_post1, g_pre2, g_post2,
           w_in, mu_shift, w0, w2, a0, a2, k_k, k_a, r_k, ln_x_w, ln_x_b, ln_v_w, ln_v_b, w_s, b_s, w_out,
           w_query, sub_keys, expert_u, expert_v):
    depth = w_in.shape[0]
    d = x_prompt.shape[-1]
    d_a = w0.shape[-1]
    n_shift = mu_shift.shape[-1]
    dec_b = x_sample.shape[0]
    cvec = jnp.concatenate([c_ctx[None], c, jnp.zeros((8 - 1 - dec_b, d), F32)], axis=0)
    xp, xs = x_prompt, x_sample
    new_f, new_b = [], []
    for l in range(depth):
        wl_in = w_in[l]
        row = lambda a: a[l].reshape(1, -1)
        w = {
            "w_in": jnp.concatenate([wl_in[:, :4 * d_a], wl_in[:, n_shift:], wl_in[:, 4 * d_a:n_shift]],
                                    axis=1).astype(BF16),
            "n_shift": n_shift,
            "lora_block": (wl_in.shape[1] - (n_shift - 4 * d_a)) // (2 * LANES),
            "mu_shift": row(mu_shift),
            "g_pre1": row(g_pre1), "g_post1": row(g_post1), "g_pre2": row(g_pre2), "g_post2": row(g_post2),
            "w0": w0[l], "w2": w2[l].reshape(-1, d_a), "a0": a0[l], "a2": a2[l].reshape(-1, d_a),
            "k_k": row(k_k), "k_a": row(k_a), "r_k": row(r_k),
            "ln_x_w": row(ln_x_w), "ln_x_b": row(ln_x_b), "ln_v_w": row(ln_v_w), "ln_v_b": row(ln_v_b),
            "w_s": w_s[l].astype(BF16), "b_st": b_s[l].T,
            "w_out": w_out[l].astype(BF16), "w_query": w_query[l].astype(BF16),
            "sub_keys": jnp.swapaxes(sub_keys[l], 0, 1).reshape(-1, N_KEYS, sub_keys.shape[-1]).astype(BF16),
            "expert_u": expert_u[l], "expert_v": expert_v[l],
        }
        modm = _modulation(cvec, w_ada[l], b_ada[l].reshape(1, -1)).reshape(8, 6, d)
        xp, sf, sb = _layer(xp, modm, 0, False, None, None, w)
        new_f.append(_head_states(sf))
        new_b.append(_head_states(sb))
        xs, _, _ = _layer(xs, modm, 1, True, _block_diag_state(state_fwd[:, l]),
                          _block_diag_state(state_bwd[:, l]), w)
    return (xp, xs, jnp.stack(new_f, axis=1), jnp.stack(new_b, axis=1))
```

```python
import functools

import jax
import jax.numpy as jnp
from jax import lax
from jax.experimental import pallas as pl
from jax.experimental.pallas import tpu as pltpu
from jax.experimental.pallas import tpu_sc as plsc

F32 = jnp.float32
BF16 = jnp.bfloat16
I32 = jnp.int32

EPS = 1e-6
GN_EPS = 64e-5
HEAD_A = 64
LANES = 128
GRID_W = 64
GMLP_CHUNK = 128
PK_HEADS = 8
N_KEYS = 128
TOPK = 16
SCAN_CHUNK = 64
SCAN_PAIRS = 8
DECAY_SCALE = 0.6065306597126334
VMEM_LIMIT = 48 * 1024 * 1024


def _params(sem):
    return pltpu.CompilerParams(dimension_semantics=sem, vmem_limit_bytes=VMEM_LIMIT)


def _sigmoid(x):
    return 1.0 / (1.0 + jnp.exp(-x))


def _gelu(x):
    return 0.5 * x * (1.0 + jnp.tanh(0.7978845608028654 * (x + 0.044715 * (x * x * x))))


def _dot(a, b):
    return jnp.dot(a.astype(BF16), b.astype(BF16), preferred_element_type=F32)


def _dot_nt(a, b):
    return lax.dot_general(a.astype(BF16), b.astype(BF16), (((1,), (1,)), ((), ())),
                           preferred_element_type=F32)


def _dot_tn(a, b):
    return lax.dot_general(a.astype(BF16), b.astype(BF16), (((0,), (0,)), ((), ())),
                           preferred_element_type=F32)


def _split(x):
    hi = x.astype(BF16)
    return hi, (x - hi.astype(F32)).astype(BF16)


def _dot_x3(a, b):
    a_hi, a_lo = _split(a)
    b_hi, b_lo = _split(b)
    dot = functools.partial(jnp.dot, preferred_element_type=F32)
    return dot(a_hi, b_hi) + (dot(a_lo, b_hi) + dot(a_hi, b_lo))


def _dot_split_rhs(a_bf, b):
    b_hi, b_mid = _split(b)
    b_lo = (b - b_hi.astype(F32) - b_mid.astype(F32)).astype(BF16)
    dot = functools.partial(jnp.dot, preferred_element_type=F32)
    return dot(a_bf, b_hi) + (dot(a_bf, b_mid) + dot(a_bf, b_lo))


def _head_sum(x, first_head):
    s_a = jnp.sum(jnp.where(first_head, x, 0.0), axis=-1, keepdims=True)
    s_b = jnp.sum(jnp.where(first_head, 0.0, x), axis=-1, keepdims=True)
    return jnp.where(first_head, s_a, s_b)


def _mod_kernel(c_ref, w_ref, b_ref, o_ref):
    c = c_ref[...]
    o_ref[...] = _dot(c * _sigmoid(c), w_ref[...]) + b_ref[...]


def _modulation(cvec, w_ada, b_ada):
    rows, d = cvec.shape
    n = w_ada.shape[1]
    tn = 1024
    return pl.pallas_call(
        _mod_kernel,
        out_shape=jax.ShapeDtypeStruct((rows, n), F32),
        grid=(n // tn,),
        in_specs=[pl.BlockSpec((rows, d), lambda j: (0, 0)),
                  pl.BlockSpec((d, tn), lambda j: (0, j)),
                  pl.BlockSpec((1, tn), lambda j: (0, j))],
        out_specs=pl.BlockSpec((rows, tn), lambda j: (0, j)),
        compiler_params=_params(("parallel",)),
        name="adaln_mod",
    )(cvec, w_ada, b_ada)


def _in_proj_kernel(x_ref, mod_ref, g_ref, w_ref, o_ref, h_ref):
    @pl.when(pl.program_id(1) == 0)
    def _():
        x = x_ref[...]
        y = x * lax.rsqrt(jnp.mean(x * x, axis=-1, keepdims=True) + EPS) * g_ref[...]
        h_ref[...] = (y * (1.0 + mod_ref[0, 1:2, :]) + mod_ref[0, 0:1, :]).astype(BF16)

    o_ref[...] = jnp.dot(h_ref[...], w_ref[...], preferred_element_type=F32)


def _in_proj(x, modm, g_pre, w_bf, group0, rows_per_group):
    n, d = x.shape
    p = w_bf.shape[1]
    tm, tn = 512, 1280
    grp = lambda i, j: (group0 + (i * tm) // rows_per_group, 0, 0)
    return pl.pallas_call(
        _in_proj_kernel,
        out_shape=jax.ShapeDtypeStruct((n, p), F32),
        grid=(n // tm, p // tn),
        in_specs=[pl.BlockSpec((tm, d), lambda i, j: (i, 0)),
                  pl.BlockSpec((1, 6, d), grp),
                  pl.BlockSpec((1, d), lambda i, j: (0, 0)),
                  pl.BlockSpec((d, tn), lambda i, j: (0, j))],
        out_specs=pl.BlockSpec((tm, tn), lambda i, j: (i, j)),
        scratch_shapes=[pltpu.VMEM((tm, d), BF16)],
        compiler_params=_params(("parallel", "arbitrary")),
        name="in_proj",
    )(x, modm, g_pre, w_bf)


def _shift_kernel(z_ref, mu_ref, o_ref, *, grid_mode, period):
    z = z_ref[...]
    rows = z.shape[0]
    t = lax.broadcasted_iota(I32, (rows, 1), 0) % period
    prev = jnp.where(t % (GRID_W if grid_mode else period) != 0, pltpu.roll(z, 1, 0), 0.0)
    nxt = jnp.where(t % (GRID_W if grid_mode else period) != (GRID_W if grid_mode else period) - 1,
                    pltpu.roll(z, rows - 1, 0), 0.0)
    if grid_mode:
        up = jnp.where(t >= GRID_W, pltpu.roll(z, GRID_W, 0), 0.0)
        down = jnp.where(t < period - GRID_W, pltpu.roll(z, rows - GRID_W, 0), 0.0)
        nb = 0.25 * (up + down + prev + nxt)
    else:
        nb = 0.5 * (prev + nxt)
    o_ref[...] = z + mu_ref[...] * (nb - z)


def _shift_mix(z, mu, n_shift, lora_block, grid_mode, period):
    n = z.shape[0]
    tr, tc = 2048, 256
    main_blocks = (n_shift // tc) - 1
    col = lambda i, j: (i, jnp.where(j < main_blocks, j, lora_block))
    return pl.pallas_call(
        functools.partial(_shift_kernel, grid_mode=grid_mode, period=period),
        out_shape=jax.ShapeDtypeStruct((n, n_shift), F32),
        grid=(n // tr, n_shift // tc),
        in_specs=[pl.BlockSpec((tr, tc), col),
                  pl.BlockSpec((1, tc), lambda i, j: (0, j))],
        out_specs=pl.BlockSpec((tr, tc), lambda i, j: (i, j)),
        compiler_params=_params(("parallel", "parallel")),
        name="token_shift",
    )(z, mu)


def _scan_chunks(chains):
    c = chains[0][0].shape[0]
    c2 = 2 * c
    n = len(chains)
    fwd = [ch[9] for ch in chains]
    first_head = lax.broadcasted_iota(I32, (1, LANES), 1) < HEAD_A
    row = lax.broadcasted_iota(I32, (c2, c2), 0)
    col = lax.broadcasted_iota(I32, (c2, c2), 1)
    eye = jnp.where(row == col, 1.0, 0.0)

    def stack(x):
        return jnp.concatenate([jnp.where(first_head, x, 0.0), jnp.where(first_head, 0.0, x)],
                               axis=0).astype(BF16)

    lhs, rhs, v2, total = [], [], [], []
    for r, k, v, logw, cl, a, k_k, k_a, _, forward in chains:
        kkr = k * k_k
        kk = kkr / jnp.maximum(jnp.sqrt(_head_sum(kkr * kkr, first_head)), 1e-12)
        kd = k * (1.0 + (a - 1.0) * k_a)
        gi = jnp.exp(-cl)
        lhs.append(jnp.concatenate([stack(kk * jnp.exp(cl - logw)), stack(r * jnp.exp(cl))], axis=0))
        rhs.append(jnp.concatenate([stack(kk * a * gi), stack(kd * gi)], axis=0))
        v2.append(stack(v))
        total.append(cl[c - 1:c, :] if forward else cl[0:1, :])

    res = [_dot_nt(lhs[i], rhs[i]) for i in range(n)]
    pr = [_dot_nt(lhs[i], chains[i][8]) for i in range(n)]
    strict = [(row > col) if f else (row < col) for f in fwd]
    incl = [(row >= col) if f else (row <= col) for f in fwd]
    ab = [jnp.where(strict[i], res[i][:c2, :c2], 0.0) for i in range(n)]
    ak = [jnp.where(strict[i], res[i][:c2, c2:], 0.0) for i in range(n)]
    gb_gk = [jnp.concatenate([jnp.where(incl[i], res[i][c2:, :c2], 0.0),
                              jnp.where(incl[i], res[i][c2:, c2:], 0.0)], axis=1).astype(BF16) for i in range(n)]
    akv = [_dot(ak[i], v2[i]) for i in range(n)]

    tinv = [eye - ab[i] for i in range(n)]
    pw = [-ab[i] for i in range(n)]
    for _ in range(c.bit_length() - 2):
        pw = [_dot(pw[i], pw[i]) for i in range(n)]
        tinv = [tinv[i] + _dot(tinv[i], pw[i]) for i in range(n)]

    u2 = [_dot(tinv[i], -pr[i][:c2, :] - akv[i]) for i in range(n)]
    uv = [jnp.concatenate([u2[i].astype(BF16), v2[i]], axis=0) for i in range(n)]
    y2 = [pr[i][c2:, :] + _dot(gb_gk[i], uv[i]) for i in range(n)]
    s_new = [(chains[i][8] + _dot_tn(uv[i], rhs[i])) * jnp.exp(total[i]) for i in range(n)]
    return [(y2[i][:c, :] + y2[i][c:, :], s_new[i]) for i in range(n)]


def _scan_direction(r_ref, k_ref, v_ref, l_ref, d, s_ref, w0, w2, a0, a2, kk, ka, forward):
    lora = l_ref[0]
    c = lora.shape[0]
    dir_rows = (lax.broadcasted_iota(I32, (LANES, 1), 0) // HEAD_A) == d
    wl = w0[d:d + 1, :] + _dot_x3(jnp.tanh(lora[:, :LANES]), jnp.where(dir_rows, w2[...], 0.0))
    logw = -DECAY_SCALE * _sigmoid(wl)
    a = _sigmoid(a0[d:d + 1, :] + _dot_x3(lora[:, LANES:], jnp.where(dir_rows, a2[...], 0.0)))
    ti = lax.broadcasted_iota(I32, (c, c), 0)
    tj = lax.broadcasted_iota(I32, (c, c), 1)
    cum = jnp.where((tj <= ti) if forward else (tj >= ti), 1.0, 0.0).astype(BF16)
    cl = _dot_split_rhs(cum, logw)
    chains = []
    for p in range(s_ref.shape[0]):
        sl = slice(p * LANES, (p + 1) * LANES)
        chains.append((r_ref[0, :, sl], k_ref[0, :, sl], v_ref[0, :, sl], logw[:, sl], cl[:, sl],
                       a[:, sl], kk[:, sl], ka[:, sl], s_ref[p], forward))
    return chains


def _scan_store(out, y_ref, s_ref):
    for p, (y, s_new) in enumerate(out):
        y_ref[0, :, p * LANES:(p + 1) * LANES] = y
        s_ref[p] = s_new


def _scan_kernel(rf, kf, vf, lf, rb, kb, vb, lb, w0, w2, a0, a2, kk, ka, s0f, s0b,
                 yf, yb, sf, sb, s2f, s2b, *, zero_init):
    c = pl.program_id(2)

    @pl.when(c == 0)
    def _():
        if zero_init:
            s2f[...] = jnp.zeros_like(s2f)
            s2b[...] = jnp.zeros_like(s2b)
        else:
            s2f[...] = s0f[0]
            s2b[...] = s0b[0]

    chains_f = _scan_direction(rf, kf, vf, lf, 0, s2f, w0, w2, a0, a2, kk, ka, True)
    chains_b = _scan_direction(rb, kb, vb, lb, 1, s2b, w0, w2, a0, a2, kk, ka, False)
    out = _scan_chunks(chains_f + chains_b)
    _scan_store(out[:len(chains_f)], yf, s2f)
    _scan_store(out[len(chains_f):], yb, s2b)

    @pl.when(c == pl.num_programs(2) - 1)
    def _():
        sf[0] = s2f[...]
        sb[0] = s2b[...]


def _rwkv_scan(zs, w0, w2r, a0, a2r, k_k, k_a, s0f, s0b, zero_init):
    b, t, _ = zs.shape
    d_a = w0.shape[1]
    pp = SCAN_PAIRS
    wide = pp * LANES
    groups = d_a // wide
    c = SCAN_CHUNK
    nc = t // c
    lora_blk = (4 * d_a) // (2 * LANES)
    fw = lambda off: (lambda i, q, j: (i, j, off + q))
    bw = lambda off: (lambda i, q, j: (i, nc - 1 - j, off + q))
    par = lambda i, q, j: (0, q)
    st = lambda i, q, j: (i, q, 0, 0)
    blk = (1, c, wide)
    lblk = (1, c, 2 * LANES)
    sblk = (1, pp, LANES, LANES)
    yshape = jax.ShapeDtypeStruct((b, t, d_a), F32)
    sshape = jax.ShapeDtypeStruct((b, d_a // LANES, LANES, LANES), F32)
    return pl.pallas_call(
        functools.partial(_scan_kernel, zero_init=zero_init),
        out_shape=(yshape, yshape, sshape, sshape),
        grid=(b, groups, nc),
        in_specs=[pl.BlockSpec(blk, fw(0)), pl.BlockSpec(blk, fw(groups)), pl.BlockSpec(blk, fw(2 * groups)),
                  pl.BlockSpec(lblk, lambda i, q, j: (i, j, lora_blk)),
                  pl.BlockSpec(blk, bw(0)), pl.BlockSpec(blk, bw(groups)), pl.BlockSpec(blk, bw(2 * groups)),
                  pl.BlockSpec(lblk, lambda i, q, j: (i, nc - 1 - j, lora_blk)),
                  pl.BlockSpec((2, wide), par), pl.BlockSpec((LANES, wide), par),
                  pl.BlockSpec((2, wide), par), pl.BlockSpec((LANES, wide), par),
                  pl.BlockSpec((1, wide), par), pl.BlockSpec((1, wide), par),
                  pl.BlockSpec(sblk, st), pl.BlockSpec(sblk, st)],
        out_specs=(pl.BlockSpec(blk, fw(0)), pl.BlockSpec(blk, bw(0)),
                   pl.BlockSpec(sblk, st), pl.BlockSpec(sblk, st)),
        scratch_shapes=[pltpu.VMEM((pp, LANES, LANES), F32), pltpu.VMEM((pp, LANES, LANES), F32)],
        compiler_params=_params(("parallel", "parallel", "arbitrary")),
        name="rwkv7_scan",
    )(zs, zs, zs, zs, zs, zs, zs, zs, w0, w2r, a0, a2r, k_k, k_a, s0f, s0b)


def _post_kernel(yf, yb, r, k, v, g, lora, u, vg, a0, a2, ka, rk, lxw, lxb, lvw, lvb, ws, bst, o_ref):
    tm = yf.shape[0]
    d_a = yf.shape[1]
    first_head = lax.broadcasted_iota(I32, (1, LANES), 1) < HEAD_A
    dir_row = lax.broadcasted_iota(I32, (LANES, 1), 0) // HEAD_A
    la = lora[:, LANES:]
    a_sum = jnp.zeros((tm, d_a), F32)
    for d in range(2):
        a_sum = a_sum + _sigmoid(a0[d:d + 1, :] + _dot_x3(la, jnp.where(dir_row == d, a2[...], 0.0)))
    rkk = r[...] * k[...] * (2.0 + (a_sum - 2.0) * ka[...]) * rk[...]
    y = yf[...] + yb[...]
    inv = 1.0 / HEAD_A
    for j in range(d_a // LANES):
        sl = slice(j * LANES, (j + 1) * LANES)
        yj = y[:, sl]
        mu = _head_sum(yj, first_head) * inv
        dl = yj - mu
        var = _head_sum(dl * dl, first_head) * inv
        yn = dl * lax.rsqrt(var + GN_EPS) * lxw[:, sl] + lxb[:, sl]
        bonus = _head_sum(rkk[:, sl], first_head) * v[:, sl]
        o_ref[:, sl] = ((yn + bonus) * _sigmoid(g[:, sl])).astype(BF16)

    uu = _gelu(u[...])
    vv = _gelu(vg[...])
    mu = jnp.mean(vv, axis=-1, keepdims=True)
    dv = vv - mu
    vn = dv * lax.rsqrt(jnp.mean(dv * dv, axis=-1, keepdims=True) + EPS) * lvw[...] + lvb[...]
    for ch in range(tm // GMLP_CHUNK):
        rows = slice(ch * GMLP_CHUNK, (ch + 1) * GMLP_CHUNK)
        for h in range(ws.shape[0]):
            cols = slice(h * LANES, (h + 1) * LANES)
            sp = _dot(ws[h], vn[rows, cols]) + bst[:, h:h + 1]
            o_ref[rows, d_a + h * LANES:d_a + (h + 1) * LANES] = (uu[rows, cols] * sp).astype(BF16)


def _post_mix(yf, yb, zs, z, a0, a2r, k_a, r_k, lxw, lxb, lvw, lvb, w_s, b_st):
    n, d_a = yf.shape
    tm = 256
    wide = lambda j: (lambda i: (i, j))
    full = lambda shape: pl.BlockSpec(shape, lambda i: (0,) * len(shape))
    lora_blk = (4 * d_a) // (2 * LANES)
    return pl.pallas_call(
        _post_kernel,
        out_shape=jax.ShapeDtypeStruct((n, 2 * d_a), BF16),
        grid=(n // tm,),
        in_specs=[pl.BlockSpec((tm, d_a), wide(0)), pl.BlockSpec((tm, d_a), wide(0)),
                  pl.BlockSpec((tm, d_a), wide(0)), pl.BlockSpec((tm, d_a), wide(1)),
                  pl.BlockSpec((tm, d_a), wide(2)), pl.BlockSpec((tm, d_a), wide(3)),
                  pl.BlockSpec((tm, 2 * LANES), wide(lora_blk)),
                  pl.BlockSpec((tm, d_a), wide(4)), pl.BlockSpec((tm, d_a), wide(5)),
                  full((2, d_a)), full((LANES, d_a)), full((1, d_a)), full((1, d_a)),
                  full((1, d_a)), full((1, d_a)), full((1, d_a)), full((1, d_a)),
                  full(w_s.shape), full(b_st.shape)],
        out_specs=pl.BlockSpec((tm, 2 * d_a), wide(0)),
        compiler_params=_params(("parallel",)),
        name="mix_post",
    )(yf, yb, zs, zs, zs, zs, zs, z, z, a0, a2r, k_a, r_k, lxw, lxb, lvw, lvb, w_s, b_st)


def _rms(x, g):
    return x * lax.rsqrt(jnp.mean(x * x, axis=-1, keepdims=True) + EPS) * g


def _out_proj_kernel(a_ref, w_ref, x_ref, mod_ref, gpost_ref, gpre_ref, x1_ref, h2_ref):
    o = jnp.dot(a_ref[...], w_ref[...], preferred_element_type=F32)
    x1 = x_ref[...] + mod_ref[0, 2:3, :] * _rms(o, gpost_ref[...])
    x1_ref[...] = x1
    h2_ref[...] = _rms(x1, gpre_ref[...]) * (1.0 + mod_ref[0, 4:5, :]) + mod_ref[0, 3:4, :]


def _out_proj(yag, w_bf, x, modm, g_post, g_pre2, group0, rows_per_group):
    n, d = x.shape
    tm = 256
    grp = lambda i: (group0 + (i * tm) // rows_per_group, 0, 0)
    row = lambda i: (i, 0)
    fix = lambda i: (0, 0)
    shp = jax.ShapeDtypeStruct((n, d), F32)
    return pl.pallas_call(
        _out_proj_kernel,
        out_shape=(shp, shp),
        grid=(n // tm,),
        in_specs=[pl.BlockSpec((tm, yag.shape[1]), row), pl.BlockSpec(w_bf.shape, fix),
                  pl.BlockSpec((tm, d), row), pl.BlockSpec((1, 6, d), grp),
                  pl.BlockSpec((1, d), fix), pl.BlockSpec((1, d), fix)],
        out_specs=(pl.BlockSpec((tm, d), row), pl.BlockSpec((tm, d), row)),
        compiler_params=_params(("parallel",)),
        name="out_proj",
    )(yag, w_bf, x, modm, g_post, g_pre2)


def _query_kernel(h_ref, wq_ref, sk_ref, o_ref):
    q = jnp.dot(h_ref[...].astype(BF16), wq_ref[...], preferred_element_type=F32)
    for g in range(sk_ref.shape[0]):
        o_ref[g * N_KEYS:(g + 1) * N_KEYS, :] = _dot_nt(sk_ref[g], q[:, g * LANES:(g + 1) * LANES])


def _query_scores(h2, wq_bf, sk):
    n, d = h2.shape
    tm = 256
    groups = sk.shape[0]
    return pl.pallas_call(
        _query_kernel,
        out_shape=jax.ShapeDtypeStruct((groups * N_KEYS, n), F32),
        grid=(n // tm,),
        in_specs=[pl.BlockSpec((tm, d), lambda i: (i, 0)),
                  pl.BlockSpec(wq_bf.shape, lambda i: (0, 0)),
                  pl.BlockSpec(sk.shape, lambda i: (0, 0, 0))],
        out_specs=pl.BlockSpec((groups * N_KEYS, tm), lambda i: (0, i)),
        compiler_params=_params(("parallel",)),
        name="peer_query",
    )(h2, wq_bf, sk)


def _top16(s):
    rows = s.shape[0]
    iota = lax.broadcasted_iota(I32, s.shape, 0)
    vals, idxs = [], []
    for _ in range(TOPK):
        m = jnp.max(s, axis=0, keepdims=True)
        i = jnp.min(jnp.where(s == m, iota, rows), axis=0, keepdims=True)
        vals.append(m)
        idxs.append(i)
        s = jnp.where(iota == i, -jnp.inf, s)
    return jnp.concatenate(vals, axis=0), jnp.concatenate(idxs, axis=0)


def _topk_kernel(s_ref, idx_ref, gate_ref):
    def head(h, carry):
        base = pl.multiple_of(h * (2 * N_KEYS), 2 * N_KEYS)
        v1, i1 = _top16(s_ref[pl.ds(base, N_KEYS), :])
        v2, i2 = _top16(s_ref[pl.ds(base + N_KEYS, N_KEYS), :])
        cand = jnp.concatenate([v1[i:i + 1, :] + v2 for i in range(TOPK)], axis=0)
        top_s, pos = _top16(cand)
        pi = pos // TOPK
        pj = pos % TOPK
        e1 = jnp.zeros_like(pos)
        e2 = jnp.zeros_like(pos)
        for i in range(TOPK):
            e1 = e1 + jnp.where(pi == i, i1[i:i + 1, :], 0)
            e2 = e2 + jnp.where(pj == i, i2[i:i + 1, :], 0)
        ex = jnp.exp(top_s - top_s[0:1, :])
        out = pl.multiple_of(h * TOPK, TOPK)
        idx_ref[pl.ds(out, TOPK), :] = e1 * N_KEYS + e2
        gate_ref[pl.ds(out, TOPK), :] = ex / jnp.sum(ex, axis=0, keepdims=True)
        return carry

    lax.fori_loop(0, PK_HEADS, head, 0)


def _peer_topk(scores):
    rows, n = scores.shape
    tt = 256
    out_rows = PK_HEADS * TOPK
    return pl.pallas_call(
        _topk_kernel,
        out_shape=(jax.ShapeDtypeStruct((out_rows, n), I32), jax.ShapeDtypeStruct((out_rows, n), F32)),
        grid=(n // tt,),
        in_specs=[pl.BlockSpec((rows, tt), lambda i: (0, i))],
        out_specs=(pl.BlockSpec((out_rows, tt), lambda i: (0, i)),
                   pl.BlockSpec((out_rows, tt), lambda i: (0, i))),
        compiler_params=_params(("parallel",)),
        name="peer_topk",
    )(scores)


SC_CORES = 2
SC_SUBCORES = 16
SC_LANES = 16
PEER_GROUP = 16
PEER_BLOCK = 32


def _peer_experts_sc(idx, x, gates, eu, ev):
    n, d = x.shape
    n_sel = idx.shape[1]
    per_w = n // (SC_CORES * SC_SUBCORES)
    L, G, TB = SC_LANES, PEER_GROUP, PEER_BLOCK
    n_groups = n_sel // G
    n_ch = d // L
    mesh = plsc.VectorSubcoreMesh(core_axis_name="c", subcore_axis_name="s")

    @functools.partial(
        pl.kernel, mesh=mesh, out_type=jax.ShapeDtypeStruct((n, d), F32),
        compiler_params=pltpu.CompilerParams(needs_layout_passes=False),
        scratch_types=[pltpu.VMEM((TB, n_sel), I32), pltpu.VMEM((TB, n_sel), F32),
                       pltpu.VMEM((d,), F32), pltpu.VMEM((d,), F32),
                       pltpu.VMEM((2, G, d), F32), pltpu.VMEM((G, 2 * L), F32), pltpu.VMEM((2 * L,), F32),
                       pltpu.SemaphoreType.DMA((2,))],
        name="peer_experts_sc")
    def k(idx_hbm, x_hbm, g_hbm, eu_hbm, ev_hbm, o_hbm, idx_v, gate_v, x_v, o_v, buf, acc_v, coef_v, sem):
        base = (lax.axis_index("s") * SC_CORES + lax.axis_index("c")) * per_w
        lane = lax.iota(I32, L)
        zero = jnp.zeros((L,), F32)

        def gather(table, t, g, slot):
            return pltpu.make_async_copy(table.at[idx_v.at[t, pl.ds(g * G, G)]], buf.at[slot], sem.at[slot])

        @pl.loop(0, per_w // TB)
        def _(blk):
            tok0 = base + blk * TB
            pltpu.sync_copy(idx_hbm.at[pl.ds(tok0, TB)], idx_v)
            pltpu.sync_copy(g_hbm.at[pl.ds(tok0, TB)], gate_v)

            @pl.loop(0, TB)
            def _(t):
                tok = tok0 + t
                pltpu.sync_copy(x_hbm.at[tok], x_v)
                gather(eu_hbm, t, 0, 0).start()

                @plsc.parallel_loop(0, n_ch)
                def _(ch):
                    o_v[pl.ds(ch * L, L)] = zero

                for g in range(n_groups):
                    gather(ev_hbm, t, g, 1).start()
                    gather(eu_hbm, t, g, 0).wait()

                    def u_body(ch, accs):
                        xv = x_v[pl.ds(ch * L, L)]
                        return tuple(accs[e] + buf[0, e, pl.ds(ch * L, L)] * xv for e in range(G))

                    accs = plsc.parallel_loop(0, n_ch, carry=tuple(zero for _ in range(G)))(u_body)
                    for e in range(G):
                        acc_v[e, pl.ds(L, L)] = accs[e]
                    act = zero
                    for l in range(L):
                        act = act + plsc.load_gather(acc_v, [lane, jnp.full((L,), L + l, I32)])
                    y = 0.7978845608028654 * (act + 0.044715 * (act * act * act))
                    coef_v[pl.ds(L, L)] = act / (1.0 + jnp.exp(-2.0 * y)) * gate_v[t, pl.ds(g * G, G)]
                    if g + 1 < n_groups:
                        gather(eu_hbm, t, g + 1, 0).start()
                    gather(ev_hbm, t, g, 1).wait()
                    cs = [plsc.load_gather(coef_v, [jnp.full((L,), L + e, I32)]) for e in range(G)]

                    @plsc.parallel_loop(0, n_ch)
                    def _(ch):
                        o = o_v[pl.ds(ch * L, L)]
                        for e in range(G):
                            o = o + cs[e] * buf[1, e, pl.ds(ch * L, L)]
                        o_v[pl.ds(ch * L, L)] = o

                pltpu.sync_copy(o_v, o_hbm.at[tok])

    return k(idx, x, gates, eu, ev)


PEER_TC_TOKENS = 16
PEER_TC_SHARE = 3072


def _peer_tc_kernel(idx_ref, x_ref, g_ref, uv_ref, o_ref, buf, sem):
    rows, n_sel = buf.shape[1], buf.shape[2]
    half = rows // 2

    def issue(t, slot):
        def body(e, carry):
            src = pl.multiple_of(idx_ref[t, e] * rows, rows)
            pltpu.make_async_copy(uv_ref.at[pl.ds(src, rows), :], buf.at[slot, :, e, :], sem.at[slot]).start()
            return carry
        lax.fori_loop(0, n_sel, body, 0, unroll=8)

    def wait(slot):
        pltpu.make_async_copy(buf.at[1 - slot], buf.at[slot], sem.at[slot]).wait()

    issue(0, 0)
    gates = g_ref[...].T
    for t in range(PEER_TC_TOKENS):
        slot = t % 2
        if t + 1 < PEER_TC_TOKENS:
            issue(t + 1, 1 - slot)
        wait(slot)
        xt = x_ref[t * half:(t + 1) * half, :]
        acc = buf[slot, 0] * xt[0:1, :]
        for c in range(1, half):
            acc = acc + buf[slot, c] * xt[c:c + 1, :]
        act = jnp.sum(acc, axis=-1, keepdims=True)
        coef = jnp.broadcast_to(_gelu(act) * gates[:, t:t + 1], (n_sel, LANES))
        o_ref[t * half:(t + 1) * half, :] = jnp.concatenate(
            [jnp.sum(coef * buf[slot, half + c], axis=0, keepdims=True) for c in range(half)], axis=0)


def _peer_experts_tc(idx, x, gates, uv):
    n, d = x.shape
    n_sel = idx.shape[1]
    tt = PEER_TC_TOKENS
    half = d // LANES
    out = pl.pallas_call(
        _peer_tc_kernel,
        out_shape=jax.ShapeDtypeStruct((n * half, LANES), F32),
        grid=(n // tt,),
        in_specs=[pl.BlockSpec((tt, n_sel), lambda i: (i, 0), memory_space=pltpu.SMEM),
                  pl.BlockSpec((tt * half, LANES), lambda i: (i, 0)),
                  pl.BlockSpec((tt, n_sel), lambda i: (i, 0)),
                  pl.BlockSpec(memory_space=pl.ANY)],
        out_specs=pl.BlockSpec((tt * half, LANES), lambda i: (i, 0)),
        scratch_shapes=[pltpu.VMEM((2, 2 * half, n_sel, LANES), F32), pltpu.SemaphoreType.DMA((2,))],
        compiler_params=pltpu.CompilerParams(dimension_semantics=("arbitrary",),
                                             vmem_limit_bytes=VMEM_LIMIT, disable_bounds_checks=True),
        name="peer_experts_tc",
    )(idx, x.reshape(n * half, LANES), gates, uv)
    return out.reshape(n, d)


def _peer_experts(idx, x, gates, w):
    m = PEER_TC_SHARE
    out_tc = _peer_experts_tc(idx[:m], x[:m], gates[:m], w["expert_uv"])
    out_sc = _peer_experts_sc(idx[m:], x[m:], gates[m:], w["expert_u"], w["expert_v"])
    return jnp.concatenate([out_tc, out_sc], axis=0)


def _final_kernel(x1_ref, p_ref, mod_ref, g_ref, o_ref):
    o_ref[...] = x1_ref[...] + mod_ref[0, 5:6, :] * _rms(p_ref[...], g_ref[...])


def _final(x1, peer, modm, g_post2, group0, rows_per_group):
    n, d = x1.shape
    tm = 512
    row = lambda i: (i, 0)
    return pl.pallas_call(
        _final_kernel,
        out_shape=jax.ShapeDtypeStruct((n, d), F32),
        grid=(n // tm,),
        in_specs=[pl.BlockSpec((tm, d), row), pl.BlockSpec((tm, d), row),
                  pl.BlockSpec((1, 6, d), lambda i: (group0 + (i * tm) // rows_per_group, 0, 0)),
                  pl.BlockSpec((1, d), lambda i: (0, 0))],
        out_specs=pl.BlockSpec((tm, d), row),
        compiler_params=_params(("parallel",)),
        name="final_residual",
    )(x1, peer, modm, g_post2)


def _block_diag_state(s):
    b, h = s.shape[:2]
    s = s.reshape(b, h // 2, 2, HEAD_A, HEAD_A)
    z = jnp.zeros_like(s[:, :, 0])
    return jnp.concatenate([jnp.concatenate([s[:, :, 0], z], axis=-1),
                            jnp.concatenate([z, s[:, :, 1]], axis=-1)], axis=-2)


def _head_states(s2):
    b, p = s2.shape[:2]
    return jnp.stack([s2[:, :, :HEAD_A, :HEAD_A], s2[:, :, HEAD_A:, HEAD_A:]],
                     axis=2).reshape(b, 2 * p, HEAD_A, HEAD_A)


def _layer(x3, modm, group0, grid_mode, s0f, s0b, w):
    b, t, d = x3.shape
    n = b * t
    x = x3.reshape(n, d)
    rows_per_group = t if grid_mode else n
    z = _in_proj(x, modm, w["g_pre1"], w["w_in"], group0, rows_per_group)
    zs = _shift_mix(z, w["mu_shift"], w["n_shift"], w["lora_block"], grid_mode, t)
    zero_init = s0f is None
    if zero_init:
        s0f = s0b = jnp.zeros((b, w["w0"].shape[1] // LANES, LANES, LANES), F32)
    yf, yb, sf, sb = _rwkv_scan(zs.reshape(b, t, -1), w["w0"], w["w2"], w["a0"], w["a2"], w["k_k"], w["k_a"],
                                s0f, s0b, zero_init)
    d_a = yf.shape[-1]
    yag = _post_mix(yf.reshape(n, d_a), yb.reshape(n, d_a), zs, z, w["a0"], w["a2"], w["k_a"], w["r_k"],
                    w["ln_x_w"], w["ln_x_b"], w["ln_v_w"], w["ln_v_b"], w["w_s"], w["b_st"])
    x1, h2 = _out_proj(yag, w["w_out"], x, modm, w["g_post1"], w["g_pre2"], group0, rows_per_group)
    scores = _query_scores(h2, w["w_query"], w["sub_keys"])
    idx_t, gates_t = _peer_topk(scores)
    peer = _peer_experts(idx_t.T, h2, gates_t.T, w)
    out = _final(x1, peer, modm, w["g_post2"], group0, rows_per_group)
    return out.reshape(b, t, d), sf, sb


def kernel(x_prompt, x_sample, c, state_fwd, state_bwd, c_ctx, w_ada, b_ada, g_pre1, g_post1, g_pre2, g_post2,
           w_in, mu_shift, w0, w2, a0, a2, k_k, k_a, r_k, ln_x_w, ln_x_b, ln_v_w, ln_v_b, w_s, b_s, w_out,
           w_query, sub_keys, expert_u, expert_v):
    depth = w_in.shape[0]
    d = x_prompt.shape[-1]
    d_a = w0.shape[-1]
    n_shift = mu_shift.shape[-1]
    dec_b = x_sample.shape[0]
    cvec = jnp.concatenate([c_ctx[None], c, jnp.zeros((8 - 1 - dec_b, d), F32)], axis=0)
    xp, xs = x_prompt, x_sample
    new_f, new_b = [], []
    for l in range(depth):
        wl_in = w_in[l]
        row = lambda a: a[l].reshape(1, -1)
        w = {
            "w_in": jnp.concatenate([wl_in[:, :4 * d_a], wl_in[:, n_shift:], wl_in[:, 4 * d_a:n_shift]],
                                    axis=1).astype(BF16),
            "n_shift": n_shift,
            "lora_block": (wl_in.shape[1] - (n_shift - 4 * d_a)) // (2 * LANES),
            "mu_shift": row(mu_shift),
            "g_pre1": row(g_pre1), "g_post1": row(g_post1), "g_pre2": row(g_pre2), "g_post2": row(g_post2),
            "w0": w0[l], "w2": w2[l].reshape(-1, d_a), "a0": a0[l], "a2": a2[l].reshape(-1, d_a),
            "k_k": row(k_k), "k_a": row(k_a), "r_k": row(r_k),
            "ln_x_w": row(ln_x_w), "ln_x_b": row(ln_x_b), "ln_v_w": row(ln_v_w), "ln_v_b": row(ln_v_b),
            "w_s": w_s[l].astype(BF16), "b_st": b_s[l].T,
            "w_out": w_out[l].astype(BF16), "w_query": w_query[l].astype(BF16),
            "sub_keys": jnp.swapaxes(sub_keys[l], 0, 1).reshape(-1, N_KEYS, sub_keys.shape[-1]).astype(BF16),
            "expert_u": expert_u[l], "expert_v": expert_v[l],
            "expert_uv": jnp.concatenate([expert_u[l].reshape(-1, d // LANES, LANES),
                                          expert_v[l].reshape(-1, d // LANES, LANES)], axis=1).reshape(-1, LANES),
        }
        modm = _modulation(cvec, w_ada[l], b_ada[l].reshape(1, -1)).reshape(8, 6, d)
        xp, sf, sb = _layer(xp, modm, 0, False, None, None, w)
        new_f.append(_head_states(sf))
        new_b.append(_head_states(sb))
        xs, _, _ = _layer(xs, modm, 1, True, _block_diag_state(state_fwd[:, l]),
                          _block_diag_state(state_bwd[:, l]), w)
    return (xp, xs, jnp.stack(new_f, axis=1), jnp.stack(new_b, axis=1))
```

```python
import functools

import jax
import jax.numpy as jnp
from jax import lax
from jax.experimental import pallas as pl
from jax.experimental.pallas import tpu as pltpu
from jax.experimental.pallas import tpu_sc as plsc

F32 = jnp.float32
BF16 = jnp.bfloat16
I32 = jnp.int32

EPS = 1e-6
GN_EPS = 64e-5
HEAD_A = 64
LANES = 128
GRID_W = 64
GMLP_CHUNK = 128
PK_HEADS = 8
N_KEYS = 128
TOPK = 16
SCAN_CHUNK = 64
SCAN_PAIRS = 8
DECAY_SCALE = 0.6065306597126334
VMEM_LIMIT = 48 * 1024 * 1024


def _params(sem):
    return pltpu.CompilerParams(dimension_semantics=sem, vmem_limit_bytes=VMEM_LIMIT)


def _sigmoid(x):
    return 1.0 / (1.0 + jnp.exp(-x))


def _gelu(x):
    return 0.5 * x * (1.0 + jnp.tanh(0.7978845608028654 * (x + 0.044715 * (x * x * x))))


def _dot(a, b):
    return jnp.dot(a.astype(BF16), b.astype(BF16), preferred_element_type=F32)


def _dot_nt(a, b):
    return lax.dot_general(a.astype(BF16), b.astype(BF16), (((1,), (1,)), ((), ())),
                           preferred_element_type=F32)


def _dot_tn(a, b):
    return lax.dot_general(a.astype(BF16), b.astype(BF16), (((0,), (0,)), ((), ())),
                           preferred_element_type=F32)


def _split(x):
    hi = x.astype(BF16)
    return hi, (x - hi.astype(F32)).astype(BF16)


def _dot_x3(a, b):
    a_hi, a_lo = _split(a)
    b_hi, b_lo = _split(b)
    dot = functools.partial(jnp.dot, preferred_element_type=F32)
    return dot(a_hi, b_hi) + (dot(a_lo, b_hi) + dot(a_hi, b_lo))


def _dot_split_rhs(a_bf, b):
    b_hi, b_mid = _split(b)
    b_lo = (b - b_hi.astype(F32) - b_mid.astype(F32)).astype(BF16)
    dot = functools.partial(jnp.dot, preferred_element_type=F32)
    return dot(a_bf, b_hi) + (dot(a_bf, b_mid) + dot(a_bf, b_lo))


def _head_sum(x, first_head):
    s_a = jnp.sum(jnp.where(first_head, x, 0.0), axis=-1, keepdims=True)
    s_b = jnp.sum(jnp.where(first_head, 0.0, x), axis=-1, keepdims=True)
    return jnp.where(first_head, s_a, s_b)


def _mod_kernel(c_ref, w_ref, b_ref, o_ref):
    c = c_ref[...]
    o_ref[...] = _dot(c * _sigmoid(c), w_ref[...]) + b_ref[...]


def _modulation(cvec, w_ada, b_ada):
    rows, d = cvec.shape
    n = w_ada.shape[1]
    tn = 1024
    return pl.pallas_call(
        _mod_kernel,
        out_shape=jax.ShapeDtypeStruct((rows, n), F32),
        grid=(n // tn,),
        in_specs=[pl.BlockSpec((rows, d), lambda j: (0, 0)),
                  pl.BlockSpec((d, tn), lambda j: (0, j)),
                  pl.BlockSpec((1, tn), lambda j: (0, j))],
        out_specs=pl.BlockSpec((rows, tn), lambda j: (0, j)),
        compiler_params=_params(("parallel",)),
        name="adaln_mod",
    )(cvec, w_ada, b_ada)


def _in_proj_kernel(x_ref, mod_ref, g_ref, w_ref, o_ref, h_ref):
    @pl.when(pl.program_id(1) == 0)
    def _():
        x = x_ref[...]
        y = x * lax.rsqrt(jnp.mean(x * x, axis=-1, keepdims=True) + EPS) * g_ref[...]
        h_ref[...] = (y * (1.0 + mod_ref[0, 1:2, :]) + mod_ref[0, 0:1, :]).astype(BF16)

    o_ref[...] = jnp.dot(h_ref[...], w_ref[...], preferred_element_type=F32)


def _in_proj(x, modm, g_pre, w_bf, group0, rows_per_group):
    n, d = x.shape
    p = w_bf.shape[1]
    tm, tn = 512, 1280
    grp = lambda i, j: (group0 + (i * tm) // rows_per_group, 0, 0)
    return pl.pallas_call(
        _in_proj_kernel,
        out_shape=jax.ShapeDtypeStruct((n, p), F32),
        grid=(n // tm, p // tn),
        in_specs=[pl.BlockSpec((tm, d), lambda i, j: (i, 0)),
                  pl.BlockSpec((1, 6, d), grp),
                  pl.BlockSpec((1, d), lambda i, j: (0, 0)),
                  pl.BlockSpec((d, tn), lambda i, j: (0, j))],
        out_specs=pl.BlockSpec((tm, tn), lambda i, j: (i, j)),
        scratch_shapes=[pltpu.VMEM((tm, d), BF16)],
        compiler_params=_params(("parallel", "arbitrary")),
        name="in_proj",
    )(x, modm, g_pre, w_bf)


def _shift_kernel(z_ref, mu_ref, o_ref, *, grid_mode, period):
    z = z_ref[...]
    rows = z.shape[0]
    t = lax.broadcasted_iota(I32, (rows, 1), 0) % period
    prev = jnp.where(t % (GRID_W if grid_mode else period) != 0, pltpu.roll(z, 1, 0), 0.0)
    nxt = jnp.where(t % (GRID_W if grid_mode else period) != (GRID_W if grid_mode else period) - 1,
                    pltpu.roll(z, rows - 1, 0), 0.0)
    if grid_mode:
        up = jnp.where(t >= GRID_W, pltpu.roll(z, GRID_W, 0), 0.0)
        down = jnp.where(t < period - GRID_W, pltpu.roll(z, rows - GRID_W, 0), 0.0)
        nb = 0.25 * (up + down + prev + nxt)
    else:
        nb = 0.5 * (prev + nxt)
    o_ref[...] = z + mu_ref[...] * (nb - z)


def _shift_mix(z, mu, n_shift, lora_block, grid_mode, period):
    n = z.shape[0]
    tr, tc = 2048, 256
    main_blocks = (n_shift // tc) - 1
    col = lambda i, j: (i, jnp.where(j < main_blocks, j, lora_block))
    return pl.pallas_call(
        functools.partial(_shift_kernel, grid_mode=grid_mode, period=period),
        out_shape=jax.ShapeDtypeStruct((n, n_shift), F32),
        grid=(n // tr, n_shift // tc),
        in_specs=[pl.BlockSpec((tr, tc), col),
                  pl.BlockSpec((1, tc), lambda i, j: (0, j))],
        out_specs=pl.BlockSpec((tr, tc), lambda i, j: (i, j)),
        compiler_params=_params(("parallel", "parallel")),
        name="token_shift",
    )(z, mu)


def _scan_chunks(chains):
    c = chains[0][0].shape[0]
    c2 = 2 * c
    n = len(chains)
    fwd = [ch[9] for ch in chains]
    first_head = lax.broadcasted_iota(I32, (1, LANES), 1) < HEAD_A
    row = lax.broadcasted_iota(I32, (c2, c2), 0)
    col = lax.broadcasted_iota(I32, (c2, c2), 1)
    eye = jnp.where(row == col, 1.0, 0.0)

    def stack(x):
        return jnp.concatenate([jnp.where(first_head, x, 0.0), jnp.where(first_head, 0.0, x)],
                               axis=0).astype(BF16)

    lhs, rhs, v2, total = [], [], [], []
    for r, k, v, logw, cl, a, k_k, k_a, _, forward in chains:
        kkr = k * k_k
        kk = kkr / jnp.maximum(jnp.sqrt(_head_sum(kkr * kkr, first_head)), 1e-12)
        kd = k * (1.0 + (a - 1.0) * k_a)
        gi = jnp.exp(-cl)
        lhs.append(jnp.concatenate([stack(kk * jnp.exp(cl - logw)), stack(r * jnp.exp(cl))], axis=0))
        rhs.append(jnp.concatenate([stack(kk * a * gi), stack(kd * gi)], axis=0))
        v2.append(stack(v))
        total.append(cl[c - 1:c, :] if forward else cl[0:1, :])

    res = [_dot_nt(lhs[i], rhs[i]) for i in range(n)]
    pr = [_dot_nt(lhs[i], chains[i][8]) for i in range(n)]
    strict = [(row > col) if f else (row < col) for f in fwd]
    incl = [(row >= col) if f else (row <= col) for f in fwd]
    ab = [jnp.where(strict[i], res[i][:c2, :c2], 0.0) for i in range(n)]
    ak = [jnp.where(strict[i], res[i][:c2, c2:], 0.0) for i in range(n)]
    gb_gk = [jnp.concatenate([jnp.where(incl[i], res[i][c2:, :c2], 0.0),
                              jnp.where(incl[i], res[i][c2:, c2:], 0.0)], axis=1).astype(BF16) for i in range(n)]
    akv = [_dot(ak[i], v2[i]) for i in range(n)]

    tinv = [eye - ab[i] for i in range(n)]
    pw = [-ab[i] for i in range(n)]
    for _ in range(c.bit_length() - 2):
        pw = [_dot(pw[i], pw[i]) for i in range(n)]
        tinv = [tinv[i] + _dot(tinv[i], pw[i]) for i in range(n)]

    u2 = [_dot(tinv[i], -pr[i][:c2, :] - akv[i]) for i in range(n)]
    uv = [jnp.concatenate([u2[i].astype(BF16), v2[i]], axis=0) for i in range(n)]
    y2 = [pr[i][c2:, :] + _dot(gb_gk[i], uv[i]) for i in range(n)]
    s_new = [(chains[i][8] + _dot_tn(uv[i], rhs[i])) * jnp.exp(total[i]) for i in range(n)]
    return [(y2[i][:c, :] + y2[i][c:, :], s_new[i]) for i in range(n)]


def _scan_direction(r_ref, k_ref, v_ref, l_ref, d, s_ref, w0, w2, a0, a2, kk, ka, forward):
    lora = l_ref[0]
    c = lora.shape[0]
    dir_rows = (lax.broadcasted_iota(I32, (LANES, 1), 0) // HEAD_A) == d
    wl = w0[d:d + 1, :] + _dot_x3(jnp.tanh(lora[:, :LANES]), jnp.where(dir_rows, w2[...], 0.0))
    logw = -DECAY_SCALE * _sigmoid(wl)
    a = _sigmoid(a0[d:d + 1, :] + _dot_x3(lora[:, LANES:], jnp.where(dir_rows, a2[...], 0.0)))
    ti = lax.broadcasted_iota(I32, (c, c), 0)
    tj = lax.broadcasted_iota(I32, (c, c), 1)
    cum = jnp.where((tj <= ti) if forward else (tj >= ti), 1.0, 0.0).astype(BF16)
    cl = _dot_split_rhs(cum, logw)
    chains = []
    for p in range(s_ref.shape[0]):
        sl = slice(p * LANES, (p + 1) * LANES)
        chains.append((r_ref[0, :, sl], k_ref[0, :, sl], v_ref[0, :, sl], logw[:, sl], cl[:, sl],
                       a[:, sl], kk[:, sl], ka[:, sl], s_ref[p], forward))
    return chains


def _scan_store(out, y_ref, s_ref):
    for p, (y, s_new) in enumerate(out):
        y_ref[0, :, p * LANES:(p + 1) * LANES] = y
        s_ref[p] = s_new


def _scan_kernel(rf, kf, vf, lf, rb, kb, vb, lb, w0, w2, a0, a2, kk, ka, s0f, s0b,
                 yf, yb, sf, sb, s2f, s2b, *, zero_init):
    c = pl.program_id(2)

    @pl.when(c == 0)
    def _():
        if zero_init:
            s2f[...] = jnp.zeros_like(s2f)
            s2b[...] = jnp.zeros_like(s2b)
        else:
            s2f[...] = s0f[0]
            s2b[...] = s0b[0]

    chains_f = _scan_direction(rf, kf, vf, lf, 0, s2f, w0, w2, a0, a2, kk, ka, True)
    chains_b = _scan_direction(rb, kb, vb, lb, 1, s2b, w0, w2, a0, a2, kk, ka, False)
    out = _scan_chunks(chains_f + chains_b)
    _scan_store(out[:len(chains_f)], yf, s2f)
    _scan_store(out[len(chains_f):], yb, s2b)

    @pl.when(c == pl.num_programs(2) - 1)
    def _():
        sf[0] = s2f[...]
        sb[0] = s2b[...]


def _rwkv_scan(zs, w0, w2r, a0, a2r, k_k, k_a, s0f, s0b, zero_init):
    b, t, _ = zs.shape
    d_a = w0.shape[1]
    pp = SCAN_PAIRS
    wide = pp * LANES
    groups = d_a // wide
    c = SCAN_CHUNK
    nc = t // c
    lora_blk = (4 * d_a) // (2 * LANES)
    fw = lambda off: (lambda i, q, j: (i, j, off + q))
    bw = lambda off: (lambda i, q, j: (i, nc - 1 - j, off + q))
    par = lambda i, q, j: (0, q)
    st = lambda i, q, j: (i, q, 0, 0)
    blk = (1, c, wide)
    lblk = (1, c, 2 * LANES)
    sblk = (1, pp, LANES, LANES)
    yshape = jax.ShapeDtypeStruct((b, t, d_a), F32)
    sshape = jax.ShapeDtypeStruct((b, d_a // LANES, LANES, LANES), F32)
    return pl.pallas_call(
        functools.partial(_scan_kernel, zero_init=zero_init),
        out_shape=(yshape, yshape, sshape, sshape),
        grid=(b, groups, nc),
        in_specs=[pl.BlockSpec(blk, fw(0)), pl.BlockSpec(blk, fw(groups)), pl.BlockSpec(blk, fw(2 * groups)),
                  pl.BlockSpec(lblk, lambda i, q, j: (i, j, lora_blk)),
                  pl.BlockSpec(blk, bw(0)), pl.BlockSpec(blk, bw(groups)), pl.BlockSpec(blk, bw(2 * groups)),
                  pl.BlockSpec(lblk, lambda i, q, j: (i, nc - 1 - j, lora_blk)),
                  pl.BlockSpec((2, wide), par), pl.BlockSpec((LANES, wide), par),
                  pl.BlockSpec((2, wide), par), pl.BlockSpec((LANES, wide), par),
                  pl.BlockSpec((1, wide), par), pl.BlockSpec((1, wide), par),
                  pl.BlockSpec(sblk, st), pl.BlockSpec(sblk, st)],
        out_specs=(pl.BlockSpec(blk, fw(0)), pl.BlockSpec(blk, bw(0)),
                   pl.BlockSpec(sblk, st), pl.BlockSpec(sblk, st)),
        scratch_shapes=[pltpu.VMEM((pp, LANES, LANES), F32), pltpu.VMEM((pp, LANES, LANES), F32)],
        compiler_params=_params(("parallel", "parallel", "arbitrary")),
        name="rwkv7_scan",
    )(zs, zs, zs, zs, zs, zs, zs, zs, w0, w2r, a0, a2r, k_k, k_a, s0f, s0b)


def _post_kernel(yf, yb, r, k, v, g, lora, u, vg, a0, a2, ka, rk, lxw, lxb, lvw, lvb, ws, bst, o_ref):
    tm = yf.shape[0]
    d_a = yf.shape[1]
    first_head = lax.broadcasted_iota(I32, (1, LANES), 1) < HEAD_A
    dir_row = lax.broadcasted_iota(I32, (LANES, 1), 0) // HEAD_A
    la = lora[:, LANES:]
    a_sum = jnp.zeros((tm, d_a), F32)
    for d in range(2):
        a_sum = a_sum + _sigmoid(a0[d:d + 1, :] + _dot_x3(la, jnp.where(dir_row == d, a2[...], 0.0)))
    rkk = r[...] * k[...] * (2.0 + (a_sum - 2.0) * ka[...]) * rk[...]
    y = yf[...] + yb[...]
    inv = 1.0 / HEAD_A
    for j in range(d_a // LANES):
        sl = slice(j * LANES, (j + 1) * LANES)
        yj = y[:, sl]
        mu = _head_sum(yj, first_head) * inv
        dl = yj - mu
        var = _head_sum(dl * dl, first_head) * inv
        yn = dl * lax.rsqrt(var + GN_EPS) * lxw[:, sl] + lxb[:, sl]
        bonus = _head_sum(rkk[:, sl], first_head) * v[:, sl]
        o_ref[:, sl] = ((yn + bonus) * _sigmoid(g[:, sl])).astype(BF16)

    uu = _gelu(u[...])
    vv = _gelu(vg[...])
    mu = jnp.mean(vv, axis=-1, keepdims=True)
    dv = vv - mu
    vn = dv * lax.rsqrt(jnp.mean(dv * dv, axis=-1, keepdims=True) + EPS) * lvw[...] + lvb[...]
    for ch in range(tm // GMLP_CHUNK):
        rows = slice(ch * GMLP_CHUNK, (ch + 1) * GMLP_CHUNK)
        for h in range(ws.shape[0]):
            cols = slice(h * LANES, (h + 1) * LANES)
            sp = _dot(ws[h], vn[rows, cols]) + bst[:, h:h + 1]
            o_ref[rows, d_a + h * LANES:d_a + (h + 1) * LANES] = (uu[rows, cols] * sp).astype(BF16)


def _post_mix(yf, yb, zs, z, a0, a2r, k_a, r_k, lxw, lxb, lvw, lvb, w_s, b_st):
    n, d_a = yf.shape
    tm = 256
    wide = lambda j: (lambda i: (i, j))
    full = lambda shape: pl.BlockSpec(shape, lambda i: (0,) * len(shape))
    lora_blk = (4 * d_a) // (2 * LANES)
    return pl.pallas_call(
        _post_kernel,
        out_shape=jax.ShapeDtypeStruct((n, 2 * d_a), BF16),
        grid=(n // tm,),
        in_specs=[pl.BlockSpec((tm, d_a), wide(0)), pl.BlockSpec((tm, d_a), wide(0)),
                  pl.BlockSpec((tm, d_a), wide(0)), pl.BlockSpec((tm, d_a), wide(1)),
                  pl.BlockSpec((tm, d_a), wide(2)), pl.BlockSpec((tm, d_a), wide(3)),
                  pl.BlockSpec((tm, 2 * LANES), wide(lora_blk)),
                  pl.BlockSpec((tm, d_a), wide(4)), pl.BlockSpec((tm, d_a), wide(5)),
                  full((2, d_a)), full((LANES, d_a)), full((1, d_a)), full((1, d_a)),
                  full((1, d_a)), full((1, d_a)), full((1, d_a)), full((1, d_a)),
                  full(w_s.shape), full(b_st.shape)],
        out_specs=pl.BlockSpec((tm, 2 * d_a), wide(0)),
        compiler_params=_params(("parallel",)),
        name="mix_post",
    )(yf, yb, zs, zs, zs, zs, zs, z, z, a0, a2r, k_a, r_k, lxw, lxb, lvw, lvb, w_s, b_st)


def _rms(x, g):
    return x * lax.rsqrt(jnp.mean(x * x, axis=-1, keepdims=True) + EPS) * g


def _out_proj_kernel(a_ref, w_ref, x_ref, mod_ref, gpost_ref, gpre_ref, x1_ref, h2_ref):
    o = jnp.dot(a_ref[...], w_ref[...], preferred_element_type=F32)
    x1 = x_ref[...] + mod_ref[0, 2:3, :] * _rms(o, gpost_ref[...])
    x1_ref[...] = x1
    h2_ref[...] = _rms(x1, gpre_ref[...]) * (1.0 + mod_ref[0, 4:5, :]) + mod_ref[0, 3:4, :]


def _out_proj(yag, w_bf, x, modm, g_post, g_pre2, group0, rows_per_group):
    n, d = x.shape
    tm = 256
    grp = lambda i: (group0 + (i * tm) // rows_per_group, 0, 0)
    row = lambda i: (i, 0)
    fix = lambda i: (0, 0)
    shp = jax.ShapeDtypeStruct((n, d), F32)
    return pl.pallas_call(
        _out_proj_kernel,
        out_shape=(shp, shp),
        grid=(n // tm,),
        in_specs=[pl.BlockSpec((tm, yag.shape[1]), row), pl.BlockSpec(w_bf.shape, fix),
                  pl.BlockSpec((tm, d), row), pl.BlockSpec((1, 6, d), grp),
                  pl.BlockSpec((1, d), fix), pl.BlockSpec((1, d), fix)],
        out_specs=(pl.BlockSpec((tm, d), row), pl.BlockSpec((tm, d), row)),
        compiler_params=_params(("parallel",)),
        name="out_proj",
    )(yag, w_bf, x, modm, g_post, g_pre2)


def _query_kernel(h_ref, wq_ref, sk_ref, o_ref):
    q = jnp.dot(h_ref[...].astype(BF16), wq_ref[...], preferred_element_type=F32)
    for g in range(sk_ref.shape[0]):
        o_ref[g * N_KEYS:(g + 1) * N_KEYS, :] = _dot_nt(sk_ref[g], q[:, g * LANES:(g + 1) * LANES])


def _query_scores(h2, wq_bf, sk):
    n, d = h2.shape
    tm = 256
    groups = sk.shape[0]
    return pl.pallas_call(
        _query_kernel,
        out_shape=jax.ShapeDtypeStruct((groups * N_KEYS, n), F32),
        grid=(n // tm,),
        in_specs=[pl.BlockSpec((tm, d), lambda i: (i, 0)),
                  pl.BlockSpec(wq_bf.shape, lambda i: (0, 0)),
                  pl.BlockSpec(sk.shape, lambda i: (0, 0, 0))],
        out_specs=pl.BlockSpec((groups * N_KEYS, tm), lambda i: (0, i)),
        compiler_params=_params(("parallel",)),
        name="peer_query",
    )(h2, wq_bf, sk)


def _top16(s):
    rows = s.shape[0]
    iota = lax.broadcasted_iota(I32, s.shape, 0)
    vals, idxs = [], []
    for _ in range(TOPK):
        m = jnp.max(s, axis=0, keepdims=True)
        i = jnp.min(jnp.where(s == m, iota, rows), axis=0, keepdims=True)
        vals.append(m)
        idxs.append(i)
        s = jnp.where(iota == i, -jnp.inf, s)
    return jnp.concatenate(vals, axis=0), jnp.concatenate(idxs, axis=0)


def _topk_kernel(s_ref, idx_ref, gate_ref):
    def head(h, carry):
        base = pl.multiple_of(h * (2 * N_KEYS), 2 * N_KEYS)
        v1, i1 = _top16(s_ref[pl.ds(base, N_KEYS), :])
        v2, i2 = _top16(s_ref[pl.ds(base + N_KEYS, N_KEYS), :])
        cand = jnp.concatenate([v1[i:i + 1, :] + v2 for i in range(TOPK)], axis=0)
        top_s, pos = _top16(cand)
        pi = pos // TOPK
        pj = pos % TOPK
        e1 = jnp.zeros_like(pos)
        e2 = jnp.zeros_like(pos)
        for i in range(TOPK):
            e1 = e1 + jnp.where(pi == i, i1[i:i + 1, :], 0)
            e2 = e2 + jnp.where(pj == i, i2[i:i + 1, :], 0)
        ex = jnp.exp(top_s - top_s[0:1, :])
        out = pl.multiple_of(h * TOPK, TOPK)
        idx_ref[pl.ds(out, TOPK), :] = e1 * N_KEYS + e2
        gate_ref[pl.ds(out, TOPK), :] = ex / jnp.sum(ex, axis=0, keepdims=True)
        return carry

    lax.fori_loop(0, PK_HEADS, head, 0)


def _peer_topk(scores):
    rows, n = scores.shape
    tt = 256
    out_rows = PK_HEADS * TOPK
    return pl.pallas_call(
        _topk_kernel,
        out_shape=(jax.ShapeDtypeStruct((out_rows, n), I32), jax.ShapeDtypeStruct((out_rows, n), F32)),
        grid=(n // tt,),
        in_specs=[pl.BlockSpec((rows, tt), lambda i: (0, i))],
        out_specs=(pl.BlockSpec((out_rows, tt), lambda i: (0, i)),
                   pl.BlockSpec((out_rows, tt), lambda i: (0, i))),
        compiler_params=_params(("parallel",)),
        name="peer_topk",
    )(scores)


SC_CORES = 2
SC_SUBCORES = 16
SC_LANES = 16
PEER_GROUP = 16
PEER_BLOCK = 16
U_MASK = -65536


def _pack_experts(eu, ev):
    hi = lax.bitcast_convert_type(eu.astype(BF16), jnp.uint16).astype(jnp.uint32)
    lo = lax.bitcast_convert_type(ev.astype(BF16), jnp.uint16).astype(jnp.uint32)
    return lax.bitcast_convert_type((hi << 16) | lo, I32)


def _unpack_u(w):
    return lax.bitcast_convert_type(w & U_MASK, F32)


def _unpack_v(w):
    return lax.bitcast_convert_type(w << 16, F32)


def _peer_experts_sc(idx, x, gates, table):
    n, d = x.shape
    n_sel = idx.shape[1]
    per_w = n // (SC_CORES * SC_SUBCORES)
    L, G, TB = SC_LANES, PEER_GROUP, PEER_BLOCK
    n_groups = n_sel // G
    n_ch = d // L
    mesh = plsc.VectorSubcoreMesh(core_axis_name="c", subcore_axis_name="s")

    @functools.partial(
        pl.kernel, mesh=mesh, out_type=jax.ShapeDtypeStruct((n, d), F32),
        compiler_params=pltpu.CompilerParams(needs_layout_passes=False),
        scratch_types=[pltpu.VMEM((TB, n_sel), I32), pltpu.VMEM((TB, n_sel), F32),
                       pltpu.VMEM((d,), F32), pltpu.VMEM((d,), F32),
                       pltpu.VMEM((2, G, d), I32), pltpu.VMEM((G, 2 * L), F32), pltpu.VMEM((2 * L,), F32),
                       pltpu.SemaphoreType.DMA((2,))],
        name="peer_experts_sc")
    def k(idx_hbm, x_hbm, g_hbm, tab_hbm, o_hbm, idx_v, gate_v, x_v, o_v, buf, acc_v, coef_v, sem):
        base = (lax.axis_index("s") * SC_CORES + lax.axis_index("c")) * per_w
        lane = lax.iota(I32, L)
        zero = jnp.zeros((L,), F32)

        def gather(t, g, slot):
            return pltpu.make_async_copy(tab_hbm.at[idx_v.at[t, pl.ds(g * G, G)]], buf.at[slot], sem.at[slot])

        @pl.loop(0, per_w // TB)
        def _(blk):
            tok0 = base + blk * TB
            pltpu.sync_copy(idx_hbm.at[pl.ds(tok0, TB)], idx_v)
            pltpu.sync_copy(g_hbm.at[pl.ds(tok0, TB)], gate_v)

            @pl.loop(0, TB)
            def _(t):
                tok = tok0 + t
                pltpu.sync_copy(x_hbm.at[tok], x_v)
                gather(t, 0, 0).start()

                @plsc.parallel_loop(0, n_ch)
                def _(ch):
                    o_v[pl.ds(ch * L, L)] = zero

                for g in range(n_groups):
                    slot = g % 2
                    if g + 1 < n_groups:
                        gather(t, g + 1, 1 - slot).start()
                    gather(t, g, slot).wait()

                    def u_body(ch, accs):
                        xv = x_v[pl.ds(ch * L, L)]
                        return tuple(accs[e] + _unpack_u(buf[slot, e, pl.ds(ch * L, L)]) * xv for e in range(G))

                    accs = plsc.parallel_loop(0, n_ch, carry=tuple(zero for _ in range(G)))(u_body)
                    for e in range(G):
                        acc_v[e, pl.ds(L, L)] = accs[e]
                    act = zero
                    for l in range(L):
                        act = act + plsc.load_gather(acc_v, [lane, jnp.full((L,), L + l, I32)])
                    y = 0.7978845608028654 * (act + 0.044715 * (act * act * act))
                    coef_v[pl.ds(L, L)] = act / (1.0 + jnp.exp(-2.0 * y)) * gate_v[t, pl.ds(g * G, G)]
                    cs = [plsc.load_gather(coef_v, [jnp.full((L,), L + e, I32)]) for e in range(G)]

                    @plsc.parallel_loop(0, n_ch)
                    def _(ch):
                        o = o_v[pl.ds(ch * L, L)]
                        for e in range(G):
                            o = o + cs[e] * _unpack_v(buf[slot, e, pl.ds(ch * L, L)])
                        o_v[pl.ds(ch * L, L)] = o

                pltpu.sync_copy(o_v, o_hbm.at[tok])

    return k(idx, x, gates, table)


PEER_TC_TOKENS = 16
PEER_TC_SHARE = 3584


def _peer_tc_kernel(idx_ref, x_ref, g_ref, uv_ref, o_ref, buf, sem):
    rows, n_sel = buf.shape[1], buf.shape[2]

    def issue(t, slot):
        def body(e, carry):
            src = pl.multiple_of(idx_ref[t, e] * rows, rows)
            pltpu.make_async_copy(uv_ref.at[pl.ds(src, rows), :], buf.at[slot, :, e, :], sem.at[slot]).start()
            return carry
        lax.fori_loop(0, n_sel, body, 0, unroll=8)

    def wait(slot):
        pltpu.make_async_copy(buf.at[1 - slot], buf.at[slot], sem.at[slot]).wait()

    issue(0, 0)
    gates = g_ref[...].T
    for t in range(PEER_TC_TOKENS):
        slot = t % 2
        if t + 1 < PEER_TC_TOKENS:
            issue(t + 1, 1 - slot)
        wait(slot)
        xt = x_ref[t * rows:(t + 1) * rows, :]
        acc = _unpack_u(buf[slot, 0]) * xt[0:1, :]
        for c in range(1, rows):
            acc = acc + _unpack_u(buf[slot, c]) * xt[c:c + 1, :]
        act = jnp.sum(acc, axis=-1, keepdims=True)
        coef = jnp.broadcast_to(_gelu(act) * gates[:, t:t + 1], (n_sel, LANES))
        o_ref[t * rows:(t + 1) * rows, :] = jnp.concatenate(
            [jnp.sum(coef * _unpack_v(buf[slot, c]), axis=0, keepdims=True) for c in range(rows)], axis=0)


def _peer_experts_tc(idx, x, gates, table_rows):
    n, d = x.shape
    n_sel = idx.shape[1]
    tt = PEER_TC_TOKENS
    rows = d // LANES
    out = pl.pallas_call(
        _peer_tc_kernel,
        out_shape=jax.ShapeDtypeStruct((n * rows, LANES), F32),
        grid=(n // tt,),
        in_specs=[pl.BlockSpec((tt, n_sel), lambda i: (i, 0), memory_space=pltpu.SMEM),
                  pl.BlockSpec((tt * rows, LANES), lambda i: (i, 0)),
                  pl.BlockSpec((tt, n_sel), lambda i: (i, 0)),
                  pl.BlockSpec(memory_space=pl.ANY)],
        out_specs=pl.BlockSpec((tt * rows, LANES), lambda i: (i, 0)),
        scratch_shapes=[pltpu.VMEM((2, rows, n_sel, LANES), I32), pltpu.SemaphoreType.DMA((2,))],
        compiler_params=pltpu.CompilerParams(dimension_semantics=("arbitrary",),
                                             vmem_limit_bytes=VMEM_LIMIT, disable_bounds_checks=True),
        name="peer_experts_tc",
    )(idx, x.reshape(n * rows, LANES), gates, table_rows)
    return out.reshape(n, d)


def _peer_experts(idx, x, gates, table, table_rows):
    m = PEER_TC_SHARE
    out_tc = _peer_experts_tc(idx[:m], x[:m], gates[:m], table_rows)
    out_sc = _peer_experts_sc(idx[m:], x[m:], gates[m:], table)
    return jnp.concatenate([out_tc, out_sc], axis=0)


def _final_kernel(x1_ref, p_ref, mod_ref, g_ref, o_ref):
    o_ref[...] = x1_ref[...] + mod_ref[0, 5:6, :] * _rms(p_ref[...], g_ref[...])


def _final(x1, peer, modm, g_post2, group0, rows_per_group):
    n, d = x1.shape
    tm = 512
    row = lambda i: (i, 0)
    return pl.pallas_call(
        _final_kernel,
        out_shape=jax.ShapeDtypeStruct((n, d), F32),
        grid=(n // tm,),
        in_specs=[pl.BlockSpec((tm, d), row), pl.BlockSpec((tm, d), row),
                  pl.BlockSpec((1, 6, d), lambda i: (group0 + (i * tm) // rows_per_group, 0, 0)),
                  pl.BlockSpec((1, d), lambda i: (0, 0))],
        out_specs=pl.BlockSpec((tm, d), row),
        compiler_params=_params(("parallel",)),
        name="final_residual",
    )(x1, peer, modm, g_post2)


def _block_diag_state(s):
    b, h = s.shape[:2]
    s = s.reshape(b, h // 2, 2, HEAD_A, HEAD_A)
    z = jnp.zeros_like(s[:, :, 0])
    return jnp.concatenate([jnp.concatenate([s[:, :, 0], z], axis=-1),
                            jnp.concatenate([z, s[:, :, 1]], axis=-1)], axis=-2)


def _head_states(s2):
    b, p = s2.shape[:2]
    return jnp.stack([s2[:, :, :HEAD_A, :HEAD_A], s2[:, :, HEAD_A:, HEAD_A:]],
                     axis=2).reshape(b, 2 * p, HEAD_A, HEAD_A)


def _layer(x3, modm, group0, grid_mode, s0f, s0b, w):
    b, t, d = x3.shape
    n = b * t
    x = x3.reshape(n, d)
    rows_per_group = t if grid_mode else n
    z = _in_proj(x, modm, w["g_pre1"], w["w_in"], group0, rows_per_group)
    zs = _shift_mix(z, w["mu_shift"], w["n_shift"], w["lora_block"], grid_mode, t)
    zero_init = s0f is None
    if zero_init:
        s0f = s0b = jnp.zeros((b, w["w0"].shape[1] // LANES, LANES, LANES), F32)
    yf, yb, sf, sb = _rwkv_scan(zs.reshape(b, t, -1), w["w0"], w["w2"], w["a0"], w["a2"], w["k_k"], w["k_a"],
                                s0f, s0b, zero_init)
    d_a = yf.shape[-1]
    yag = _post_mix(yf.reshape(n, d_a), yb.reshape(n, d_a), zs, z, w["a0"], w["a2"], w["k_a"], w["r_k"],
                    w["ln_x_w"], w["ln_x_b"], w["ln_v_w"], w["ln_v_b"], w["w_s"], w["b_st"])
    x1, h2 = _out_proj(yag, w["w_out"], x, modm, w["g_post1"], w["g_pre2"], group0, rows_per_group)
    scores = _query_scores(h2, w["w_query"], w["sub_keys"])
    idx_t, gates_t = _peer_topk(scores)
    peer = _peer_experts(idx_t.T, h2, gates_t.T, w["experts"], w["expert_rows"])
    out = _final(x1, peer, modm, w["g_post2"], group0, rows_per_group)
    return out.reshape(b, t, d), sf, sb


def kernel(x_prompt, x_sample, c, state_fwd, state_bwd, c_ctx, w_ada, b_ada, g_pre1, g_post1, g_pre2, g_post2,
           w_in, mu_shift, w0, w2, a0, a2, k_k, k_a, r_k, ln_x_w, ln_x_b, ln_v_w, ln_v_b, w_s, b_s, w_out,
           w_query, sub_keys, expert_u, expert_v):
    depth = w_in.shape[0]
    d = x_prompt.shape[-1]
    d_a = w0.shape[-1]
    n_shift = mu_shift.shape[-1]
    dec_b = x_sample.shape[0]
    cvec = jnp.concatenate([c_ctx[None], c, jnp.zeros((8 - 1 - dec_b, d), F32)], axis=0)
    xp, xs = x_prompt, x_sample
    new_f, new_b = [], []
    for l in range(depth):
        wl_in = w_in[l]
        row = lambda a: a[l].reshape(1, -1)
        w = {
            "w_in": jnp.concatenate([wl_in[:, :4 * d_a], wl_in[:, n_shift:], wl_in[:, 4 * d_a:n_shift]],
                                    axis=1).astype(BF16),
            "n_shift": n_shift,
            "lora_block": (wl_in.shape[1] - (n_shift - 4 * d_a)) // (2 * LANES),
            "mu_shift": row(mu_shift),
            "g_pre1": row(g_pre1), "g_post1": row(g_post1), "g_pre2": row(g_pre2), "g_post2": row(g_post2),
            "w0": w0[l], "w2": w2[l].reshape(-1, d_a), "a0": a0[l], "a2": a2[l].reshape(-1, d_a),
            "k_k": row(k_k), "k_a": row(k_a), "r_k": row(r_k),
            "ln_x_w": row(ln_x_w), "ln_x_b": row(ln_x_b), "ln_v_w": row(ln_v_w), "ln_v_b": row(ln_v_b),
            "w_s": w_s[l].astype(BF16), "b_st": b_s[l].T,
            "w_out": w_out[l].astype(BF16), "w_query": w_query[l].astype(BF16),
            "sub_keys": jnp.swapaxes(sub_keys[l], 0, 1).reshape(-1, N_KEYS, sub_keys.shape[-1]).astype(BF16),
            "experts": _pack_experts(expert_u[l], expert_v[l]),
        }
        w["expert_rows"] = w["experts"].reshape(-1, LANES)
        modm = _modulation(cvec, w_ada[l], b_ada[l].reshape(1, -1)).reshape(8, 6, d)
        xp, sf, sb = _layer(xp, modm, 0, False, None, None, w)
        new_f.append(_head_states(sf))
        new_b.append(_head_states(sb))
        xs, _, _ = _layer(xs, modm, 1, True, _block_diag_state(state_fwd[:, l]),
                          _block_diag_state(state_bwd[:, l]), w)
    return (xp, xs, jnp.stack(new_f, axis=1), jnp.stack(new_b, axis=1))
```

```python
import functools

import jax
import jax.numpy as jnp
from jax import lax
from jax.experimental import pallas as pl
from jax.experimental.pallas import tpu as pltpu
from jax.experimental.pallas import tpu_sc as plsc

F32 = jnp.float32
BF16 = jnp.bfloat16
I32 = jnp.int32

EPS = 1e-6
GN_EPS = 64e-5
HEAD_A = 64
LANES = 128
GRID_W = 64
GMLP_CHUNK = 128
PK_HEADS = 8
N_KEYS = 128
TOPK = 16
SCAN_CHUNK = 64
SCAN_PAIRS = 8
DECAY_SCALE = 0.6065306597126334
VMEM_LIMIT = 48 * 1024 * 1024


def _params(sem):
    return pltpu.CompilerParams(dimension_semantics=sem, vmem_limit_bytes=VMEM_LIMIT)


def _sigmoid(x):
    return 1.0 / (1.0 + jnp.exp(-x))


def _gelu(x):
    return 0.5 * x * (1.0 + jnp.tanh(0.7978845608028654 * (x + 0.044715 * (x * x * x))))


def _dot(a, b):
    return jnp.dot(a.astype(BF16), b.astype(BF16), preferred_element_type=F32)


def _dot_nt(a, b):
    return lax.dot_general(a.astype(BF16), b.astype(BF16), (((1,), (1,)), ((), ())),
                           preferred_element_type=F32)


def _dot_tn(a, b):
    return lax.dot_general(a.astype(BF16), b.astype(BF16), (((0,), (0,)), ((), ())),
                           preferred_element_type=F32)


def _split(x):
    hi = x.astype(BF16)
    return hi, (x - hi.astype(F32)).astype(BF16)


def _dot_x3(a, b):
    a_hi, a_lo = _split(a)
    b_hi, b_lo = _split(b)
    dot = functools.partial(jnp.dot, preferred_element_type=F32)
    return dot(a_hi, b_hi) + (dot(a_lo, b_hi) + dot(a_hi, b_lo))


def _dot_split_rhs(a_bf, b):
    b_hi, b_mid = _split(b)
    b_lo = (b - b_hi.astype(F32) - b_mid.astype(F32)).astype(BF16)
    dot = functools.partial(jnp.dot, preferred_element_type=F32)
    return dot(a_bf, b_hi) + (dot(a_bf, b_mid) + dot(a_bf, b_lo))


def _head_sum(x, first_head):
    s_a = jnp.sum(jnp.where(first_head, x, 0.0), axis=-1, keepdims=True)
    s_b = jnp.sum(jnp.where(first_head, 0.0, x), axis=-1, keepdims=True)
    return jnp.where(first_head, s_a, s_b)


def _mod_kernel(c_ref, w_ref, b_ref, o_ref):
    c = c_ref[...]
    o_ref[...] = _dot(c * _sigmoid(c), w_ref[...]) + b_ref[...]


def _modulation(cvec, w_ada, b_ada):
    rows, d = cvec.shape
    n = w_ada.shape[1]
    tn = 1024
    return pl.pallas_call(
        _mod_kernel,
        out_shape=jax.ShapeDtypeStruct((rows, n), F32),
        grid=(n // tn,),
        in_specs=[pl.BlockSpec((rows, d), lambda j: (0, 0)),
                  pl.BlockSpec((d, tn), lambda j: (0, j)),
                  pl.BlockSpec((1, tn), lambda j: (0, j))],
        out_specs=pl.BlockSpec((rows, tn), lambda j: (0, j)),
        compiler_params=_params(("parallel",)),
        name="adaln_mod",
    )(cvec, w_ada, b_ada)


def _in_proj_kernel(x_ref, mod_ref, g_ref, w_ref, o_ref, h_ref):
    @pl.when(pl.program_id(1) == 0)
    def _():
        x = x_ref[...]
        y = x * lax.rsqrt(jnp.mean(x * x, axis=-1, keepdims=True) + EPS) * g_ref[...]
        h_ref[...] = (y * (1.0 + mod_ref[0, 1:2, :]) + mod_ref[0, 0:1, :]).astype(BF16)

    o_ref[...] = jnp.dot(h_ref[...], w_ref[...], preferred_element_type=F32)


def _in_proj(x, modm, g_pre, w_bf, group0, rows_per_group):
    n, d = x.shape
    p = w_bf.shape[1]
    tm, tn = 512, 1280
    grp = lambda i, j: (group0 + (i * tm) // rows_per_group, 0, 0)
    return pl.pallas_call(
        _in_proj_kernel,
        out_shape=jax.ShapeDtypeStruct((n, p), F32),
        grid=(n // tm, p // tn),
        in_specs=[pl.BlockSpec((tm, d), lambda i, j: (i, 0)),
                  pl.BlockSpec((1, 6, d), grp),
                  pl.BlockSpec((1, d), lambda i, j: (0, 0)),
                  pl.BlockSpec((d, tn), lambda i, j: (0, j))],
        out_specs=pl.BlockSpec((tm, tn), lambda i, j: (i, j)),
        scratch_shapes=[pltpu.VMEM((tm, d), BF16)],
        compiler_params=_params(("parallel", "arbitrary")),
        name="in_proj",
    )(x, modm, g_pre, w_bf)


def _shift_kernel(z_ref, mu_ref, o_ref, *, grid_mode, period):
    z = z_ref[...]
    rows = z.shape[0]
    t = lax.broadcasted_iota(I32, (rows, 1), 0) % period
    prev = jnp.where(t % (GRID_W if grid_mode else period) != 0, pltpu.roll(z, 1, 0), 0.0)
    nxt = jnp.where(t % (GRID_W if grid_mode else period) != (GRID_W if grid_mode else period) - 1,
                    pltpu.roll(z, rows - 1, 0), 0.0)
    if grid_mode:
        up = jnp.where(t >= GRID_W, pltpu.roll(z, GRID_W, 0), 0.0)
        down = jnp.where(t < period - GRID_W, pltpu.roll(z, rows - GRID_W, 0), 0.0)
        nb = 0.25 * (up + down + prev + nxt)
    else:
        nb = 0.5 * (prev + nxt)
    o_ref[...] = z + mu_ref[...] * (nb - z)


def _shift_mix(z, mu, n_shift, lora_block, grid_mode, period):
    n = z.shape[0]
    tr, tc = 2048, 256
    main_blocks = (n_shift // tc) - 1
    col = lambda i, j: (i, jnp.where(j < main_blocks, j, lora_block))
    return pl.pallas_call(
        functools.partial(_shift_kernel, grid_mode=grid_mode, period=period),
        out_shape=jax.ShapeDtypeStruct((n, n_shift), F32),
        grid=(n // tr, n_shift // tc),
        in_specs=[pl.BlockSpec((tr, tc), col),
                  pl.BlockSpec((1, tc), lambda i, j: (0, j))],
        out_specs=pl.BlockSpec((tr, tc), lambda i, j: (i, j)),
        compiler_params=_params(("parallel", "parallel")),
        name="token_shift",
    )(z, mu)


def _scan_chunks(chains):
    c = chains[0][0].shape[0]
    c2 = 2 * c
    n = len(chains)
    fwd = [ch[9] for ch in chains]
    first_head = lax.broadcasted_iota(I32, (1, LANES), 1) < HEAD_A
    row = lax.broadcasted_iota(I32, (c2, c2), 0)
    col = lax.broadcasted_iota(I32, (c2, c2), 1)
    eye = jnp.where(row == col, 1.0, 0.0)

    def stack(x):
        return jnp.concatenate([jnp.where(first_head, x, 0.0), jnp.where(first_head, 0.0, x)],
                               axis=0).astype(BF16)

    lhs, rhs, v2, total = [], [], [], []
    for r, k, v, logw, cl, a, k_k, k_a, _, forward in chains:
        kkr = k * k_k
        kk = kkr / jnp.maximum(jnp.sqrt(_head_sum(kkr * kkr, first_head)), 1e-12)
        kd = k * (1.0 + (a - 1.0) * k_a)
        gi = jnp.exp(-cl)
        lhs.append(jnp.concatenate([stack(kk * jnp.exp(cl - logw)), stack(r * jnp.exp(cl))], axis=0))
        rhs.append(jnp.concatenate([stack(kk * a * gi), stack(kd * gi)], axis=0))
        v2.append(stack(v))
        total.append(cl[c - 1:c, :] if forward else cl[0:1, :])

    res = [_dot_nt(lhs[i], rhs[i]) for i in range(n)]
    pr = [_dot_nt(lhs[i], chains[i][8]) for i in range(n)]
    strict = [(row > col) if f else (row < col) for f in fwd]
    incl = [(row >= col) if f else (row <= col) for f in fwd]
    ab = [jnp.where(strict[i], res[i][:c2, :c2], 0.0) for i in range(n)]
    ak = [jnp.where(strict[i], res[i][:c2, c2:], 0.0) for i in range(n)]
    gb_gk = [jnp.concatenate([jnp.where(incl[i], res[i][c2:, :c2], 0.0),
                              jnp.where(incl[i], res[i][c2:, c2:], 0.0)], axis=1).astype(BF16) for i in range(n)]
    akv = [_dot(ak[i], v2[i]) for i in range(n)]

    tinv = [eye - ab[i] for i in range(n)]
    pw = [-ab[i] for i in range(n)]
    for _ in range(c.bit_length() - 2):
        pw = [_dot(pw[i], pw[i]) for i in range(n)]
        tinv = [tinv[i] + _dot(tinv[i], pw[i]) for i in range(n)]

    u2 = [_dot(tinv[i], -pr[i][:c2, :] - akv[i]) for i in range(n)]
    uv = [jnp.concatenate([u2[i].astype(BF16), v2[i]], axis=0) for i in range(n)]
    y2 = [pr[i][c2:, :] + _dot(gb_gk[i], uv[i]) for i in range(n)]
    s_new = [(chains[i][8] + _dot_tn(uv[i], rhs[i])) * jnp.exp(total[i]) for i in range(n)]
    return [(y2[i][:c, :] + y2[i][c:, :], s_new[i]) for i in range(n)]


def _scan_direction(r_ref, k_ref, v_ref, l_ref, d, s_ref, w0, w2, a0, a2, kk, ka, forward):
    lora = l_ref[0]
    c = lora.shape[0]
    dir_rows = (lax.broadcasted_iota(I32, (LANES, 1), 0) // HEAD_A) == d
    wl = w0[d:d + 1, :] + _dot_x3(jnp.tanh(lora[:, :LANES]), jnp.where(dir_rows, w2[...], 0.0))
    logw = -DECAY_SCALE * _sigmoid(wl)
    a = _sigmoid(a0[d:d + 1, :] + _dot_x3(lora[:, LANES:], jnp.where(dir_rows, a2[...], 0.0)))
    ti = lax.broadcasted_iota(I32, (c, c), 0)
    tj = lax.broadcasted_iota(I32, (c, c), 1)
    cum = jnp.where((tj <= ti) if forward else (tj >= ti), 1.0, 0.0).astype(BF16)
    cl = _dot_split_rhs(cum, logw)
    chains = []
    for p in range(s_ref.shape[0]):
        sl = slice(p * LANES, (p + 1) * LANES)
        chains.append((r_ref[0, :, sl], k_ref[0, :, sl], v_ref[0, :, sl], logw[:, sl], cl[:, sl],
                       a[:, sl], kk[:, sl], ka[:, sl], s_ref[p], forward))
    return chains


def _scan_store(out, y_ref, s_ref):
    for p, (y, s_new) in enumerate(out):
        y_ref[0, :, p * LANES:(p + 1) * LANES] = y
        s_ref[p] = s_new


def _scan_kernel(rf, kf, vf, lf, rb, kb, vb, lb, w0, w2, a0, a2, kk, ka, s0f, s0b,
                 yf, yb, sf, sb, s2f, s2b, *, zero_init):
    c = pl.program_id(2)

    @pl.when(c == 0)
    def _():
        if zero_init:
            s2f[...] = jnp.zeros_like(s2f)
            s2b[...] = jnp.zeros_like(s2b)
        else:
            s2f[...] = s0f[0]
            s2b[...] = s0b[0]

    chains_f = _scan_direction(rf, kf, vf, lf, 0, s2f, w0, w2, a0, a2, kk, ka, True)
    chains_b = _scan_direction(rb, kb, vb, lb, 1, s2b, w0, w2, a0, a2, kk, ka, False)
    out = _scan_chunks(chains_f + chains_b)
    _scan_store(out[:len(chains_f)], yf, s2f)
    _scan_store(out[len(chains_f):], yb, s2b)

    @pl.when(c == pl.num_programs(2) - 1)
    def _():
        sf[0] = s2f[...]
        sb[0] = s2b[...]


def _rwkv_scan(zs, w0, w2r, a0, a2r, k_k, k_a, s0f, s0b, zero_init):
    b, t, _ = zs.shape
    d_a = w0.shape[1]
    pp = SCAN_PAIRS
    wide = pp * LANES
    groups = d_a // wide
    c = SCAN_CHUNK
    nc = t // c
    lora_blk = (4 * d_a) // (2 * LANES)
    fw = lambda off: (lambda i, q, j: (i, j, off + q))
    bw = lambda off: (lambda i, q, j: (i, nc - 1 - j, off + q))
    par = lambda i, q, j: (0, q)
    st = lambda i, q, j: (i, q, 0, 0)
    blk = (1, c, wide)
    lblk = (1, c, 2 * LANES)
    sblk = (1, pp, LANES, LANES)
    yshape = jax.ShapeDtypeStruct((b, t, d_a), F32)
    sshape = jax.ShapeDtypeStruct((b, d_a // LANES, LANES, LANES), F32)
    return pl.pallas_call(
        functools.partial(_scan_kernel, zero_init=zero_init),
        out_shape=(yshape, yshape, sshape, sshape),
        grid=(b, groups, nc),
        in_specs=[pl.BlockSpec(blk, fw(0)), pl.BlockSpec(blk, fw(groups)), pl.BlockSpec(blk, fw(2 * groups)),
                  pl.BlockSpec(lblk, lambda i, q, j: (i, j, lora_blk)),
                  pl.BlockSpec(blk, bw(0)), pl.BlockSpec(blk, bw(groups)), pl.BlockSpec(blk, bw(2 * groups)),
                  pl.BlockSpec(lblk, lambda i, q, j: (i, nc - 1 - j, lora_blk)),
                  pl.BlockSpec((2, wide), par), pl.BlockSpec((LANES, wide), par),
                  pl.BlockSpec((2, wide), par), pl.BlockSpec((LANES, wide), par),
                  pl.BlockSpec((1, wide), par), pl.BlockSpec((1, wide), par),
                  pl.BlockSpec(sblk, st), pl.BlockSpec(sblk, st)],
        out_specs=(pl.BlockSpec(blk, fw(0)), pl.BlockSpec(blk, bw(0)),
                   pl.BlockSpec(sblk, st), pl.BlockSpec(sblk, st)),
        scratch_shapes=[pltpu.VMEM((pp, LANES, LANES), F32), pltpu.VMEM((pp, LANES, LANES), F32)],
        compiler_params=_params(("parallel", "parallel", "arbitrary")),
        name="rwkv7_scan",
    )(zs, zs, zs, zs, zs, zs, zs, zs, w0, w2r, a0, a2r, k_k, k_a, s0f, s0b)


def _post_kernel(yf, yb, r, k, v, g, lora, u, vg, a0, a2, ka, rk, lxw, lxb, lvw, lvb, ws, bst, o_ref):
    tm = yf.shape[0]
    d_a = yf.shape[1]
    first_head = lax.broadcasted_iota(I32, (1, LANES), 1) < HEAD_A
    dir_row = lax.broadcasted_iota(I32, (LANES, 1), 0) // HEAD_A
    la = lora[:, LANES:]
    a_sum = jnp.zeros((tm, d_a), F32)
    for d in range(2):
        a_sum = a_sum + _sigmoid(a0[d:d + 1, :] + _dot_x3(la, jnp.where(dir_row == d, a2[...], 0.0)))
    rkk = r[...] * k[...] * (2.0 + (a_sum - 2.0) * ka[...]) * rk[...]
    y = yf[...] + yb[...]
    inv = 1.0 / HEAD_A
    for j in range(d_a // LANES):
        sl = slice(j * LANES, (j + 1) * LANES)
        yj = y[:, sl]
        mu = _head_sum(yj, first_head) * inv
        dl = yj - mu
        var = _head_sum(dl * dl, first_head) * inv
        yn = dl * lax.rsqrt(var + GN_EPS) * lxw[:, sl] + lxb[:, sl]
        bonus = _head_sum(rkk[:, sl], first_head) * v[:, sl]
        o_ref[:, sl] = ((yn + bonus) * _sigmoid(g[:, sl])).astype(BF16)

    uu = _gelu(u[...])
    vv = _gelu(vg[...])
    mu = jnp.mean(vv, axis=-1, keepdims=True)
    dv = vv - mu
    vn = dv * lax.rsqrt(jnp.mean(dv * dv, axis=-1, keepdims=True) + EPS) * lvw[...] + lvb[...]
    for ch in range(tm // GMLP_CHUNK):
        rows = slice(ch * GMLP_CHUNK, (ch + 1) * GMLP_CHUNK)
        for h in range(ws.shape[0]):
            cols = slice(h * LANES, (h + 1) * LANES)
            sp = _dot(ws[h], vn[rows, cols]) + bst[:, h:h + 1]
            o_ref[rows, d_a + h * LANES:d_a + (h + 1) * LANES] = (uu[rows, cols] * sp).astype(BF16)


def _post_mix(yf, yb, zs, z, a0, a2r, k_a, r_k, lxw, lxb, lvw, lvb, w_s, b_st):
    n, d_a = yf.shape
    tm = 256
    wide = lambda j: (lambda i: (i, j))
    full = lambda shape: pl.BlockSpec(shape, lambda i: (0,) * len(shape))
    lora_blk = (4 * d_a) // (2 * LANES)
    return pl.pallas_call(
        _post_kernel,
        out_shape=jax.ShapeDtypeStruct((n, 2 * d_a), BF16),
        grid=(n // tm,),
        in_specs=[pl.BlockSpec((tm, d_a), wide(0)), pl.BlockSpec((tm, d_a), wide(0)),
                  pl.BlockSpec((tm, d_a), wide(0)), pl.BlockSpec((tm, d_a), wide(1)),
                  pl.BlockSpec((tm, d_a), wide(2)), pl.BlockSpec((tm, d_a), wide(3)),
                  pl.BlockSpec((tm, 2 * LANES), wide(lora_blk)),
                  pl.BlockSpec((tm, d_a), wide(4)), pl.BlockSpec((tm, d_a), wide(5)),
                  full((2, d_a)), full((LANES, d_a)), full((1, d_a)), full((1, d_a)),
                  full((1, d_a)), full((1, d_a)), full((1, d_a)), full((1, d_a)),
                  full(w_s.shape), full(b_st.shape)],
        out_specs=pl.BlockSpec((tm, 2 * d_a), wide(0)),
        compiler_params=_params(("parallel",)),
        name="mix_post",
    )(yf, yb, zs, zs, zs, zs, zs, z, z, a0, a2r, k_a, r_k, lxw, lxb, lvw, lvb, w_s, b_st)


def _rms(x, g):
    return x * lax.rsqrt(jnp.mean(x * x, axis=-1, keepdims=True) + EPS) * g


def _out_proj_kernel(a_ref, w_ref, x_ref, mod_ref, gpost_ref, gpre_ref, x1_ref, h2_ref):
    o = jnp.dot(a_ref[...], w_ref[...], preferred_element_type=F32)
    x1 = x_ref[...] + mod_ref[0, 2:3, :] * _rms(o, gpost_ref[...])
    x1_ref[...] = x1
    h2_ref[...] = _rms(x1, gpre_ref[...]) * (1.0 + mod_ref[0, 4:5, :]) + mod_ref[0, 3:4, :]


def _out_proj(yag, w_bf, x, modm, g_post, g_pre2, group0, rows_per_group):
    n, d = x.shape
    tm = 256
    grp = lambda i: (group0 + (i * tm) // rows_per_group, 0, 0)
    row = lambda i: (i, 0)
    fix = lambda i: (0, 0)
    shp = jax.ShapeDtypeStruct((n, d), F32)
    return pl.pallas_call(
        _out_proj_kernel,
        out_shape=(shp, shp),
        grid=(n // tm,),
        in_specs=[pl.BlockSpec((tm, yag.shape[1]), row), pl.BlockSpec(w_bf.shape, fix),
                  pl.BlockSpec((tm, d), row), pl.BlockSpec((1, 6, d), grp),
                  pl.BlockSpec((1, d), fix), pl.BlockSpec((1, d), fix)],
        out_specs=(pl.BlockSpec((tm, d), row), pl.BlockSpec((tm, d), row)),
        compiler_params=_params(("parallel",)),
        name="out_proj",
    )(yag, w_bf, x, modm, g_post, g_pre2)


def _query_kernel(h_ref, wq_ref, sk_ref, o_ref):
    q = jnp.dot(h_ref[...].astype(BF16), wq_ref[...], preferred_element_type=F32)
    for g in range(sk_ref.shape[0]):
        o_ref[g * N_KEYS:(g + 1) * N_KEYS, :] = _dot_nt(sk_ref[g], q[:, g * LANES:(g + 1) * LANES])


def _query_scores(h2, wq_bf, sk):
    n, d = h2.shape
    tm = 256
    groups = sk.shape[0]
    return pl.pallas_call(
        _query_kernel,
        out_shape=jax.ShapeDtypeStruct((groups * N_KEYS, n), F32),
        grid=(n // tm,),
        in_specs=[pl.BlockSpec((tm, d), lambda i: (i, 0)),
                  pl.BlockSpec(wq_bf.shape, lambda i: (0, 0)),
                  pl.BlockSpec(sk.shape, lambda i: (0, 0, 0))],
        out_specs=pl.BlockSpec((groups * N_KEYS, tm), lambda i: (0, i)),
        compiler_params=_params(("parallel",)),
        name="peer_query",
    )(h2, wq_bf, sk)


def _top16(s):
    rows = s.shape[0]
    iota = lax.broadcasted_iota(I32, s.shape, 0)
    vals, idxs = [], []
    for _ in range(TOPK):
        m = jnp.max(s, axis=0, keepdims=True)
        i = jnp.min(jnp.where(s == m, iota, rows), axis=0, keepdims=True)
        vals.append(m)
        idxs.append(i)
        s = jnp.where(iota == i, -jnp.inf, s)
    return jnp.concatenate(vals, axis=0), jnp.concatenate(idxs, axis=0)


def _topk_kernel(s_ref, idx_ref, gate_ref):
    def head(h, carry):
        base = pl.multiple_of(h * (2 * N_KEYS), 2 * N_KEYS)
        v1, i1 = _top16(s_ref[pl.ds(base, N_KEYS), :])
        v2, i2 = _top16(s_ref[pl.ds(base + N_KEYS, N_KEYS), :])
        cand = jnp.concatenate([v1[i:i + 1, :] + v2 for i in range(TOPK)], axis=0)
        top_s, pos = _top16(cand)
        pi = pos // TOPK
        pj = pos % TOPK
        e1 = jnp.zeros_like(pos)
        e2 = jnp.zeros_like(pos)
        for i in range(TOPK):
            e1 = e1 + jnp.where(pi == i, i1[i:i + 1, :], 0)
            e2 = e2 + jnp.where(pj == i, i2[i:i + 1, :], 0)
        ex = jnp.exp(top_s - top_s[0:1, :])
        out = pl.multiple_of(h * TOPK, TOPK)
        idx_ref[pl.ds(out, TOPK), :] = e1 * N_KEYS + e2
        gate_ref[pl.ds(out, TOPK), :] = ex / jnp.sum(ex, axis=0, keepdims=True)
        return carry

    lax.fori_loop(0, PK_HEADS, head, 0)


def _peer_topk(scores):
    rows, n = scores.shape
    tt = 256
    out_rows = PK_HEADS * TOPK
    return pl.pallas_call(
        _topk_kernel,
        out_shape=(jax.ShapeDtypeStruct((out_rows, n), I32), jax.ShapeDtypeStruct((out_rows, n), F32)),
        grid=(n // tt,),
        in_specs=[pl.BlockSpec((rows, tt), lambda i: (0, i))],
        out_specs=(pl.BlockSpec((out_rows, tt), lambda i: (0, i)),
                   pl.BlockSpec((out_rows, tt), lambda i: (0, i))),
        compiler_params=_params(("parallel",)),
        name="peer_topk",
    )(scores)


SC_CORES = 2
SC_SUBCORES = 16
SC_LANES = 16
PEER_GROUP = 16
PEER_BLOCK = 16
U_MASK = -65536


def _pack_experts(eu, ev):
    hi = lax.bitcast_convert_type(eu.astype(BF16), jnp.uint16).astype(jnp.uint32)
    lo = lax.bitcast_convert_type(ev.astype(BF16), jnp.uint16).astype(jnp.uint32)
    return lax.bitcast_convert_type((hi << 16) | lo, I32)


def _unpack_u(w):
    return lax.bitcast_convert_type(w & U_MASK, F32)


def _unpack_v(w):
    return lax.bitcast_convert_type(w << 16, F32)


def _peer_experts_sc(idx, x, gates, table):
    n, d = x.shape
    n_sel = idx.shape[1]
    per_w = n // (SC_CORES * SC_SUBCORES)
    L, G, TB = SC_LANES, PEER_GROUP, PEER_BLOCK
    n_groups = n_sel // G
    n_ch = d // L
    mesh = plsc.VectorSubcoreMesh(core_axis_name="c", subcore_axis_name="s")

    @functools.partial(
        pl.kernel, mesh=mesh, out_type=jax.ShapeDtypeStruct((n, d), F32),
        compiler_params=pltpu.CompilerParams(needs_layout_passes=False),
        scratch_types=[pltpu.VMEM((TB, n_sel), I32), pltpu.VMEM((TB, n_sel), F32),
                       pltpu.VMEM((d,), F32), pltpu.VMEM((d,), F32),
                       pltpu.VMEM((2, G, d), I32), pltpu.VMEM((G, 2 * L), F32), pltpu.VMEM((2 * L,), F32),
                       pltpu.SemaphoreType.DMA((2,))],
        name="peer_experts_sc")
    def k(idx_hbm, x_hbm, g_hbm, tab_hbm, o_hbm, idx_v, gate_v, x_v, o_v, buf, acc_v, coef_v, sem):
        base = (lax.axis_index("s") * SC_CORES + lax.axis_index("c")) * per_w
        lane = lax.iota(I32, L)
        zero = jnp.zeros((L,), F32)

        def gather(t, g, slot):
            return pltpu.make_async_copy(tab_hbm.at[idx_v.at[t, pl.ds(g * G, G)]], buf.at[slot], sem.at[slot])

        @pl.loop(0, per_w // TB)
        def _(blk):
            tok0 = base + blk * TB
            pltpu.sync_copy(idx_hbm.at[pl.ds(tok0, TB)], idx_v)
            pltpu.sync_copy(g_hbm.at[pl.ds(tok0, TB)], gate_v)

            @pl.loop(0, TB)
            def _(t):
                tok = tok0 + t
                pltpu.sync_copy(x_hbm.at[tok], x_v)
                gather(t, 0, 0).start()

                @plsc.parallel_loop(0, n_ch)
                def _(ch):
                    o_v[pl.ds(ch * L, L)] = zero

                for g in range(n_groups):
                    slot = g % 2
                    if g + 1 < n_groups:
                        gather(t, g + 1, 1 - slot).start()
                    gather(t, g, slot).wait()

                    def u_body(ch, accs):
                        xv = x_v[pl.ds(ch * L, L)]
                        return tuple(accs[e] + _unpack_u(buf[slot, e, pl.ds(ch * L, L)]) * xv for e in range(G))

                    accs = plsc.parallel_loop(0, n_ch, carry=tuple(zero for _ in range(G)))(u_body)
                    for e in range(G):
                        acc_v[e, pl.ds(L, L)] = accs[e]
                    act = zero
                    for l in range(L):
                        act = act + plsc.load_gather(acc_v, [lane, jnp.full((L,), L + l, I32)])
                    y = 0.7978845608028654 * (act + 0.044715 * (act * act * act))
                    coef_v[pl.ds(L, L)] = act / (1.0 + jnp.exp(-2.0 * y)) * gate_v[t, pl.ds(g * G, G)]
                    cs = [plsc.load_gather(coef_v, [jnp.full((L,), L + e, I32)]) for e in range(G)]

                    @plsc.parallel_loop(0, n_ch)
                    def _(ch):
                        o = o_v[pl.ds(ch * L, L)]
                        for e in range(G):
                            o = o + cs[e] * _unpack_v(buf[slot, e, pl.ds(ch * L, L)])
                        o_v[pl.ds(ch * L, L)] = o

                pltpu.sync_copy(o_v, o_hbm.at[tok])

    return k(idx, x, gates, table)


PEER_TC_TOKENS = 16
PEER_TC_SHARE = 3584


def _peer_tc_kernel(idx_ref, x_ref, g_ref, uv_ref, o_ref, buf, sem):
    rows, n_sel = buf.shape[1], buf.shape[2]

    per_step = n_sel // (2 * rows)

    def issue(t, slot, lo, hi):
        for e in range(lo, hi):
            src = pl.multiple_of(idx_ref[t, e] * rows, rows)
            pltpu.make_async_copy(uv_ref.at[pl.ds(src, rows), :], buf.at[slot, :, e, :], sem.at[slot]).start()

    def wait(slot):
        pltpu.make_async_copy(buf.at[1 - slot], buf.at[slot], sem.at[slot]).wait()

    issue(0, 0, 0, n_sel)
    gates = g_ref[...].T
    for t in range(PEER_TC_TOKENS):
        slot = t % 2
        nxt = t + 1 < PEER_TC_TOKENS
        wait(slot)
        xt = x_ref[t * rows:(t + 1) * rows, :]
        acc = None
        for c in range(rows):
            term = _unpack_u(buf[slot, c]) * xt[c:c + 1, :]
            acc = term if acc is None else acc + term
            if nxt:
                issue(t + 1, 1 - slot, c * per_step, (c + 1) * per_step)
        act = jnp.sum(acc, axis=-1, keepdims=True)
        coef = jnp.broadcast_to(_gelu(act) * gates[:, t:t + 1], (n_sel, LANES))
        out_rows = []
        for c in range(rows):
            out_rows.append(jnp.sum(coef * _unpack_v(buf[slot, c]), axis=0, keepdims=True))
            if nxt:
                issue(t + 1, 1 - slot, (rows + c) * per_step, (rows + c + 1) * per_step)
        o_ref[t * rows:(t + 1) * rows, :] = jnp.concatenate(out_rows, axis=0)


def _peer_experts_tc(idx, x, gates, table_rows):
    n, d = x.shape
    n_sel = idx.shape[1]
    tt = PEER_TC_TOKENS
    rows = d // LANES
    out = pl.pallas_call(
        _peer_tc_kernel,
        out_shape=jax.ShapeDtypeStruct((n * rows, LANES), F32),
        grid=(n // tt,),
        in_specs=[pl.BlockSpec((tt, n_sel), lambda i: (i, 0), memory_space=pltpu.SMEM),
                  pl.BlockSpec((tt * rows, LANES), lambda i: (i, 0)),
                  pl.BlockSpec((tt, n_sel), lambda i: (i, 0)),
                  pl.BlockSpec(memory_space=pl.ANY)],
        out_specs=pl.BlockSpec((tt * rows, LANES), lambda i: (i, 0)),
        scratch_shapes=[pltpu.VMEM((2, rows, n_sel, LANES), I32), pltpu.SemaphoreType.DMA((2,))],
        compiler_params=pltpu.CompilerParams(dimension_semantics=("arbitrary",),
                                             vmem_limit_bytes=VMEM_LIMIT, disable_bounds_checks=True),
        name="peer_experts_tc",
    )(idx, x.reshape(n * rows, LANES), gates, table_rows)
    return out.reshape(n, d)


def _peer_experts(idx, x, gates, table, table_rows):
    m = PEER_TC_SHARE
    out_tc = _peer_experts_tc(idx[:m], x[:m], gates[:m], table_rows)
    out_sc = _peer_experts_sc(idx[m:], x[m:], gates[m:], table)
    return jnp.concatenate([out_tc, out_sc], axis=0)


def _final_kernel(x1_ref, p_ref, mod_ref, g_ref, o_ref):
    o_ref[...] = x1_ref[...] + mod_ref[0, 5:6, :] * _rms(p_ref[...], g_ref[...])


def _final(x1, peer, modm, g_post2, group0, rows_per_group):
    n, d = x1.shape
    tm = 512
    row = lambda i: (i, 0)
    return pl.pallas_call(
        _final_kernel,
        out_shape=jax.ShapeDtypeStruct((n, d), F32),
        grid=(n // tm,),
        in_specs=[pl.BlockSpec((tm, d), row), pl.BlockSpec((tm, d), row),
                  pl.BlockSpec((1, 6, d), lambda i: (group0 + (i * tm) // rows_per_group, 0, 0)),
                  pl.BlockSpec((1, d), lambda i: (0, 0))],
        out_specs=pl.BlockSpec((tm, d), row),
        compiler_params=_params(("parallel",)),
        name="final_residual",
    )(x1, peer, modm, g_post2)


def _block_diag_state(s):
    b, h = s.shape[:2]
    s = s.reshape(b, h // 2, 2, HEAD_A, HEAD_A)
    z = jnp.zeros_like(s[:, :, 0])
    return jnp.concatenate([jnp.concatenate([s[:, :, 0], z], axis=-1),
                            jnp.concatenate([z, s[:, :, 1]], axis=-1)], axis=-2)


def _head_states(s2):
    b, p = s2.shape[:2]
    return jnp.stack([s2[:, :, :HEAD_A, :HEAD_A], s2[:, :, HEAD_A:, HEAD_A:]],
                     axis=2).reshape(b, 2 * p, HEAD_A, HEAD_A)


def _layer(x3, modm, group0, grid_mode, s0f, s0b, w):
    b, t, d = x3.shape
    n = b * t
    x = x3.reshape(n, d)
    rows_per_group = t if grid_mode else n
    z = _in_proj(x, modm, w["g_pre1"], w["w_in"], group0, rows_per_group)
    zs = _shift_mix(z, w["mu_shift"], w["n_shift"], w["lora_block"], grid_mode, t)
    zero_init = s0f is None
    if zero_init:
        s0f = s0b = jnp.zeros((b, w["w0"].shape[1] // LANES, LANES, LANES), F32)
    yf, yb, sf, sb = _rwkv_scan(zs.reshape(b, t, -1), w["w0"], w["w2"], w["a0"], w["a2"], w["k_k"], w["k_a"],
                                s0f, s0b, zero_init)
    d_a = yf.shape[-1]
    yag = _post_mix(yf.reshape(n, d_a), yb.reshape(n, d_a), zs, z, w["a0"], w["a2"], w["k_a"], w["r_k"],
                    w["ln_x_w"], w["ln_x_b"], w["ln_v_w"], w["ln_v_b"], w["w_s"], w["b_st"])
    x1, h2 = _out_proj(yag, w["w_out"], x, modm, w["g_post1"], w["g_pre2"], group0, rows_per_group)
    scores = _query_scores(h2, w["w_query"], w["sub_keys"])
    idx_t, gates_t = _peer_topk(scores)
    peer = _peer_experts(idx_t.T, h2, gates_t.T, w["experts"], w["expert_rows"])
    out = _final(x1, peer, modm, w["g_post2"], group0, rows_per_group)
    return out.reshape(b, t, d), sf, sb


def kernel(x_prompt, x_sample, c, state_fwd, state_bwd, c_ctx, w_ada, b_ada, g_pre1, g_post1, g_pre2, g_post2,
           w_in, mu_shift, w0, w2, a0, a2, k_k, k_a, r_k, ln_x_w, ln_x_b, ln_v_w, ln_v_b, w_s, b_s, w_out,
           w_query, sub_keys, expert_u, expert_v):
    depth = w_in.shape[0]
    d = x_prompt.shape[-1]
    d_a = w0.shape[-1]
    n_shift = mu_shift.shape[-1]
    dec_b = x_sample.shape[0]
    cvec = jnp.concatenate([c_ctx[None], c, jnp.zeros((8 - 1 - dec_b, d), F32)], axis=0)
    xp, xs = x_prompt, x_sample
    new_f, new_b = [], []
    for l in range(depth):
        wl_in = w_in[l]
        row = lambda a: a[l].reshape(1, -1)
        w = {
            "w_in": jnp.concatenate([wl_in[:, :4 * d_a], wl_in[:, n_shift:], wl_in[:, 4 * d_a:n_shift]],
                                    axis=1).astype(BF16),
            "n_shift": n_shift,
            "lora_block": (wl_in.shape[1] - (n_shift - 4 * d_a)) // (2 * LANES),
            "mu_shift": row(mu_shift),
            "g_pre1": row(g_pre1), "g_post1": row(g_post1), "g_pre2": row(g_pre2), "g_post2": row(g_post2),
            "w0": w0[l], "w2": w2[l].reshape(-1, d_a), "a0": a0[l], "a2": a2[l].reshape(-1, d_a),
            "k_k": row(k_k), "k_a": row(k_a), "r_k": row(r_k),
            "ln_x_w": row(ln_x_w), "ln_x_b": row(ln_x_b), "ln_v_w": row(ln_v_w), "ln_v_b": row(ln_v_b),
            "w_s": w_s[l].astype(BF16), "b_st": b_s[l].T,
            "w_out": w_out[l].astype(BF16), "w_query": w_query[l].astype(BF16),
            "sub_keys": jnp.swapaxes(sub_keys[l], 0, 1).reshape(-1, N_KEYS, sub_keys.shape[-1]).astype(BF16),
            "experts": _pack_experts(expert_u[l], expert_v[l]),
        }
        w["expert_rows"] = w["experts"].reshape(-1, LANES)
        modm = _modulation(cvec, w_ada[l], b_ada[l].reshape(1, -1)).reshape(8, 6, d)
        xp, sf, sb = _layer(xp, modm, 0, False, None, None, w)
        new_f.append(_head_states(sf))
        new_b.append(_head_states(sb))
        xs, _, _ = _layer(xs, modm, 1, True, _block_diag_state(state_fwd[:, l]),
                          _block_diag_state(state_bwd[:, l]), w)
    return (xp, xs, jnp.stack(new_f, axis=1), jnp.stack(new_b, axis=1))
```

```python
import functools

import jax
import jax.numpy as jnp
from jax import lax
from jax.experimental import pallas as pl
from jax.experimental.pallas import tpu as pltpu
from jax.experimental.pallas import tpu_sc as plsc

F32 = jnp.float32
BF16 = jnp.bfloat16
I32 = jnp.int32

EPS = 1e-6
GN_EPS = 64e-5
HEAD_A = 64
LANES = 128
GRID_W = 64
GMLP_CHUNK = 128
PK_HEADS = 8
N_KEYS = 128
TOPK = 16
SCAN_CHUNK = 64
SCAN_PAIRS = 8
DECAY_SCALE = 0.6065306597126334
VMEM_LIMIT = 48 * 1024 * 1024


def _params(sem):
    return pltpu.CompilerParams(dimension_semantics=sem, vmem_limit_bytes=VMEM_LIMIT)


def _sigmoid(x):
    return 1.0 / (1.0 + jnp.exp(-x))


def _gelu(x):
    return 0.5 * x * (1.0 + jnp.tanh(0.7978845608028654 * (x + 0.044715 * (x * x * x))))


def _dot(a, b):
    return jnp.dot(a.astype(BF16), b.astype(BF16), preferred_element_type=F32)


def _dot_nt(a, b):
    return lax.dot_general(a.astype(BF16), b.astype(BF16), (((1,), (1,)), ((), ())),
                           preferred_element_type=F32)


def _dot_tn(a, b):
    return lax.dot_general(a.astype(BF16), b.astype(BF16), (((0,), (0,)), ((), ())),
                           preferred_element_type=F32)


def _split(x):
    hi = x.astype(BF16)
    return hi, (x - hi.astype(F32)).astype(BF16)


def _dot_x3(a, b):
    a_hi, a_lo = _split(a)
    b_hi, b_lo = _split(b)
    dot = functools.partial(jnp.dot, preferred_element_type=F32)
    return dot(a_hi, b_hi) + (dot(a_lo, b_hi) + dot(a_hi, b_lo))


def _dot_split_rhs(a_bf, b):
    b_hi, b_mid = _split(b)
    b_lo = (b - b_hi.astype(F32) - b_mid.astype(F32)).astype(BF16)
    dot = functools.partial(jnp.dot, preferred_element_type=F32)
    return dot(a_bf, b_hi) + (dot(a_bf, b_mid) + dot(a_bf, b_lo))


def _head_sum(x, first_head):
    s_a = jnp.sum(jnp.where(first_head, x, 0.0), axis=-1, keepdims=True)
    s_b = jnp.sum(jnp.where(first_head, 0.0, x), axis=-1, keepdims=True)
    return jnp.where(first_head, s_a, s_b)


def _mod_kernel(c_ref, w_ref, b_ref, o_ref):
    c = c_ref[...]
    o_ref[...] = _dot(c * _sigmoid(c), w_ref[...]) + b_ref[...]


def _modulation(cvec, w_ada, b_ada):
    rows, d = cvec.shape
    n = w_ada.shape[1]
    tn = 1024
    return pl.pallas_call(
        _mod_kernel,
        out_shape=jax.ShapeDtypeStruct((rows, n), F32),
        grid=(n // tn,),
        in_specs=[pl.BlockSpec((rows, d), lambda j: (0, 0)),
                  pl.BlockSpec((d, tn), lambda j: (0, j)),
                  pl.BlockSpec((1, tn), lambda j: (0, j))],
        out_specs=pl.BlockSpec((rows, tn), lambda j: (0, j)),
        compiler_params=_params(("parallel",)),
        name="adaln_mod",
    )(cvec, w_ada, b_ada)


def _in_proj_kernel(x_ref, mod_ref, g_ref, w_ref, o_ref, h_ref):
    @pl.when(pl.program_id(1) == 0)
    def _():
        x = x_ref[...]
        y = x * lax.rsqrt(jnp.mean(x * x, axis=-1, keepdims=True) + EPS) * g_ref[...]
        h_ref[...] = (y * (1.0 + mod_ref[0, 1:2, :]) + mod_ref[0, 0:1, :]).astype(BF16)

    o_ref[...] = jnp.dot(h_ref[...], w_ref[...], preferred_element_type=F32)


def _in_proj(x, modm, g_pre, w_bf, group0, rows_per_group):
    n, d = x.shape
    p = w_bf.shape[1]
    tm, tn = 512, 1280
    grp = lambda i, j: (group0 + (i * tm) // rows_per_group, 0, 0)
    return pl.pallas_call(
        _in_proj_kernel,
        out_shape=jax.ShapeDtypeStruct((n, p), F32),
        grid=(n // tm, p // tn),
        in_specs=[pl.BlockSpec((tm, d), lambda i, j: (i, 0)),
                  pl.BlockSpec((1, 6, d), grp),
                  pl.BlockSpec((1, d), lambda i, j: (0, 0)),
                  pl.BlockSpec((d, tn), lambda i, j: (0, j))],
        out_specs=pl.BlockSpec((tm, tn), lambda i, j: (i, j)),
        scratch_shapes=[pltpu.VMEM((tm, d), BF16)],
        compiler_params=_params(("parallel", "arbitrary")),
        name="in_proj",
    )(x, modm, g_pre, w_bf)


def _shift_kernel(z_ref, mu_ref, o_ref, *, grid_mode, period):
    z = z_ref[...]
    rows = z.shape[0]
    t = lax.broadcasted_iota(I32, (rows, 1), 0) % period
    prev = jnp.where(t % (GRID_W if grid_mode else period) != 0, pltpu.roll(z, 1, 0), 0.0)
    nxt = jnp.where(t % (GRID_W if grid_mode else period) != (GRID_W if grid_mode else period) - 1,
                    pltpu.roll(z, rows - 1, 0), 0.0)
    if grid_mode:
        up = jnp.where(t >= GRID_W, pltpu.roll(z, GRID_W, 0), 0.0)
        down = jnp.where(t < period - GRID_W, pltpu.roll(z, rows - GRID_W, 0), 0.0)
        nb = 0.25 * (up + down + prev + nxt)
    else:
        nb = 0.5 * (prev + nxt)
    o_ref[...] = z + mu_ref[...] * (nb - z)


def _shift_mix(z, mu, n_shift, lora_block, grid_mode, period):
    n = z.shape[0]
    tr, tc = 2048, 256
    main_blocks = (n_shift // tc) - 1
    col = lambda i, j: (i, jnp.where(j < main_blocks, j, lora_block))
    return pl.pallas_call(
        functools.partial(_shift_kernel, grid_mode=grid_mode, period=period),
        out_shape=jax.ShapeDtypeStruct((n, n_shift), F32),
        grid=(n // tr, n_shift // tc),
        in_specs=[pl.BlockSpec((tr, tc), col),
                  pl.BlockSpec((1, tc), lambda i, j: (0, j))],
        out_specs=pl.BlockSpec((tr, tc), lambda i, j: (i, j)),
        compiler_params=_params(("parallel", "parallel")),
        name="token_shift",
    )(z, mu)


def _scan_chunks(chains):
    c = chains[0][0].shape[0]
    c2 = 2 * c
    n = len(chains)
    fwd = [ch[9] for ch in chains]
    first_head = lax.broadcasted_iota(I32, (1, LANES), 1) < HEAD_A
    row = lax.broadcasted_iota(I32, (c2, c2), 0)
    col = lax.broadcasted_iota(I32, (c2, c2), 1)
    eye = jnp.where(row == col, 1.0, 0.0)

    def stack(x):
        return jnp.concatenate([jnp.where(first_head, x, 0.0), jnp.where(first_head, 0.0, x)],
                               axis=0).astype(BF16)

    lhs, rhs, v2, total = [], [], [], []
    for r, k, v, logw, cl, a, k_k, k_a, _, forward in chains:
        kkr = k * k_k
        kk = kkr / jnp.maximum(jnp.sqrt(_head_sum(kkr * kkr, first_head)), 1e-12)
        kd = k * (1.0 + (a - 1.0) * k_a)
        gi = jnp.exp(-cl)
        lhs.append(jnp.concatenate([stack(kk * jnp.exp(cl - logw)), stack(r * jnp.exp(cl))], axis=0))
        rhs.append(jnp.concatenate([stack(kk * a * gi), stack(kd * gi)], axis=0))
        v2.append(stack(v))
        total.append(cl[c - 1:c, :] if forward else cl[0:1, :])

    res = [_dot_nt(lhs[i], rhs[i]) for i in range(n)]
    pr = [_dot_nt(lhs[i], chains[i][8]) for i in range(n)]
    strict = [(row > col) if f else (row < col) for f in fwd]
    incl = [(row >= col) if f else (row <= col) for f in fwd]
    ab = [jnp.where(strict[i], res[i][:c2, :c2], 0.0) for i in range(n)]
    ak = [jnp.where(strict[i], res[i][:c2, c2:], 0.0) for i in range(n)]
    gb_gk = [jnp.concatenate([jnp.where(incl[i], res[i][c2:, :c2], 0.0),
                              jnp.where(incl[i], res[i][c2:, c2:], 0.0)], axis=1).astype(BF16) for i in range(n)]
    akv = [_dot(ak[i], v2[i]) for i in range(n)]

    tinv = [eye - ab[i] for i in range(n)]
    pw = [-ab[i] for i in range(n)]
    for _ in range(c.bit_length() - 2):
        pw = [_dot(pw[i], pw[i]) for i in range(n)]
        tinv = [tinv[i] + _dot(tinv[i], pw[i]) for i in range(n)]

    u2 = [_dot(tinv[i], -pr[i][:c2, :] - akv[i]) for i in range(n)]
    uv = [jnp.concatenate([u2[i].astype(BF16), v2[i]], axis=0) for i in range(n)]
    y2 = [pr[i][c2:, :] + _dot(gb_gk[i], uv[i]) for i in range(n)]
    s_new = [(chains[i][8] + _dot_tn(uv[i], rhs[i])) * jnp.exp(total[i]) for i in range(n)]
    return [(y2[i][:c, :] + y2[i][c:, :], s_new[i]) for i in range(n)]


def _scan_direction(r_ref, k_ref, v_ref, l_ref, d, s_ref, w0, w2, a0, a2, kk, ka, forward):
    lora = l_ref[0]
    c = lora.shape[0]
    dir_rows = (lax.broadcasted_iota(I32, (LANES, 1), 0) // HEAD_A) == d
    wl = w0[d:d + 1, :] + _dot_x3(jnp.tanh(lora[:, :LANES]), jnp.where(dir_rows, w2[...], 0.0))
    logw = -DECAY_SCALE * _sigmoid(wl)
    a = _sigmoid(a0[d:d + 1, :] + _dot_x3(lora[:, LANES:], jnp.where(dir_rows, a2[...], 0.0)))
    ti = lax.broadcasted_iota(I32, (c, c), 0)
    tj = lax.broadcasted_iota(I32, (c, c), 1)
    cum = jnp.where((tj <= ti) if forward else (tj >= ti), 1.0, 0.0).astype(BF16)
    cl = _dot_split_rhs(cum, logw)
    chains = []
    for p in range(s_ref.shape[0]):
        sl = slice(p * LANES, (p + 1) * LANES)
        chains.append((r_ref[0, :, sl], k_ref[0, :, sl], v_ref[0, :, sl], logw[:, sl], cl[:, sl],
                       a[:, sl], kk[:, sl], ka[:, sl], s_ref[p], forward))
    return chains


def _scan_store(out, y_ref, s_ref):
    for p, (y, s_new) in enumerate(out):
        y_ref[0, :, p * LANES:(p + 1) * LANES] = y
        s_ref[p] = s_new


def _scan_kernel(rf, kf, vf, lf, rb, kb, vb, lb, w0, w2, a0, a2, kk, ka, s0f, s0b,
                 yf, yb, sf, sb, s2f, s2b, *, zero_init):
    c = pl.program_id(2)

    @pl.when(c == 0)
    def _():
        if zero_init:
            s2f[...] = jnp.zeros_like(s2f)
            s2b[...] = jnp.zeros_like(s2b)
        else:
            s2f[...] = s0f[0]
            s2b[...] = s0b[0]

    chains_f = _scan_direction(rf, kf, vf, lf, 0, s2f, w0, w2, a0, a2, kk, ka, True)
    chains_b = _scan_direction(rb, kb, vb, lb, 1, s2b, w0, w2, a0, a2, kk, ka, False)
    out = _scan_chunks(chains_f + chains_b)
    _scan_store(out[:len(chains_f)], yf, s2f)
    _scan_store(out[len(chains_f):], yb, s2b)

    @pl.when(c == pl.num_programs(2) - 1)
    def _():
        sf[0] = s2f[...]
        sb[0] = s2b[...]


def _rwkv_scan(zs, w0, w2r, a0, a2r, k_k, k_a, s0f, s0b, zero_init):
    b, t, _ = zs.shape
    d_a = w0.shape[1]
    pp = SCAN_PAIRS
    wide = pp * LANES
    groups = d_a // wide
    c = SCAN_CHUNK
    nc = t // c
    lora_blk = (4 * d_a) // (2 * LANES)
    fw = lambda off: (lambda i, q, j: (i, j, off + q))
    bw = lambda off: (lambda i, q, j: (i, nc - 1 - j, off + q))
    par = lambda i, q, j: (0, q)
    st = lambda i, q, j: (i, q, 0, 0)
    blk = (1, c, wide)
    lblk = (1, c, 2 * LANES)
    sblk = (1, pp, LANES, LANES)
    yshape = jax.ShapeDtypeStruct((b, t, d_a), F32)
    sshape = jax.ShapeDtypeStruct((b, d_a // LANES, LANES, LANES), F32)
    return pl.pallas_call(
        functools.partial(_scan_kernel, zero_init=zero_init),
        out_shape=(yshape, yshape, sshape, sshape),
        grid=(b, groups, nc),
        in_specs=[pl.BlockSpec(blk, fw(0)), pl.BlockSpec(blk, fw(groups)), pl.BlockSpec(blk, fw(2 * groups)),
                  pl.BlockSpec(lblk, lambda i, q, j: (i, j, lora_blk)),
                  pl.BlockSpec(blk, bw(0)), pl.BlockSpec(blk, bw(groups)), pl.BlockSpec(blk, bw(2 * groups)),
                  pl.BlockSpec(lblk, lambda i, q, j: (i, nc - 1 - j, lora_blk)),
                  pl.BlockSpec((2, wide), par), pl.BlockSpec((LANES, wide), par),
                  pl.BlockSpec((2, wide), par), pl.BlockSpec((LANES, wide), par),
                  pl.BlockSpec((1, wide), par), pl.BlockSpec((1, wide), par),
                  pl.BlockSpec(sblk, st), pl.BlockSpec(sblk, st)],
        out_specs=(pl.BlockSpec(blk, fw(0)), pl.BlockSpec(blk, bw(0)),
                   pl.BlockSpec(sblk, st), pl.BlockSpec(sblk, st)),
        scratch_shapes=[pltpu.VMEM((pp, LANES, LANES), F32), pltpu.VMEM((pp, LANES, LANES), F32)],
        compiler_params=_params(("parallel", "parallel", "arbitrary")),
        name="rwkv7_scan",
    )(zs, zs, zs, zs, zs, zs, zs, zs, w0, w2r, a0, a2r, k_k, k_a, s0f, s0b)


def _post_kernel(yf, yb, r, k, v, g, lora, u, vg, a0, a2, ka, rk, lxw, lxb, lvw, lvb, ws, bst, o_ref):
    tm = yf.shape[0]
    d_a = yf.shape[1]
    first_head = lax.broadcasted_iota(I32, (1, LANES), 1) < HEAD_A
    dir_row = lax.broadcasted_iota(I32, (LANES, 1), 0) // HEAD_A
    la = lora[:, LANES:]
    a_sum = jnp.zeros((tm, d_a), F32)
    for d in range(2):
        a_sum = a_sum + _sigmoid(a0[d:d + 1, :] + _dot_x3(la, jnp.where(dir_row == d, a2[...], 0.0)))
    rkk = r[...] * k[...] * (2.0 + (a_sum - 2.0) * ka[...]) * rk[...]
    y = yf[...] + yb[...]
    inv = 1.0 / HEAD_A
    for j in range(d_a // LANES):
        sl = slice(j * LANES, (j + 1) * LANES)
        yj = y[:, sl]
        mu = _head_sum(yj, first_head) * inv
        dl = yj - mu
        var = _head_sum(dl * dl, first_head) * inv
        yn = dl * lax.rsqrt(var + GN_EPS) * lxw[:, sl] + lxb[:, sl]
        bonus = _head_sum(rkk[:, sl], first_head) * v[:, sl]
        o_ref[:, sl] = ((yn + bonus) * _sigmoid(g[:, sl])).astype(BF16)

    uu = _gelu(u[...])
    vv = _gelu(vg[...])
    mu = jnp.mean(vv, axis=-1, keepdims=True)
    dv = vv - mu
    vn = dv * lax.rsqrt(jnp.mean(dv * dv, axis=-1, keepdims=True) + EPS) * lvw[...] + lvb[...]
    for ch in range(tm // GMLP_CHUNK):
        rows = slice(ch * GMLP_CHUNK, (ch + 1) * GMLP_CHUNK)
        for h in range(ws.shape[0]):
            cols = slice(h * LANES, (h + 1) * LANES)
            sp = _dot(ws[h], vn[rows, cols]) + bst[:, h:h + 1]
            o_ref[rows, d_a + h * LANES:d_a + (h + 1) * LANES] = (uu[rows, cols] * sp).astype(BF16)


def _post_mix(yf, yb, zs, z, a0, a2r, k_a, r_k, lxw, lxb, lvw, lvb, w_s, b_st):
    n, d_a = yf.shape
    tm = 256
    wide = lambda j: (lambda i: (i, j))
    full = lambda shape: pl.BlockSpec(shape, lambda i: (0,) * len(shape))
    lora_blk = (4 * d_a) // (2 * LANES)
    return pl.pallas_call(
        _post_kernel,
        out_shape=jax.ShapeDtypeStruct((n, 2 * d_a), BF16),
        grid=(n // tm,),
        in_specs=[pl.BlockSpec((tm, d_a), wide(0)), pl.BlockSpec((tm, d_a), wide(0)),
                  pl.BlockSpec((tm, d_a), wide(0)), pl.BlockSpec((tm, d_a), wide(1)),
                  pl.BlockSpec((tm, d_a), wide(2)), pl.BlockSpec((tm, d_a), wide(3)),
                  pl.BlockSpec((tm, 2 * LANES), wide(lora_blk)),
                  pl.BlockSpec((tm, d_a), wide(4)), pl.BlockSpec((tm, d_a), wide(5)),
                  full((2, d_a)), full((LANES, d_a)), full((1, d_a)), full((1, d_a)),
                  full((1, d_a)), full((1, d_a)), full((1, d_a)), full((1, d_a)),
                  full(w_s.shape), full(b_st.shape)],
        out_specs=pl.BlockSpec((tm, 2 * d_a), wide(0)),
        compiler_params=_params(("parallel",)),
        name="mix_post",
    )(yf, yb, zs, zs, zs, zs, zs, z, z, a0, a2r, k_a, r_k, lxw, lxb, lvw, lvb, w_s, b_st)


def _rms(x, g):
    return x * lax.rsqrt(jnp.mean(x * x, axis=-1, keepdims=True) + EPS) * g


def _out_proj_kernel(a_ref, w_ref, x_ref, mod_ref, gpost_ref, gpre_ref, x1_ref, h2_ref):
    o = jnp.dot(a_ref[...], w_ref[...], preferred_element_type=F32)
    x1 = x_ref[...] + mod_ref[0, 2:3, :] * _rms(o, gpost_ref[...])
    x1_ref[...] = x1
    h2_ref[...] = _rms(x1, gpre_ref[...]) * (1.0 + mod_ref[0, 4:5, :]) + mod_ref[0, 3:4, :]


def _out_proj(yag, w_bf, x, modm, g_post, g_pre2, group0, rows_per_group):
    n, d = x.shape
    tm = 256
    grp = lambda i: (group0 + (i * tm) // rows_per_group, 0, 0)
    row = lambda i: (i, 0)
    fix = lambda i: (0, 0)
    shp = jax.ShapeDtypeStruct((n, d), F32)
    return pl.pallas_call(
        _out_proj_kernel,
        out_shape=(shp, shp),
        grid=(n // tm,),
        in_specs=[pl.BlockSpec((tm, yag.shape[1]), row), pl.BlockSpec(w_bf.shape, fix),
                  pl.BlockSpec((tm, d), row), pl.BlockSpec((1, 6, d), grp),
                  pl.BlockSpec((1, d), fix), pl.BlockSpec((1, d), fix)],
        out_specs=(pl.BlockSpec((tm, d), row), pl.BlockSpec((tm, d), row)),
        compiler_params=_params(("parallel",)),
        name="out_proj",
    )(yag, w_bf, x, modm, g_post, g_pre2)


def _query_kernel(h_ref, wq_ref, sk_ref, o_ref):
    q = jnp.dot(h_ref[...].astype(BF16), wq_ref[...], preferred_element_type=F32)
    for g in range(sk_ref.shape[0]):
        o_ref[g * N_KEYS:(g + 1) * N_KEYS, :] = _dot_nt(sk_ref[g], q[:, g * LANES:(g + 1) * LANES])


def _query_scores(h2, wq_bf, sk):
    n, d = h2.shape
    tm = 256
    groups = sk.shape[0]
    return pl.pallas_call(
        _query_kernel,
        out_shape=jax.ShapeDtypeStruct((groups * N_KEYS, n), F32),
        grid=(n // tm,),
        in_specs=[pl.BlockSpec((tm, d), lambda i: (i, 0)),
                  pl.BlockSpec(wq_bf.shape, lambda i: (0, 0)),
                  pl.BlockSpec(sk.shape, lambda i: (0, 0, 0))],
        out_specs=pl.BlockSpec((groups * N_KEYS, tm), lambda i: (0, i)),
        compiler_params=_params(("parallel",)),
        name="peer_query",
    )(h2, wq_bf, sk)


def _top16(s):
    rows = s.shape[0]
    iota = lax.broadcasted_iota(I32, s.shape, 0)
    vals, idxs = [], []
    for _ in range(TOPK):
        m = jnp.max(s, axis=0, keepdims=True)
        i = jnp.min(jnp.where(s == m, iota, rows), axis=0, keepdims=True)
        vals.append(m)
        idxs.append(i)
        s = jnp.where(iota == i, -jnp.inf, s)
    return jnp.concatenate(vals, axis=0), jnp.concatenate(idxs, axis=0)


def _topk_kernel(s_ref, idx_ref, gate_ref):
    def head(h, carry):
        base = pl.multiple_of(h * (2 * N_KEYS), 2 * N_KEYS)
        v1, i1 = _top16(s_ref[pl.ds(base, N_KEYS), :])
        v2, i2 = _top16(s_ref[pl.ds(base + N_KEYS, N_KEYS), :])
        cand = jnp.concatenate([v1[i:i + 1, :] + v2 for i in range(TOPK)], axis=0)
        top_s, pos = _top16(cand)
        pi = pos // TOPK
        pj = pos % TOPK
        e1 = jnp.zeros_like(pos)
        e2 = jnp.zeros_like(pos)
        for i in range(TOPK):
            e1 = e1 + jnp.where(pi == i, i1[i:i + 1, :], 0)
            e2 = e2 + jnp.where(pj == i, i2[i:i + 1, :], 0)
        ex = jnp.exp(top_s - top_s[0:1, :])
        out = pl.multiple_of(h * TOPK, TOPK)
        idx_ref[pl.ds(out, TOPK), :] = e1 * N_KEYS + e2
        gate_ref[pl.ds(out, TOPK), :] = ex / jnp.sum(ex, axis=0, keepdims=True)
        return carry

    lax.fori_loop(0, PK_HEADS, head, 0)


def _peer_topk(scores):
    rows, n = scores.shape
    tt = 256
    out_rows = PK_HEADS * TOPK
    return pl.pallas_call(
        _topk_kernel,
        out_shape=(jax.ShapeDtypeStruct((out_rows, n), I32), jax.ShapeDtypeStruct((out_rows, n), F32)),
        grid=(n // tt,),
        in_specs=[pl.BlockSpec((rows, tt), lambda i: (0, i))],
        out_specs=(pl.BlockSpec((out_rows, tt), lambda i: (0, i)),
                   pl.BlockSpec((out_rows, tt), lambda i: (0, i))),
        compiler_params=_params(("parallel",)),
        name="peer_topk",
    )(scores)


SC_CORES = 2
SC_SUBCORES = 16
SC_LANES = 16
PEER_GROUP = 16
PEER_BLOCK = 8
U_MASK = -65536


def _pack_experts(eu, ev):
    hi = lax.bitcast_convert_type(eu.astype(BF16), jnp.uint16).astype(jnp.uint32)
    lo = lax.bitcast_convert_type(ev.astype(BF16), jnp.uint16).astype(jnp.uint32)
    return lax.bitcast_convert_type((hi << 16) | lo, I32)


def _unpack_u(w):
    return lax.bitcast_convert_type(w & U_MASK, F32)


def _unpack_v(w):
    return lax.bitcast_convert_type(w << 16, F32)


def _peer_experts_sc(idx, x, gates, table):
    n, d = x.shape
    n_sel = idx.shape[1]
    per_w = n // (SC_CORES * SC_SUBCORES)
    L, G, TB = SC_LANES, PEER_GROUP, PEER_BLOCK
    n_groups = n_sel // G
    n_ch = d // L
    mesh = plsc.VectorSubcoreMesh(core_axis_name="c", subcore_axis_name="s")

    @functools.partial(
        pl.kernel, mesh=mesh, out_type=jax.ShapeDtypeStruct((n, d), F32),
        compiler_params=pltpu.CompilerParams(needs_layout_passes=False),
        scratch_types=[pltpu.VMEM((TB, n_sel), I32), pltpu.VMEM((TB, n_sel), F32),
                       pltpu.VMEM((d,), F32), pltpu.VMEM((d,), F32),
                       pltpu.VMEM((2, G, d), I32), pltpu.VMEM((G, 2 * L), F32), pltpu.VMEM((2 * L,), F32),
                       pltpu.SemaphoreType.DMA((2,))],
        name="peer_experts_sc")
    def k(idx_hbm, x_hbm, g_hbm, tab_hbm, o_hbm, idx_v, gate_v, x_v, o_v, buf, acc_v, coef_v, sem):
        base = (lax.axis_index("s") * SC_CORES + lax.axis_index("c")) * per_w
        lane = lax.iota(I32, L)
        zero = jnp.zeros((L,), F32)

        def gather(t, g, slot):
            return pltpu.make_async_copy(tab_hbm.at[idx_v.at[t, pl.ds(g * G, G)]], buf.at[slot], sem.at[slot])

        @pl.loop(0, per_w // TB)
        def _(blk):
            tok0 = base + blk * TB
            pltpu.sync_copy(idx_hbm.at[pl.ds(tok0, TB)], idx_v)
            pltpu.sync_copy(g_hbm.at[pl.ds(tok0, TB)], gate_v)

            @pl.loop(0, TB)
            def _(t):
                tok = tok0 + t
                pltpu.sync_copy(x_hbm.at[tok], x_v)
                gather(t, 0, 0).start()

                @plsc.parallel_loop(0, n_ch)
                def _(ch):
                    o_v[pl.ds(ch * L, L)] = zero

                for g in range(n_groups):
                    slot = g % 2
                    if g + 1 < n_groups:
                        gather(t, g + 1, 1 - slot).start()
                    gather(t, g, slot).wait()

                    def u_body(ch, accs):
                        xv = x_v[pl.ds(ch * L, L)]
                        return tuple(accs[e] + _unpack_u(buf[slot, e, pl.ds(ch * L, L)]) * xv for e in range(G))

                    accs = plsc.parallel_loop(0, n_ch, carry=tuple(zero for _ in range(G)))(u_body)
                    for e in range(G):
                        acc_v[e, pl.ds(L, L)] = accs[e]
                    act = zero
                    for l in range(L):
                        act = act + plsc.load_gather(acc_v, [lane, jnp.full((L,), L + l, I32)])
                    y = 0.7978845608028654 * (act + 0.044715 * (act * act * act))
                    coef_v[pl.ds(L, L)] = act / (1.0 + jnp.exp(-2.0 * y)) * gate_v[t, pl.ds(g * G, G)]
                    cs = [plsc.load_gather(coef_v, [jnp.full((L,), L + e, I32)]) for e in range(G)]

                    @plsc.parallel_loop(0, n_ch)
                    def _(ch):
                        o = o_v[pl.ds(ch * L, L)]
                        for e in range(G):
                            o = o + cs[e] * _unpack_v(buf[slot, e, pl.ds(ch * L, L)])
                        o_v[pl.ds(ch * L, L)] = o

                pltpu.sync_copy(o_v, o_hbm.at[tok])

    return k(idx, x, gates, table)


PEER_TC_TOKENS = 64
PEER_TC_CTX = 1280
PEER_TC_LATENT = 3072


def _peer_tc_kernel(idx_ref, x_ref, g_ref, uv_ref, o_ref, buf, sem):
    rows, n_sel = buf.shape[1], buf.shape[2]
    tokens = g_ref.shape[0]

    def issue(t, slot):
        def body(e, carry):
            src = pl.multiple_of(idx_ref[t, e] * rows, rows)
            pltpu.make_async_copy(uv_ref.at[pl.ds(src, rows), :], buf.at[slot, :, e, :], sem.at[slot]).start()
            return carry
        lax.fori_loop(0, n_sel, body, 0, unroll=8)

    def wait(slot):
        pltpu.make_async_copy(buf.at[1 - slot], buf.at[slot], sem.at[slot]).wait()

    issue(0, 0)
    g_t = g_ref[...].T
    g_hi, g_mid = _split(g_t)
    g_lo = (g_t - g_hi.astype(F32) - g_mid.astype(F32)).astype(BF16)
    tok_iota = lax.broadcasted_iota(I32, (tokens, LANES), 0)
    dot = functools.partial(jnp.dot, preferred_element_type=F32)

    def token(t, carry):
        slot = t % 2

        @pl.when(t + 1 < tokens)
        def _():
            issue(t + 1, 1 - slot)

        wait(slot)
        pick = jnp.where(tok_iota == t, 1.0, 0.0).astype(BF16)
        gate = dot(g_hi, pick) + (dot(g_mid, pick) + dot(g_lo, pick))
        base = pl.multiple_of(t * rows, rows)
        xt = x_ref[pl.ds(base, rows), :]
        acc = _unpack_u(buf[slot, 0]) * xt[0:1, :]
        for c in range(1, rows):
            acc = acc + _unpack_u(buf[slot, c]) * xt[c:c + 1, :]
        act = jnp.sum(acc, axis=-1, keepdims=True)
        coef = _gelu(act) * gate
        o_ref[pl.ds(base, rows), :] = jnp.concatenate(
            [jnp.sum(coef * _unpack_v(buf[slot, c]), axis=0, keepdims=True) for c in range(rows)], axis=0)
        return carry

    lax.fori_loop(0, tokens, token, 0)


def _peer_experts_tc(idx, x, gates, table_rows):
    n, d = x.shape
    n_sel = idx.shape[1]
    tt = PEER_TC_TOKENS
    rows = d // LANES
    out = pl.pallas_call(
        _peer_tc_kernel,
        out_shape=jax.ShapeDtypeStruct((n * rows, LANES), F32),
        grid=(n // tt,),
        in_specs=[pl.BlockSpec((tt, n_sel), lambda i: (i, 0), memory_space=pltpu.SMEM),
                  pl.BlockSpec((tt * rows, LANES), lambda i: (i, 0)),
                  pl.BlockSpec((tt, n_sel), lambda i: (i, 0)),
                  pl.BlockSpec(memory_space=pl.ANY)],
        out_specs=pl.BlockSpec((tt * rows, LANES), lambda i: (i, 0)),
        scratch_shapes=[pltpu.VMEM((2, rows, n_sel, LANES), I32), pltpu.SemaphoreType.DMA((2,))],
        compiler_params=pltpu.CompilerParams(dimension_semantics=("arbitrary",),
                                             vmem_limit_bytes=VMEM_LIMIT, disable_bounds_checks=True),
        name="peer_experts_tc",
    )(idx, x.reshape(n * rows, LANES), gates, table_rows)
    return out.reshape(n, d)


def _peer_experts(idx, x, gates, table, table_rows, m):
    out_tc = _peer_experts_tc(idx[:m], x[:m], gates[:m], table_rows)
    out_sc = _peer_experts_sc(idx[m:], x[m:], gates[m:], table)
    return jnp.concatenate([out_tc, out_sc], axis=0)


def _final_kernel(x1_ref, p_ref, mod_ref, g_ref, o_ref):
    o_ref[...] = x1_ref[...] + mod_ref[0, 5:6, :] * _rms(p_ref[...], g_ref[...])


def _final(x1, peer, modm, g_post2, group0, rows_per_group):
    n, d = x1.shape
    tm = 512
    row = lambda i: (i, 0)
    return pl.pallas_call(
        _final_kernel,
        out_shape=jax.ShapeDtypeStruct((n, d), F32),
        grid=(n // tm,),
        in_specs=[pl.BlockSpec((tm, d), row), pl.BlockSpec((tm, d), row),
                  pl.BlockSpec((1, 6, d), lambda i: (group0 + (i * tm) // rows_per_group, 0, 0)),
                  pl.BlockSpec((1, d), lambda i: (0, 0))],
        out_specs=pl.BlockSpec((tm, d), row),
        compiler_params=_params(("parallel",)),
        name="final_residual",
    )(x1, peer, modm, g_post2)


def _block_diag_state(s):
    b, h = s.shape[:2]
    s = s.reshape(b, h // 2, 2, HEAD_A, HEAD_A)
    z = jnp.zeros_like(s[:, :, 0])
    return jnp.concatenate([jnp.concatenate([s[:, :, 0], z], axis=-1),
                            jnp.concatenate([z, s[:, :, 1]], axis=-1)], axis=-2)


def _head_states(s2):
    b, p = s2.shape[:2]
    return jnp.stack([s2[:, :, :HEAD_A, :HEAD_A], s2[:, :, HEAD_A:, HEAD_A:]],
                     axis=2).reshape(b, 2 * p, HEAD_A, HEAD_A)


def _layer(x3, modm, group0, grid_mode, s0f, s0b, w, tc_tokens):
    b, t, d = x3.shape
    n = b * t
    x = x3.reshape(n, d)
    rows_per_group = t if grid_mode else n
    z = _in_proj(x, modm, w["g_pre1"], w["w_in"], group0, rows_per_group)
    zs = _shift_mix(z, w["mu_shift"], w["n_shift"], w["lora_block"], grid_mode, t)
    zero_init = s0f is None
    if zero_init:
        s0f = s0b = jnp.zeros((b, w["w0"].shape[1] // LANES, LANES, LANES), F32)
    yf, yb, sf, sb = _rwkv_scan(zs.reshape(b, t, -1), w["w0"], w["w2"], w["a0"], w["a2"], w["k_k"], w["k_a"],
                                s0f, s0b, zero_init)
    d_a = yf.shape[-1]
    yag = _post_mix(yf.reshape(n, d_a), yb.reshape(n, d_a), zs, z, w["a0"], w["a2"], w["k_a"], w["r_k"],
                    w["ln_x_w"], w["ln_x_b"], w["ln_v_w"], w["ln_v_b"], w["w_s"], w["b_st"])
    x1, h2 = _out_proj(yag, w["w_out"], x, modm, w["g_post1"], w["g_pre2"], group0, rows_per_group)
    scores = _query_scores(h2, w["w_query"], w["sub_keys"])
    idx_t, gates_t = _peer_topk(scores)
    peer = _peer_experts(idx_t.T, h2, gates_t.T, w["experts"], w["expert_rows"], tc_tokens)
    out = _final(x1, peer, modm, w["g_post2"], group0, rows_per_group)
    return out.reshape(b, t, d), sf, sb


def kernel(x_prompt, x_sample, c, state_fwd, state_bwd, c_ctx, w_ada, b_ada, g_pre1, g_post1, g_pre2, g_post2,
           w_in, mu_shift, w0, w2, a0, a2, k_k, k_a, r_k, ln_x_w, ln_x_b, ln_v_w, ln_v_b, w_s, b_s, w_out,
           w_query, sub_keys, expert_u, expert_v):
    depth = w_in.shape[0]
    d = x_prompt.shape[-1]
    d_a = w0.shape[-1]
    n_shift = mu_shift.shape[-1]
    dec_b = x_sample.shape[0]
    cvec = jnp.concatenate([c_ctx[None], c, jnp.zeros((8 - 1 - dec_b, d), F32)], axis=0)
    xp, xs = x_prompt, x_sample
    new_f, new_b = [], []
    for l in range(depth):
        wl_in = w_in[l]
        row = lambda a: a[l].reshape(1, -1)
        w = {
            "w_in": jnp.concatenate([wl_in[:, :4 * d_a], wl_in[:, n_shift:], wl_in[:, 4 * d_a:n_shift]],
                                    axis=1).astype(BF16),
            "n_shift": n_shift,
            "lora_block": (wl_in.shape[1] - (n_shift - 4 * d_a)) // (2 * LANES),
            "mu_shift": row(mu_shift),
            "g_pre1": row(g_pre1), "g_post1": row(g_post1), "g_pre2": row(g_pre2), "g_post2": row(g_post2),
            "w0": w0[l], "w2": w2[l].reshape(-1, d_a), "a0": a0[l], "a2": a2[l].reshape(-1, d_a),
            "k_k": row(k_k), "k_a": row(k_a), "r_k": row(r_k),
            "ln_x_w": row(ln_x_w), "ln_x_b": row(ln_x_b), "ln_v_w": row(ln_v_w), "ln_v_b": row(ln_v_b),
            "w_s": w_s[l].astype(BF16), "b_st": b_s[l].T,
            "w_out": w_out[l].astype(BF16), "w_query": w_query[l].astype(BF16),
            "sub_keys": jnp.swapaxes(sub_keys[l], 0, 1).reshape(-1, N_KEYS, sub_keys.shape[-1]).astype(BF16),
            "experts": _pack_experts(expert_u[l], expert_v[l]),
        }
        w["expert_rows"] = w["experts"].reshape(-1, LANES)
        modm = _modulation(cvec, w_ada[l], b_ada[l].reshape(1, -1)).reshape(8, 6, d)
        half = xp.shape[0] // 2
        parts = [_layer(xp[:half], modm, 0, False, None, None, w, PEER_TC_CTX),
                 _layer(xp[half:], modm, 0, False, None, None, w, PEER_TC_CTX)]
        xp = jnp.concatenate([p[0] for p in parts], axis=0)
        new_f.append(_head_states(jnp.concatenate([p[1] for p in parts], axis=0)))
        new_b.append(_head_states(jnp.concatenate([p[2] for p in parts], axis=0)))
        xs, _, _ = _layer(xs, modm, 1, True, _block_diag_state(state_fwd[:, l]),
                          _block_diag_state(state_bwd[:, l]), w, PEER_TC_LATENT)
    return (xp, xs, jnp.stack(new_f, axis=1), jnp.stack(new_b, axis=1))
```

```python
import functools

import jax
import jax.numpy as jnp
from jax import lax
from jax.experimental import pallas as pl
from jax.experimental.pallas import tpu as pltpu
from jax.experimental.pallas import tpu_sc as plsc

F32 = jnp.float32
BF16 = jnp.bfloat16
I32 = jnp.int32

EPS = 1e-6
GN_EPS = 64e-5
HEAD_A = 64
LANES = 128
GRID_W = 64
GMLP_CHUNK = 128
PK_HEADS = 8
N_KEYS = 128
TOPK = 16
SCAN_CHUNK = 64
SCAN_PAIRS = 8
DECAY_SCALE = 0.6065306597126334
VMEM_LIMIT = 48 * 1024 * 1024


def _params(sem):
    return pltpu.CompilerParams(dimension_semantics=sem, vmem_limit_bytes=VMEM_LIMIT)


def _sigmoid(x):
    return 1.0 / (1.0 + jnp.exp(-x))


def _gelu(x):
    return 0.5 * x * (1.0 + jnp.tanh(0.7978845608028654 * (x + 0.044715 * (x * x * x))))


def _dot(a, b):
    return jnp.dot(a.astype(BF16), b.astype(BF16), preferred_element_type=F32)


def _dot_nt(a, b):
    return lax.dot_general(a.astype(BF16), b.astype(BF16), (((1,), (1,)), ((), ())),
                           preferred_element_type=F32)


def _dot_tn(a, b):
    return lax.dot_general(a.astype(BF16), b.astype(BF16), (((0,), (0,)), ((), ())),
                           preferred_element_type=F32)


def _split(x):
    hi = x.astype(BF16)
    return hi, (x - hi.astype(F32)).astype(BF16)


def _dot_x3(a, b):
    a_hi, a_lo = _split(a)
    b_hi, b_lo = _split(b)
    dot = functools.partial(jnp.dot, preferred_element_type=F32)
    return dot(a_hi, b_hi) + (dot(a_lo, b_hi) + dot(a_hi, b_lo))


def _dot_split_rhs(a_bf, b):
    b_hi, b_mid = _split(b)
    b_lo = (b - b_hi.astype(F32) - b_mid.astype(F32)).astype(BF16)
    dot = functools.partial(jnp.dot, preferred_element_type=F32)
    return dot(a_bf, b_hi) + (dot(a_bf, b_mid) + dot(a_bf, b_lo))


def _head_sum(x, first_head):
    s_a = jnp.sum(jnp.where(first_head, x, 0.0), axis=-1, keepdims=True)
    s_b = jnp.sum(jnp.where(first_head, 0.0, x), axis=-1, keepdims=True)
    return jnp.where(first_head, s_a, s_b)


def _mod_kernel(c_ref, w_ref, b_ref, o_ref):
    c = c_ref[...]
    o_ref[...] = _dot(c * _sigmoid(c), w_ref[...]) + b_ref[...]


def _modulation(cvec, w_ada, b_ada):
    rows, d = cvec.shape
    n = w_ada.shape[1]
    tn = 1024
    return pl.pallas_call(
        _mod_kernel,
        out_shape=jax.ShapeDtypeStruct((rows, n), F32),
        grid=(n // tn,),
        in_specs=[pl.BlockSpec((rows, d), lambda j: (0, 0)),
                  pl.BlockSpec((d, tn), lambda j: (0, j)),
                  pl.BlockSpec((1, tn), lambda j: (0, j))],
        out_specs=pl.BlockSpec((rows, tn), lambda j: (0, j)),
        compiler_params=_params(("parallel",)),
        name="adaln_mod",
    )(cvec, w_ada, b_ada)


def _in_proj_kernel(x_ref, mod_ref, g_ref, w_ref, o_ref, h_ref):
    @pl.when(pl.program_id(1) == 0)
    def _():
        x = x_ref[...]
        y = x * lax.rsqrt(jnp.mean(x * x, axis=-1, keepdims=True) + EPS) * g_ref[...]
        h_ref[...] = (y * (1.0 + mod_ref[0, 1:2, :]) + mod_ref[0, 0:1, :]).astype(BF16)

    o_ref[...] = jnp.dot(h_ref[...], w_ref[...], preferred_element_type=F32)


def _in_proj(x, modm, g_pre, w_bf, group0, rows_per_group):
    n, d = x.shape
    p = w_bf.shape[1]
    tm, tn = 512, 1280
    grp = lambda i, j: (group0 + (i * tm) // rows_per_group, 0, 0)
    return pl.pallas_call(
        _in_proj_kernel,
        out_shape=jax.ShapeDtypeStruct((n, p), F32),
        grid=(n // tm, p // tn),
        in_specs=[pl.BlockSpec((tm, d), lambda i, j: (i, 0)),
                  pl.BlockSpec((1, 6, d), grp),
                  pl.BlockSpec((1, d), lambda i, j: (0, 0)),
                  pl.BlockSpec((d, tn), lambda i, j: (0, j))],
        out_specs=pl.BlockSpec((tm, tn), lambda i, j: (i, j)),
        scratch_shapes=[pltpu.VMEM((tm, d), BF16)],
        compiler_params=_params(("parallel", "arbitrary")),
        name="in_proj",
    )(x, modm, g_pre, w_bf)


def _shift_kernel(z_ref, mu_ref, o_ref, *, grid_mode, period):
    z = z_ref[...]
    rows = z.shape[0]
    t = lax.broadcasted_iota(I32, (rows, 1), 0) % period
    prev = jnp.where(t % (GRID_W if grid_mode else period) != 0, pltpu.roll(z, 1, 0), 0.0)
    nxt = jnp.where(t % (GRID_W if grid_mode else period) != (GRID_W if grid_mode else period) - 1,
                    pltpu.roll(z, rows - 1, 0), 0.0)
    if grid_mode:
        up = jnp.where(t >= GRID_W, pltpu.roll(z, GRID_W, 0), 0.0)
        down = jnp.where(t < period - GRID_W, pltpu.roll(z, rows - GRID_W, 0), 0.0)
        nb = 0.25 * (up + down + prev + nxt)
    else:
        nb = 0.5 * (prev + nxt)
    o_ref[...] = z + mu_ref[...] * (nb - z)


def _shift_mix(z, mu, n_shift, lora_block, grid_mode, period):
    n = z.shape[0]
    tr, tc = 2048, 256
    main_blocks = (n_shift // tc) - 1
    col = lambda i, j: (i, jnp.where(j < main_blocks, j, lora_block))
    return pl.pallas_call(
        functools.partial(_shift_kernel, grid_mode=grid_mode, period=period),
        out_shape=jax.ShapeDtypeStruct((n, n_shift), F32),
        grid=(n // tr, n_shift // tc),
        in_specs=[pl.BlockSpec((tr, tc), col),
                  pl.BlockSpec((1, tc), lambda i, j: (0, j))],
        out_specs=pl.BlockSpec((tr, tc), lambda i, j: (i, j)),
        compiler_params=_params(("parallel", "parallel")),
        name="token_shift",
    )(z, mu)


def _scan_chunks(chains):
    c = chains[0][0].shape[0]
    c2 = 2 * c
    n = len(chains)
    fwd = [ch[9] for ch in chains]
    first_head = lax.broadcasted_iota(I32, (1, LANES), 1) < HEAD_A
    row = lax.broadcasted_iota(I32, (c2, c2), 0)
    col = lax.broadcasted_iota(I32, (c2, c2), 1)
    eye = jnp.where(row == col, 1.0, 0.0)

    def stack(x):
        return jnp.concatenate([jnp.where(first_head, x, 0.0), jnp.where(first_head, 0.0, x)],
                               axis=0).astype(BF16)

    lhs, rhs, v2, total = [], [], [], []
    for r, k, v, logw, cl, a, k_k, k_a, _, forward in chains:
        kkr = k * k_k
        kk = kkr / jnp.maximum(jnp.sqrt(_head_sum(kkr * kkr, first_head)), 1e-12)
        kd = k * (1.0 + (a - 1.0) * k_a)
        gi = jnp.exp(-cl)
        lhs.append(jnp.concatenate([stack(kk * jnp.exp(cl - logw)), stack(r * jnp.exp(cl))], axis=0))
        rhs.append(jnp.concatenate([stack(kk * a * gi), stack(kd * gi)], axis=0))
        v2.append(stack(v))
        total.append(cl[c - 1:c, :] if forward else cl[0:1, :])

    res = [_dot_nt(lhs[i], rhs[i]) for i in range(n)]
    pr = [_dot_nt(lhs[i], chains[i][8]) for i in range(n)]
    strict = [(row > col) if f else (row < col) for f in fwd]
    incl = [(row >= col) if f else (row <= col) for f in fwd]
    ab = [jnp.where(strict[i], res[i][:c2, :c2], 0.0) for i in range(n)]
    ak = [jnp.where(strict[i], res[i][:c2, c2:], 0.0) for i in range(n)]
    gb_gk = [jnp.concatenate([jnp.where(incl[i], res[i][c2:, :c2], 0.0),
                              jnp.where(incl[i], res[i][c2:, c2:], 0.0)], axis=1).astype(BF16) for i in range(n)]
    akv = [_dot(ak[i], v2[i]) for i in range(n)]

    tinv = [eye - ab[i] for i in range(n)]
    pw = [-ab[i] for i in range(n)]
    for _ in range(c.bit_length() - 2):
        pw = [_dot(pw[i], pw[i]) for i in range(n)]
        tinv = [tinv[i] + _dot(tinv[i], pw[i]) for i in range(n)]

    u2 = [_dot(tinv[i], -pr[i][:c2, :] - akv[i]) for i in range(n)]
    uv = [jnp.concatenate([u2[i].astype(BF16), v2[i]], axis=0) for i in range(n)]
    y2 = [pr[i][c2:, :] + _dot(gb_gk[i], uv[i]) for i in range(n)]
    s_new = [(chains[i][8] + _dot_tn(uv[i], rhs[i])) * jnp.exp(total[i]) for i in range(n)]
    return [(y2[i][:c, :] + y2[i][c:, :], s_new[i]) for i in range(n)]


def _scan_direction(r_ref, k_ref, v_ref, l_ref, d, s_ref, w0, w2, a0, a2, kk, ka, forward):
    lora = l_ref[0]
    c = lora.shape[0]
    dir_rows = (lax.broadcasted_iota(I32, (LANES, 1), 0) // HEAD_A) == d
    wl = w0[d:d + 1, :] + _dot_x3(jnp.tanh(lora[:, :LANES]), jnp.where(dir_rows, w2[...], 0.0))
    logw = -DECAY_SCALE * _sigmoid(wl)
    a = _sigmoid(a0[d:d + 1, :] + _dot_x3(lora[:, LANES:], jnp.where(dir_rows, a2[...], 0.0)))
    ti = lax.broadcasted_iota(I32, (c, c), 0)
    tj = lax.broadcasted_iota(I32, (c, c), 1)
    cum = jnp.where((tj <= ti) if forward else (tj >= ti), 1.0, 0.0).astype(BF16)
    cl = _dot_split_rhs(cum, logw)
    chains = []
    for p in range(s_ref.shape[0]):
        sl = slice(p * LANES, (p + 1) * LANES)
        chains.append((r_ref[0, :, sl], k_ref[0, :, sl], v_ref[0, :, sl], logw[:, sl], cl[:, sl],
                       a[:, sl], kk[:, sl], ka[:, sl], s_ref[p], forward))
    return chains


def _scan_store(out, y_ref, s_ref):
    for p, (y, s_new) in enumerate(out):
        y_ref[0, :, p * LANES:(p + 1) * LANES] = y
        s_ref[p] = s_new


def _scan_kernel(rf, kf, vf, lf, rb, kb, vb, lb, w0, w2, a0, a2, kk, ka, s0f, s0b,
                 yf, yb, sf, sb, s2f, s2b, *, zero_init):
    c = pl.program_id(2)

    @pl.when(c == 0)
    def _():
        if zero_init:
            s2f[...] = jnp.zeros_like(s2f)
            s2b[...] = jnp.zeros_like(s2b)
        else:
            s2f[...] = s0f[0]
            s2b[...] = s0b[0]

    chains_f = _scan_direction(rf, kf, vf, lf, 0, s2f, w0, w2, a0, a2, kk, ka, True)
    chains_b = _scan_direction(rb, kb, vb, lb, 1, s2b, w0, w2, a0, a2, kk, ka, False)
    out = _scan_chunks(chains_f + chains_b)
    _scan_store(out[:len(chains_f)], yf, s2f)
    _scan_store(out[len(chains_f):], yb, s2b)

    @pl.when(c == pl.num_programs(2) - 1)
    def _():
        sf[0] = s2f[...]
        sb[0] = s2b[...]


def _rwkv_scan(zs, w0, w2r, a0, a2r, k_k, k_a, s0f, s0b, zero_init):
    b, t, _ = zs.shape
    d_a = w0.shape[1]
    pp = SCAN_PAIRS
    wide = pp * LANES
    groups = d_a // wide
    c = SCAN_CHUNK
    nc = t // c
    lora_blk = (4 * d_a) // (2 * LANES)
    fw = lambda off: (lambda i, q, j: (i, j, off + q))
    bw = lambda off: (lambda i, q, j: (i, nc - 1 - j, off + q))
    par = lambda i, q, j: (0, q)
    st = lambda i, q, j: (i, q, 0, 0)
    blk = (1, c, wide)
    lblk = (1, c, 2 * LANES)
    sblk = (1, pp, LANES, LANES)
    yshape = jax.ShapeDtypeStruct((b, t, d_a), F32)
    sshape = jax.ShapeDtypeStruct((b, d_a // LANES, LANES, LANES), F32)
    return pl.pallas_call(
        functools.partial(_scan_kernel, zero_init=zero_init),
        out_shape=(yshape, yshape, sshape, sshape),
        grid=(b, groups, nc),
        in_specs=[pl.BlockSpec(blk, fw(0)), pl.BlockSpec(blk, fw(groups)), pl.BlockSpec(blk, fw(2 * groups)),
                  pl.BlockSpec(lblk, lambda i, q, j: (i, j, lora_blk)),
                  pl.BlockSpec(blk, bw(0)), pl.BlockSpec(blk, bw(groups)), pl.BlockSpec(blk, bw(2 * groups)),
                  pl.BlockSpec(lblk, lambda i, q, j: (i, nc - 1 - j, lora_blk)),
                  pl.BlockSpec((2, wide), par), pl.BlockSpec((LANES, wide), par),
                  pl.BlockSpec((2, wide), par), pl.BlockSpec((LANES, wide), par),
                  pl.BlockSpec((1, wide), par), pl.BlockSpec((1, wide), par),
                  pl.BlockSpec(sblk, st), pl.BlockSpec(sblk, st)],
        out_specs=(pl.BlockSpec(blk, fw(0)), pl.BlockSpec(blk, bw(0)),
                   pl.BlockSpec(sblk, st), pl.BlockSpec(sblk, st)),
        scratch_shapes=[pltpu.VMEM((pp, LANES, LANES), F32), pltpu.VMEM((pp, LANES, LANES), F32)],
        compiler_params=_params(("parallel", "parallel", "arbitrary")),
        name="rwkv7_scan",
    )(zs, zs, zs, zs, zs, zs, zs, zs, w0, w2r, a0, a2r, k_k, k_a, s0f, s0b)


def _post_kernel(yf, yb, r, k, v, g, lora, u, vg, a0, a2, ka, rk, lxw, lxb, lvw, lvb, ws, bst, o_ref):
    tm = yf.shape[0]
    d_a = yf.shape[1]
    first_head = lax.broadcasted_iota(I32, (1, LANES), 1) < HEAD_A
    dir_row = lax.broadcasted_iota(I32, (LANES, 1), 0) // HEAD_A
    la = lora[:, LANES:]
    a_sum = jnp.zeros((tm, d_a), F32)
    for d in range(2):
        a_sum = a_sum + _sigmoid(a0[d:d + 1, :] + _dot_x3(la, jnp.where(dir_row == d, a2[...], 0.0)))
    rkk = r[...] * k[...] * (2.0 + (a_sum - 2.0) * ka[...]) * rk[...]
    y = yf[...] + yb[...]
    inv = 1.0 / HEAD_A
    for j in range(d_a // LANES):
        sl = slice(j * LANES, (j + 1) * LANES)
        yj = y[:, sl]
        mu = _head_sum(yj, first_head) * inv
        dl = yj - mu
        var = _head_sum(dl * dl, first_head) * inv
        yn = dl * lax.rsqrt(var + GN_EPS) * lxw[:, sl] + lxb[:, sl]
        bonus = _head_sum(rkk[:, sl], first_head) * v[:, sl]
        o_ref[:, sl] = ((yn + bonus) * _sigmoid(g[:, sl])).astype(BF16)

    uu = _gelu(u[...])
    vv = _gelu(vg[...])
    mu = jnp.mean(vv, axis=-1, keepdims=True)
    dv = vv - mu
    vn = dv * lax.rsqrt(jnp.mean(dv * dv, axis=-1, keepdims=True) + EPS) * lvw[...] + lvb[...]
    for ch in range(tm // GMLP_CHUNK):
        rows = slice(ch * GMLP_CHUNK, (ch + 1) * GMLP_CHUNK)
        for h in range(ws.shape[0]):
            cols = slice(h * LANES, (h + 1) * LANES)
            sp = _dot(ws[h], vn[rows, cols]) + bst[:, h:h + 1]
            o_ref[rows, d_a + h * LANES:d_a + (h + 1) * LANES] = (uu[rows, cols] * sp).astype(BF16)


def _post_mix(yf, yb, zs, z, a0, a2r, k_a, r_k, lxw, lxb, lvw, lvb, w_s, b_st):
    n, d_a = yf.shape
    tm = 256
    wide = lambda j: (lambda i: (i, j))
    full = lambda shape: pl.BlockSpec(shape, lambda i: (0,) * len(shape))
    lora_blk = (4 * d_a) // (2 * LANES)
    return pl.pallas_call(
        _post_kernel,
        out_shape=jax.ShapeDtypeStruct((n, 2 * d_a), BF16),
        grid=(n // tm,),
        in_specs=[pl.BlockSpec((tm, d_a), wide(0)), pl.BlockSpec((tm, d_a), wide(0)),
                  pl.BlockSpec((tm, d_a), wide(0)), pl.BlockSpec((tm, d_a), wide(1)),
                  pl.BlockSpec((tm, d_a), wide(2)), pl.BlockSpec((tm, d_a), wide(3)),
                  pl.BlockSpec((tm, 2 * LANES), wide(lora_blk)),
                  pl.BlockSpec((tm, d_a), wide(4)), pl.BlockSpec((tm, d_a), wide(5)),
                  full((2, d_a)), full((LANES, d_a)), full((1, d_a)), full((1, d_a)),
                  full((1, d_a)), full((1, d_a)), full((1, d_a)), full((1, d_a)),
                  full(w_s.shape), full(b_st.shape)],
        out_specs=pl.BlockSpec((tm, 2 * d_a), wide(0)),
        compiler_params=_params(("parallel",)),
        name="mix_post",
    )(yf, yb, zs, zs, zs, zs, zs, z, z, a0, a2r, k_a, r_k, lxw, lxb, lvw, lvb, w_s, b_st)


def _rms(x, g):
    return x * lax.rsqrt(jnp.mean(x * x, axis=-1, keepdims=True) + EPS) * g


def _out_proj_kernel(a_ref, w_ref, x_ref, mod_ref, gpost_ref, gpre_ref, x1_ref, h2_ref):
    o = jnp.dot(a_ref[...], w_ref[...], preferred_element_type=F32)
    x1 = x_ref[...] + mod_ref[0, 2:3, :] * _rms(o, gpost_ref[...])
    x1_ref[...] = x1
    h2_ref[...] = _rms(x1, gpre_ref[...]) * (1.0 + mod_ref[0, 4:5, :]) + mod_ref[0, 3:4, :]


def _out_proj(yag, w_bf, x, modm, g_post, g_pre2, group0, rows_per_group):
    n, d = x.shape
    tm = 256
    grp = lambda i: (group0 + (i * tm) // rows_per_group, 0, 0)
    row = lambda i: (i, 0)
    fix = lambda i: (0, 0)
    shp = jax.ShapeDtypeStruct((n, d), F32)
    return pl.pallas_call(
        _out_proj_kernel,
        out_shape=(shp, shp),
        grid=(n // tm,),
        in_specs=[pl.BlockSpec((tm, yag.shape[1]), row), pl.BlockSpec(w_bf.shape, fix),
                  pl.BlockSpec((tm, d), row), pl.BlockSpec((1, 6, d), grp),
                  pl.BlockSpec((1, d), fix), pl.BlockSpec((1, d), fix)],
        out_specs=(pl.BlockSpec((tm, d), row), pl.BlockSpec((tm, d), row)),
        compiler_params=_params(("parallel",)),
        name="out_proj",
    )(yag, w_bf, x, modm, g_post, g_pre2)


def _query_kernel(h_ref, wq_ref, sk_ref, o_ref):
    q = jnp.dot(h_ref[...].astype(BF16), wq_ref[...], preferred_element_type=F32)
    for g in range(sk_ref.shape[0]):
        o_ref[g * N_KEYS:(g + 1) * N_KEYS, :] = _dot_nt(sk_ref[g], q[:, g * LANES:(g + 1) * LANES])


def _query_scores(h2, wq_bf, sk):
    n, d = h2.shape
    tm = 256
    groups = sk.shape[0]
    return pl.pallas_call(
        _query_kernel,
        out_shape=jax.ShapeDtypeStruct((groups * N_KEYS, n), F32),
        grid=(n // tm,),
        in_specs=[pl.BlockSpec((tm, d), lambda i: (i, 0)),
                  pl.BlockSpec(wq_bf.shape, lambda i: (0, 0)),
                  pl.BlockSpec(sk.shape, lambda i: (0, 0, 0))],
        out_specs=pl.BlockSpec((groups * N_KEYS, tm), lambda i: (0, i)),
        compiler_params=_params(("parallel",)),
        name="peer_query",
    )(h2, wq_bf, sk)


def _top16(s):
    rows = s.shape[0]
    iota = lax.broadcasted_iota(I32, s.shape, 0)
    vals, idxs = [], []
    for _ in range(TOPK):
        m = jnp.max(s, axis=0, keepdims=True)
        i = jnp.min(jnp.where(s == m, iota, rows), axis=0, keepdims=True)
        vals.append(m)
        idxs.append(i)
        s = jnp.where(iota == i, -jnp.inf, s)
    return jnp.concatenate(vals, axis=0), jnp.concatenate(idxs, axis=0)


def _topk_kernel(s_ref, idx_ref, gate_ref):
    def head(h, carry):
        base = pl.multiple_of(h * (2 * N_KEYS), 2 * N_KEYS)
        v1, i1 = _top16(s_ref[pl.ds(base, N_KEYS), :])
        v2, i2 = _top16(s_ref[pl.ds(base + N_KEYS, N_KEYS), :])
        cand = jnp.concatenate([v1[i:i + 1, :] + v2 for i in range(TOPK)], axis=0)
        top_s, pos = _top16(cand)
        pi = pos // TOPK
        pj = pos % TOPK
        e1 = jnp.zeros_like(pos)
        e2 = jnp.zeros_like(pos)
        for i in range(TOPK):
            e1 = e1 + jnp.where(pi == i, i1[i:i + 1, :], 0)
            e2 = e2 + jnp.where(pj == i, i2[i:i + 1, :], 0)
        ex = jnp.exp(top_s - top_s[0:1, :])
        out = pl.multiple_of(h * TOPK, TOPK)
        idx_ref[pl.ds(out, TOPK), :] = e1 * N_KEYS + e2
        gate_ref[pl.ds(out, TOPK), :] = ex / jnp.sum(ex, axis=0, keepdims=True)
        return carry

    lax.fori_loop(0, PK_HEADS, head, 0)


def _peer_topk(scores):
    rows, n = scores.shape
    tt = 256
    out_rows = PK_HEADS * TOPK
    return pl.pallas_call(
        _topk_kernel,
        out_shape=(jax.ShapeDtypeStruct((out_rows, n), I32), jax.ShapeDtypeStruct((out_rows, n), F32)),
        grid=(n // tt,),
        in_specs=[pl.BlockSpec((rows, tt), lambda i: (0, i))],
        out_specs=(pl.BlockSpec((out_rows, tt), lambda i: (0, i)),
                   pl.BlockSpec((out_rows, tt), lambda i: (0, i))),
        compiler_params=_params(("parallel",)),
        name="peer_topk",
    )(scores)


SC_CORES = 2
SC_SUBCORES = 16
SC_LANES = 16
PEER_GROUP = 16
PEER_BLOCK = 8
U_MASK = -65536


def _pack_experts(eu, ev):
    hi = lax.bitcast_convert_type(eu.astype(BF16), jnp.uint16).astype(jnp.uint32)
    lo = lax.bitcast_convert_type(ev.astype(BF16), jnp.uint16).astype(jnp.uint32)
    return lax.bitcast_convert_type((hi << 16) | lo, I32)


def _unpack_u(w):
    return lax.bitcast_convert_type(w & U_MASK, F32)


def _unpack_v(w):
    return lax.bitcast_convert_type(w << 16, F32)


def _peer_experts_sc(idx, x, gates, table):
    n, d = x.shape
    n_sel = idx.shape[1]
    per_w = n // (SC_CORES * SC_SUBCORES)
    L, G, TB = SC_LANES, PEER_GROUP, PEER_BLOCK
    n_groups = n_sel // G
    n_ch = d // L
    mesh = plsc.VectorSubcoreMesh(core_axis_name="c", subcore_axis_name="s")

    @functools.partial(
        pl.kernel, mesh=mesh, out_type=jax.ShapeDtypeStruct((n, d), F32),
        compiler_params=pltpu.CompilerParams(needs_layout_passes=False),
        scratch_types=[pltpu.VMEM((TB, n_sel), I32), pltpu.VMEM((TB, n_sel), F32),
                       pltpu.VMEM((2, d), F32), pltpu.VMEM((2, d), F32),
                       pltpu.VMEM((2, G, d), I32), pltpu.VMEM((G, 2 * L), F32), pltpu.VMEM((2 * L,), F32),
                       pltpu.SemaphoreType.DMA((2,)), pltpu.SemaphoreType.DMA((2,)), pltpu.SemaphoreType.DMA((2,))],
        name="peer_experts_sc")
    def k(idx_hbm, x_hbm, g_hbm, tab_hbm, o_hbm, idx_v, gate_v, x_v, o_v, buf, acc_v, coef_v, sem, xsem, osem):
        base = (lax.axis_index("s") * SC_CORES + lax.axis_index("c")) * per_w
        lane = lax.iota(I32, L)
        zero = jnp.zeros((L,), F32)

        def gather(t, g, slot):
            return pltpu.make_async_copy(tab_hbm.at[idx_v.at[t, pl.ds(g * G, G)]], buf.at[slot], sem.at[slot])

        def x_copy(j, par):
            return pltpu.make_async_copy(x_hbm.at[base + j], x_v.at[par], xsem.at[par])

        def o_copy(j, par):
            return pltpu.make_async_copy(o_v.at[par], o_hbm.at[base + j], osem.at[par])

        x_copy(0, 0).start()

        @pl.loop(0, per_w // TB)
        def _(blk):
            tok0 = base + blk * TB
            pltpu.sync_copy(idx_hbm.at[pl.ds(tok0, TB)], idx_v)
            pltpu.sync_copy(g_hbm.at[pl.ds(tok0, TB)], gate_v)

            @pl.loop(0, TB)
            def _(t):
                j = blk * TB + t
                par = j % 2
                gather(t, 0, 0).start()

                @pl.when(j + 1 < per_w)
                def _():
                    x_copy(j + 1, 1 - par).start()

                x_copy(j, par).wait()

                @pl.when(j >= 2)
                def _():
                    o_copy(j - 2, par).wait()

                for g in range(n_groups):
                    slot = g % 2
                    if g + 1 < n_groups:
                        gather(t, g + 1, 1 - slot).start()
                    gather(t, g, slot).wait()

                    def u_body(ch, accs):
                        xv = x_v[par, pl.ds(ch * L, L)]
                        return tuple(accs[e] + _unpack_u(buf[slot, e, pl.ds(ch * L, L)]) * xv for e in range(G))

                    accs = plsc.parallel_loop(0, n_ch, carry=tuple(zero for _ in range(G)))(u_body)
                    for e in range(G):
                        acc_v[e, pl.ds(L, L)] = accs[e]
                    act = zero
                    for l in range(L):
                        act = act + plsc.load_gather(acc_v, [lane, jnp.full((L,), L + l, I32)])
                    y = 0.7978845608028654 * (act + 0.044715 * (act * act * act))
                    coef_v[pl.ds(L, L)] = act / (1.0 + jnp.exp(-2.0 * y)) * gate_v[t, pl.ds(g * G, G)]
                    cs = [plsc.load_gather(coef_v, [jnp.full((L,), L + e, I32)]) for e in range(G)]

                    @plsc.parallel_loop(0, n_ch)
                    def _(ch):
                        o = zero if g == 0 else o_v[par, pl.ds(ch * L, L)]
                        for e in range(G):
                            o = o + cs[e] * _unpack_v(buf[slot, e, pl.ds(ch * L, L)])
                        o_v[par, pl.ds(ch * L, L)] = o

                o_copy(j, par).start()

        o_copy(per_w - 2, 0).wait()
        o_copy(per_w - 1, 1).wait()

    return k(idx, x, gates, table)


PEER_TC_TOKENS = 64
CTX_PARTS = 4
PEER_TC_CTX = 512
PEER_TC_LATENT = 2560


def _peer_tc_kernel(idx_ref, x_ref, g_ref, uv_ref, o_ref, buf, sem):
    rows, n_sel = buf.shape[1], buf.shape[2]
    tokens = g_ref.shape[0]

    def issue(t, slot):
        def body(e, carry):
            src = pl.multiple_of(idx_ref[t, e] * rows, rows)
            pltpu.make_async_copy(uv_ref.at[pl.ds(src, rows), :], buf.at[slot, :, e, :], sem.at[slot]).start()
            return carry
        lax.fori_loop(0, n_sel, body, 0, unroll=8)

    def wait(slot):
        pltpu.make_async_copy(buf.at[1 - slot], buf.at[slot], sem.at[slot]).wait()

    issue(0, 0)
    g_t = g_ref[...].T
    g_hi, g_mid = _split(g_t)
    g_lo = (g_t - g_hi.astype(F32) - g_mid.astype(F32)).astype(BF16)
    tok_iota = lax.broadcasted_iota(I32, (tokens, LANES), 0)
    dot = functools.partial(jnp.dot, preferred_element_type=F32)

    def token(t, carry):
        slot = t % 2

        @pl.when(t + 1 < tokens)
        def _():
            issue(t + 1, 1 - slot)

        wait(slot)
        pick = jnp.where(tok_iota == t, 1.0, 0.0).astype(BF16)
        gate = dot(g_hi, pick) + (dot(g_mid, pick) + dot(g_lo, pick))
        base = pl.multiple_of(t * rows, rows)
        xt = x_ref[pl.ds(base, rows), :]
        acc = _unpack_u(buf[slot, 0]) * xt[0:1, :]
        for c in range(1, rows):
            acc = acc + _unpack_u(buf[slot, c]) * xt[c:c + 1, :]
        act = jnp.sum(acc, axis=-1, keepdims=True)
        coef = _gelu(act) * gate
        o_ref[pl.ds(base, rows), :] = jnp.concatenate(
            [jnp.sum(coef * _unpack_v(buf[slot, c]), axis=0, keepdims=True) for c in range(rows)], axis=0)
        return carry

    lax.fori_loop(0, tokens, token, 0)


def _peer_experts_tc(idx, x, gates, table_rows):
    n, d = x.shape
    n_sel = idx.shape[1]
    tt = PEER_TC_TOKENS
    rows = d // LANES
    out = pl.pallas_call(
        _peer_tc_kernel,
        out_shape=jax.ShapeDtypeStruct((n * rows, LANES), F32),
        grid=(n // tt,),
        in_specs=[pl.BlockSpec((tt, n_sel), lambda i: (i, 0), memory_space=pltpu.SMEM),
                  pl.BlockSpec((tt * rows, LANES), lambda i: (i, 0)),
                  pl.BlockSpec((tt, n_sel), lambda i: (i, 0)),
                  pl.BlockSpec(memory_space=pl.ANY)],
        out_specs=pl.BlockSpec((tt * rows, LANES), lambda i: (i, 0)),
        scratch_shapes=[pltpu.VMEM((2, rows, n_sel, LANES), I32), pltpu.SemaphoreType.DMA((2,))],
        compiler_params=pltpu.CompilerParams(dimension_semantics=("arbitrary",),
                                             vmem_limit_bytes=VMEM_LIMIT, disable_bounds_checks=True),
        name="peer_experts_tc",
    )(idx, x.reshape(n * rows, LANES), gates, table_rows)
    return out.reshape(n, d)


def _peer_experts(idx, x, gates, table, table_rows, m):
    out_tc = _peer_experts_tc(idx[:m], x[:m], gates[:m], table_rows)
    out_sc = _peer_experts_sc(idx[m:], x[m:], gates[m:], table)
    return jnp.concatenate([out_tc, out_sc], axis=0)


def _final_kernel(x1_ref, p_ref, mod_ref, g_ref, o_ref):
    o_ref[...] = x1_ref[...] + mod_ref[0, 5:6, :] * _rms(p_ref[...], g_ref[...])


def _final(x1, peer, modm, g_post2, group0, rows_per_group):
    n, d = x1.shape
    tm = 512
    row = lambda i: (i, 0)
    return pl.pallas_call(
        _final_kernel,
        out_shape=jax.ShapeDtypeStruct((n, d), F32),
        grid=(n // tm,),
        in_specs=[pl.BlockSpec((tm, d), row), pl.BlockSpec((tm, d), row),
                  pl.BlockSpec((1, 6, d), lambda i: (group0 + (i * tm) // rows_per_group, 0, 0)),
                  pl.BlockSpec((1, d), lambda i: (0, 0))],
        out_specs=pl.BlockSpec((tm, d), row),
        compiler_params=_params(("parallel",)),
        name="final_residual",
    )(x1, peer, modm, g_post2)


def _block_diag_state(s):
    b, h = s.shape[:2]
    s = s.reshape(b, h // 2, 2, HEAD_A, HEAD_A)
    z = jnp.zeros_like(s[:, :, 0])
    return jnp.concatenate([jnp.concatenate([s[:, :, 0], z], axis=-1),
                            jnp.concatenate([z, s[:, :, 1]], axis=-1)], axis=-2)


def _head_states(s2):
    b, p = s2.shape[:2]
    return jnp.stack([s2[:, :, :HEAD_A, :HEAD_A], s2[:, :, HEAD_A:, HEAD_A:]],
                     axis=2).reshape(b, 2 * p, HEAD_A, HEAD_A)


def _layer(x3, modm, group0, grid_mode, s0f, s0b, w, tc_tokens):
    b, t, d = x3.shape
    n = b * t
    x = x3.reshape(n, d)
    rows_per_group = t if grid_mode else n
    z = _in_proj(x, modm, w["g_pre1"], w["w_in"], group0, rows_per_group)
    zs = _shift_mix(z, w["mu_shift"], w["n_shift"], w["lora_block"], grid_mode, t)
    zero_init = s0f is None
    if zero_init:
        s0f = s0b = jnp.zeros((b, w["w0"].shape[1] // LANES, LANES, LANES), F32)
    yf, yb, sf, sb = _rwkv_scan(zs.reshape(b, t, -1), w["w0"], w["w2"], w["a0"], w["a2"], w["k_k"], w["k_a"],
                                s0f, s0b, zero_init)
    d_a = yf.shape[-1]
    yag = _post_mix(yf.reshape(n, d_a), yb.reshape(n, d_a), zs, z, w["a0"], w["a2"], w["k_a"], w["r_k"],
                    w["ln_x_w"], w["ln_x_b"], w["ln_v_w"], w["ln_v_b"], w["w_s"], w["b_st"])
    x1, h2 = _out_proj(yag, w["w_out"], x, modm, w["g_post1"], w["g_pre2"], group0, rows_per_group)
    scores = _query_scores(h2, w["w_query"], w["sub_keys"])
    idx_t, gates_t = _peer_topk(scores)
    peer = _peer_experts(idx_t.T, h2, gates_t.T, w["experts"], w["expert_rows"], tc_tokens)
    out = _final(x1, peer, modm, w["g_post2"], group0, rows_per_group)
    return out.reshape(b, t, d), sf, sb


def kernel(x_prompt, x_sample, c, state_fwd, state_bwd, c_ctx, w_ada, b_ada, g_pre1, g_post1, g_pre2, g_post2,
           w_in, mu_shift, w0, w2, a0, a2, k_k, k_a, r_k, ln_x_w, ln_x_b, ln_v_w, ln_v_b, w_s, b_s, w_out,
           w_query, sub_keys, expert_u, expert_v):
    depth = w_in.shape[0]
    d = x_prompt.shape[-1]
    d_a = w0.shape[-1]
    n_shift = mu_shift.shape[-1]
    dec_b = x_sample.shape[0]
    cvec = jnp.concatenate([c_ctx[None], c, jnp.zeros((8 - 1 - dec_b, d), F32)], axis=0)
    xp, xs = x_prompt, x_sample
    new_f, new_b = [], []
    for l in range(depth):
        wl_in = w_in[l]
        row = lambda a: a[l].reshape(1, -1)
        w = {
            "w_in": jnp.concatenate([wl_in[:, :4 * d_a], wl_in[:, n_shift:], wl_in[:, 4 * d_a:n_shift]],
                                    axis=1).astype(BF16),
            "n_shift": n_shift,
            "lora_block": (wl_in.shape[1] - (n_shift - 4 * d_a)) // (2 * LANES),
            "mu_shift": row(mu_shift),
            "g_pre1": row(g_pre1), "g_post1": row(g_post1), "g_pre2": row(g_pre2), "g_post2": row(g_post2),
            "w0": w0[l], "w2": w2[l].reshape(-1, d_a), "a0": a0[l], "a2": a2[l].reshape(-1, d_a),
            "k_k": row(k_k), "k_a": row(k_a), "r_k": row(r_k),
            "ln_x_w": row(ln_x_w), "ln_x_b": row(ln_x_b), "ln_v_w": row(ln_v_w), "ln_v_b": row(ln_v_b),
            "w_s": w_s[l].astype(BF16), "b_st": b_s[l].T,
            "w_out": w_out[l].astype(BF16), "w_query": w_query[l].astype(BF16),
            "sub_keys": jnp.swapaxes(sub_keys[l], 0, 1).reshape(-1, N_KEYS, sub_keys.shape[-1]).astype(BF16),
            "experts": _pack_experts(expert_u[l], expert_v[l]),
        }
        w["expert_rows"] = w["experts"].reshape(-1, LANES)
        modm = _modulation(cvec, w_ada[l], b_ada[l].reshape(1, -1)).reshape(8, 6, d)
        piece = xp.shape[0] // CTX_PARTS
        parts = [_layer(xp[i * piece:(i + 1) * piece], modm, 0, False, None, None, w, PEER_TC_CTX)
                 for i in range(CTX_PARTS)]
        xp = jnp.concatenate([p[0] for p in parts], axis=0)
        new_f.append(_head_states(jnp.concatenate([p[1] for p in parts], axis=0)))
        new_b.append(_head_states(jnp.concatenate([p[2] for p in parts], axis=0)))
        xs, _, _ = _layer(xs, modm, 1, True, _block_diag_state(state_fwd[:, l]),
                          _block_diag_state(state_bwd[:, l]), w, PEER_TC_LATENT)
    return (xp, xs, jnp.stack(new_f, axis=1), jnp.stack(new_b, axis=1))
```

```python
import functools

import jax
import jax.numpy as jnp
from jax import lax
from jax.experimental import pallas as pl
from jax.experimental.pallas import tpu as pltpu
from jax.experimental.pallas import tpu_sc as plsc

F32 = jnp.float32
BF16 = jnp.bfloat16
I32 = jnp.int32

EPS = 1e-6
GN_EPS = 64e-5
HEAD_A = 64
LANES = 128
GRID_W = 64
GMLP_CHUNK = 128
PK_HEADS = 8
N_KEYS = 128
TOPK = 16
SCAN_CHUNK = 64
SCAN_PAIRS = 8
DECAY_SCALE = 0.6065306597126334
VMEM_LIMIT = 48 * 1024 * 1024


def _params(sem):
    return pltpu.CompilerParams(dimension_semantics=sem, vmem_limit_bytes=VMEM_LIMIT)


def _sigmoid(x):
    return 1.0 / (1.0 + jnp.exp(-x))


def _gelu(x):
    return 0.5 * x * (1.0 + jnp.tanh(0.7978845608028654 * (x + 0.044715 * (x * x * x))))


def _dot(a, b):
    return jnp.dot(a.astype(BF16), b.astype(BF16), preferred_element_type=F32)


def _dot_nt(a, b):
    return lax.dot_general(a.astype(BF16), b.astype(BF16), (((1,), (1,)), ((), ())),
                           preferred_element_type=F32)


def _dot_tn(a, b):
    return lax.dot_general(a.astype(BF16), b.astype(BF16), (((0,), (0,)), ((), ())),
                           preferred_element_type=F32)


def _split(x):
    hi = x.astype(BF16)
    return hi, (x - hi.astype(F32)).astype(BF16)


def _dot_x3(a, b):
    a_hi, a_lo = _split(a)
    b_hi, b_lo = _split(b)
    dot = functools.partial(jnp.dot, preferred_element_type=F32)
    return dot(a_hi, b_hi) + (dot(a_lo, b_hi) + dot(a_hi, b_lo))


def _dot_split_rhs(a_bf, b):
    b_hi, b_mid = _split(b)
    b_lo = (b - b_hi.astype(F32) - b_mid.astype(F32)).astype(BF16)
    dot = functools.partial(jnp.dot, preferred_element_type=F32)
    return dot(a_bf, b_hi) + (dot(a_bf, b_mid) + dot(a_bf, b_lo))


def _head_sum(x, first_head):
    s_a = jnp.sum(jnp.where(first_head, x, 0.0), axis=-1, keepdims=True)
    s_b = jnp.sum(jnp.where(first_head, 0.0, x), axis=-1, keepdims=True)
    return jnp.where(first_head, s_a, s_b)


def _mod_kernel(c_ref, w_ref, b_ref, o_ref):
    c = c_ref[...]
    o_ref[...] = _dot(c * _sigmoid(c), w_ref[...]) + b_ref[...]


def _modulation(cvec, w_ada, b_ada):
    rows, d = cvec.shape
    n = w_ada.shape[1]
    tn = 1024
    return pl.pallas_call(
        _mod_kernel,
        out_shape=jax.ShapeDtypeStruct((rows, n), F32),
        grid=(n // tn,),
        in_specs=[pl.BlockSpec((rows, d), lambda j: (0, 0)),
                  pl.BlockSpec((d, tn), lambda j: (0, j)),
                  pl.BlockSpec((1, tn), lambda j: (0, j))],
        out_specs=pl.BlockSpec((rows, tn), lambda j: (0, j)),
        compiler_params=_params(("parallel",)),
        name="adaln_mod",
    )(cvec, w_ada, b_ada)


def _in_proj_kernel(x_ref, mod_ref, g_ref, w_ref, o_ref, h_ref):
    @pl.when(pl.program_id(1) == 0)
    def _():
        x = x_ref[...]
        y = x * lax.rsqrt(jnp.mean(x * x, axis=-1, keepdims=True) + EPS) * g_ref[...]
        h_ref[...] = (y * (1.0 + mod_ref[0, 1:2, :]) + mod_ref[0, 0:1, :]).astype(BF16)

    o_ref[...] = jnp.dot(h_ref[...], w_ref[...], preferred_element_type=F32)


def _in_proj(x, modm, g_pre, w_bf, group0, rows_per_group):
    n, d = x.shape
    p = w_bf.shape[1]
    tm, tn = 512, 1280
    grp = lambda i, j: (group0 + (i * tm) // rows_per_group, 0, 0)
    return pl.pallas_call(
        _in_proj_kernel,
        out_shape=jax.ShapeDtypeStruct((n, p), F32),
        grid=(n // tm, p // tn),
        in_specs=[pl.BlockSpec((tm, d), lambda i, j: (i, 0)),
                  pl.BlockSpec((1, 6, d), grp),
                  pl.BlockSpec((1, d), lambda i, j: (0, 0)),
                  pl.BlockSpec((d, tn), lambda i, j: (0, j))],
        out_specs=pl.BlockSpec((tm, tn), lambda i, j: (i, j)),
        scratch_shapes=[pltpu.VMEM((tm, d), BF16)],
        compiler_params=_params(("parallel", "arbitrary")),
        name="in_proj",
    )(x, modm, g_pre, w_bf)


def _shift_kernel(z_ref, mu_ref, o_ref, *, grid_mode, period):
    z = z_ref[...]
    rows = z.shape[0]
    t = lax.broadcasted_iota(I32, (rows, 1), 0) % period
    prev = jnp.where(t % (GRID_W if grid_mode else period) != 0, pltpu.roll(z, 1, 0), 0.0)
    nxt = jnp.where(t % (GRID_W if grid_mode else period) != (GRID_W if grid_mode else period) - 1,
                    pltpu.roll(z, rows - 1, 0), 0.0)
    if grid_mode:
        up = jnp.where(t >= GRID_W, pltpu.roll(z, GRID_W, 0), 0.0)
        down = jnp.where(t < period - GRID_W, pltpu.roll(z, rows - GRID_W, 0), 0.0)
        nb = 0.25 * (up + down + prev + nxt)
    else:
        nb = 0.5 * (prev + nxt)
    o_ref[...] = z + mu_ref[...] * (nb - z)


def _shift_mix(z, mu, n_shift, lora_block, grid_mode, period):
    n = z.shape[0]
    tr, tc = 2048, 256
    main_blocks = (n_shift // tc) - 1
    col = lambda i, j: (i, jnp.where(j < main_blocks, j, lora_block))
    return pl.pallas_call(
        functools.partial(_shift_kernel, grid_mode=grid_mode, period=period),
        out_shape=jax.ShapeDtypeStruct((n, n_shift), F32),
        grid=(n // tr, n_shift // tc),
        in_specs=[pl.BlockSpec((tr, tc), col),
                  pl.BlockSpec((1, tc), lambda i, j: (0, j))],
        out_specs=pl.BlockSpec((tr, tc), lambda i, j: (i, j)),
        compiler_params=_params(("parallel", "parallel")),
        name="token_shift",
    )(z, mu)


def _scan_chunks(chains):
    c = chains[0][0].shape[0]
    c2 = 2 * c
    n = len(chains)
    fwd = [ch[9] for ch in chains]
    first_head = lax.broadcasted_iota(I32, (1, LANES), 1) < HEAD_A
    row = lax.broadcasted_iota(I32, (c2, c2), 0)
    col = lax.broadcasted_iota(I32, (c2, c2), 1)
    eye = jnp.where(row == col, 1.0, 0.0)

    def stack(x):
        return jnp.concatenate([jnp.where(first_head, x, 0.0), jnp.where(first_head, 0.0, x)],
                               axis=0).astype(BF16)

    lhs, rhs, v2, total = [], [], [], []
    for r, k, v, logw, cl, a, k_k, k_a, _, forward in chains:
        kkr = k * k_k
        kk = kkr / jnp.maximum(jnp.sqrt(_head_sum(kkr * kkr, first_head)), 1e-12)
        kd = k * (1.0 + (a - 1.0) * k_a)
        gi = jnp.exp(-cl)
        lhs.append(jnp.concatenate([stack(kk * jnp.exp(cl - logw)), stack(r * jnp.exp(cl))], axis=0))
        rhs.append(jnp.concatenate([stack(kk * a * gi), stack(kd * gi)], axis=0))
        v2.append(stack(v))
        total.append(cl[c - 1:c, :] if forward else cl[0:1, :])

    res = [_dot_nt(lhs[i], rhs[i]) for i in range(n)]
    pr = [_dot_nt(lhs[i], chains[i][8]) for i in range(n)]
    strict = [(row > col) if f else (row < col) for f in fwd]
    incl = [(row >= col) if f else (row <= col) for f in fwd]
    ab = [jnp.where(strict[i], res[i][:c2, :c2], 0.0) for i in range(n)]
    ak = [jnp.where(strict[i], res[i][:c2, c2:], 0.0) for i in range(n)]
    gb_gk = [jnp.concatenate([jnp.where(incl[i], res[i][c2:, :c2], 0.0),
                              jnp.where(incl[i], res[i][c2:, c2:], 0.0)], axis=1).astype(BF16) for i in range(n)]
    akv = [_dot(ak[i], v2[i]) for i in range(n)]

    tinv = [eye - ab[i] for i in range(n)]
    pw = [-ab[i] for i in range(n)]
    for _ in range(c.bit_length() - 2):
        pw = [_dot(pw[i], pw[i]) for i in range(n)]
        tinv = [tinv[i] + _dot(tinv[i], pw[i]) for i in range(n)]

    u2 = [_dot(tinv[i], -pr[i][:c2, :] - akv[i]) for i in range(n)]
    uv = [jnp.concatenate([u2[i].astype(BF16), v2[i]], axis=0) for i in range(n)]
    y2 = [pr[i][c2:, :] + _dot(gb_gk[i], uv[i]) for i in range(n)]
    s_new = [(chains[i][8] + _dot_tn(uv[i], rhs[i])) * jnp.exp(total[i]) for i in range(n)]
    return [(y2[i][:c, :] + y2[i][c:, :], s_new[i]) for i in range(n)]


def _scan_direction(r_ref, k_ref, v_ref, l_ref, d, s_ref, w0, w2, a0, a2, kk, ka, forward):
    lora = l_ref[0]
    c = lora.shape[0]
    dir_rows = (lax.broadcasted_iota(I32, (LANES, 1), 0) // HEAD_A) == d
    wl = w0[d:d + 1, :] + _dot_x3(jnp.tanh(lora[:, :LANES]), jnp.where(dir_rows, w2[...], 0.0))
    logw = -DECAY_SCALE * _sigmoid(wl)
    a = _sigmoid(a0[d:d + 1, :] + _dot_x3(lora[:, LANES:], jnp.where(dir_rows, a2[...], 0.0)))
    ti = lax.broadcasted_iota(I32, (c, c), 0)
    tj = lax.broadcasted_iota(I32, (c, c), 1)
    cum = jnp.where((tj <= ti) if forward else (tj >= ti), 1.0, 0.0).astype(BF16)
    cl = _dot_split_rhs(cum, logw)
    chains = []
    for p in range(s_ref.shape[0]):
        sl = slice(p * LANES, (p + 1) * LANES)
        chains.append((r_ref[0, :, sl], k_ref[0, :, sl], v_ref[0, :, sl], logw[:, sl], cl[:, sl],
                       a[:, sl], kk[:, sl], ka[:, sl], s_ref[p], forward))
    return chains


def _scan_store(out, y_ref, s_ref):
    for p, (y, s_new) in enumerate(out):
        y_ref[0, :, p * LANES:(p + 1) * LANES] = y
        s_ref[p] = s_new


def _scan_kernel(rf, kf, vf, lf, rb, kb, vb, lb, w0, w2, a0, a2, kk, ka, s0f, s0b,
                 yf, yb, sf, sb, s2f, s2b, *, zero_init):
    c = pl.program_id(2)

    @pl.when(c == 0)
    def _():
        if zero_init:
            s2f[...] = jnp.zeros_like(s2f)
            s2b[...] = jnp.zeros_like(s2b)
        else:
            s2f[...] = s0f[0]
            s2b[...] = s0b[0]

    chains_f = _scan_direction(rf, kf, vf, lf, 0, s2f, w0, w2, a0, a2, kk, ka, True)
    chains_b = _scan_direction(rb, kb, vb, lb, 1, s2b, w0, w2, a0, a2, kk, ka, False)
    out = _scan_chunks(chains_f + chains_b)
    _scan_store(out[:len(chains_f)], yf, s2f)
    _scan_store(out[len(chains_f):], yb, s2b)

    @pl.when(c == pl.num_programs(2) - 1)
    def _():
        sf[0] = s2f[...]
        sb[0] = s2b[...]


def _rwkv_scan(zs, w0, w2r, a0, a2r, k_k, k_a, s0f, s0b, zero_init):
    b, t, _ = zs.shape
    d_a = w0.shape[1]
    pp = SCAN_PAIRS
    wide = pp * LANES
    groups = d_a // wide
    c = SCAN_CHUNK
    nc = t // c
    lora_blk = (4 * d_a) // (2 * LANES)
    fw = lambda off: (lambda i, q, j: (i, j, off + q))
    bw = lambda off: (lambda i, q, j: (i, nc - 1 - j, off + q))
    par = lambda i, q, j: (0, q)
    st = lambda i, q, j: (i, q, 0, 0)
    blk = (1, c, wide)
    lblk = (1, c, 2 * LANES)
    sblk = (1, pp, LANES, LANES)
    yshape = jax.ShapeDtypeStruct((b, t, d_a), F32)
    sshape = jax.ShapeDtypeStruct((b, d_a // LANES, LANES, LANES), F32)
    return pl.pallas_call(
        functools.partial(_scan_kernel, zero_init=zero_init),
        out_shape=(yshape, yshape, sshape, sshape),
        grid=(b, groups, nc),
        in_specs=[pl.BlockSpec(blk, fw(0)), pl.BlockSpec(blk, fw(groups)), pl.BlockSpec(blk, fw(2 * groups)),
                  pl.BlockSpec(lblk, lambda i, q, j: (i, j, lora_blk)),
                  pl.BlockSpec(blk, bw(0)), pl.BlockSpec(blk, bw(groups)), pl.BlockSpec(blk, bw(2 * groups)),
                  pl.BlockSpec(lblk, lambda i, q, j: (i, nc - 1 - j, lora_blk)),
                  pl.BlockSpec((2, wide), par), pl.BlockSpec((LANES, wide), par),
                  pl.BlockSpec((2, wide), par), pl.BlockSpec((LANES, wide), par),
                  pl.BlockSpec((1, wide), par), pl.BlockSpec((1, wide), par),
                  pl.BlockSpec(sblk, st), pl.BlockSpec(sblk, st)],
        out_specs=(pl.BlockSpec(blk, fw(0)), pl.BlockSpec(blk, bw(0)),
                   pl.BlockSpec(sblk, st), pl.BlockSpec(sblk, st)),
        scratch_shapes=[pltpu.VMEM((pp, LANES, LANES), F32), pltpu.VMEM((pp, LANES, LANES), F32)],
        compiler_params=_params(("parallel", "parallel", "arbitrary")),
        name="rwkv7_scan",
    )(zs, zs, zs, zs, zs, zs, zs, zs, w0, w2r, a0, a2r, k_k, k_a, s0f, s0b)


def _post_kernel(yf, yb, r, k, v, g, lora, u, vg, a0, a2, ka, rk, lxw, lxb, lvw, lvb, ws, bst, o_ref):
    tm = yf.shape[0]
    d_a = yf.shape[1]
    first_head = lax.broadcasted_iota(I32, (1, LANES), 1) < HEAD_A
    dir_row = lax.broadcasted_iota(I32, (LANES, 1), 0) // HEAD_A
    la = lora[:, LANES:]
    a_sum = jnp.zeros((tm, d_a), F32)
    for d in range(2):
        a_sum = a_sum + _sigmoid(a0[d:d + 1, :] + _dot_x3(la, jnp.where(dir_row == d, a2[...], 0.0)))
    rkk = r[...] * k[...] * (2.0 + (a_sum - 2.0) * ka[...]) * rk[...]
    y = yf[...] + yb[...]
    inv = 1.0 / HEAD_A
    for j in range(d_a // LANES):
        sl = slice(j * LANES, (j + 1) * LANES)
        yj = y[:, sl]
        mu = _head_sum(yj, first_head) * inv
        dl = yj - mu
        var = _head_sum(dl * dl, first_head) * inv
        yn = dl * lax.rsqrt(var + GN_EPS) * lxw[:, sl] + lxb[:, sl]
        bonus = _head_sum(rkk[:, sl], first_head) * v[:, sl]
        o_ref[:, sl] = ((yn + bonus) * _sigmoid(g[:, sl])).astype(BF16)

    uu = _gelu(u[...])
    vv = _gelu(vg[...])
    mu = jnp.mean(vv, axis=-1, keepdims=True)
    dv = vv - mu
    vn = dv * lax.rsqrt(jnp.mean(dv * dv, axis=-1, keepdims=True) + EPS) * lvw[...] + lvb[...]
    for ch in range(tm // GMLP_CHUNK):
        rows = slice(ch * GMLP_CHUNK, (ch + 1) * GMLP_CHUNK)
        for h in range(ws.shape[0]):
            cols = slice(h * LANES, (h + 1) * LANES)
            sp = _dot(ws[h], vn[rows, cols]) + bst[:, h:h + 1]
            o_ref[rows, d_a + h * LANES:d_a + (h + 1) * LANES] = (uu[rows, cols] * sp).astype(BF16)


def _post_mix(yf, yb, zs, z, a0, a2r, k_a, r_k, lxw, lxb, lvw, lvb, w_s, b_st):
    n, d_a = yf.shape
    tm = 256
    wide = lambda j: (lambda i: (i, j))
    full = lambda shape: pl.BlockSpec(shape, lambda i: (0,) * len(shape))
    lora_blk = (4 * d_a) // (2 * LANES)
    return pl.pallas_call(
        _post_kernel,
        out_shape=jax.ShapeDtypeStruct((n, 2 * d_a), BF16),
        grid=(n // tm,),
        in_specs=[pl.BlockSpec((tm, d_a), wide(0)), pl.BlockSpec((tm, d_a), wide(0)),
                  pl.BlockSpec((tm, d_a), wide(0)), pl.BlockSpec((tm, d_a), wide(1)),
                  pl.BlockSpec((tm, d_a), wide(2)), pl.BlockSpec((tm, d_a), wide(3)),
                  pl.BlockSpec((tm, 2 * LANES), wide(lora_blk)),
                  pl.BlockSpec((tm, d_a), wide(4)), pl.BlockSpec((tm, d_a), wide(5)),
                  full((2, d_a)), full((LANES, d_a)), full((1, d_a)), full((1, d_a)),
                  full((1, d_a)), full((1, d_a)), full((1, d_a)), full((1, d_a)),
                  full(w_s.shape), full(b_st.shape)],
        out_specs=pl.BlockSpec((tm, 2 * d_a), wide(0)),
        compiler_params=_params(("parallel",)),
        name="mix_post",
    )(yf, yb, zs, zs, zs, zs, zs, z, z, a0, a2r, k_a, r_k, lxw, lxb, lvw, lvb, w_s, b_st)


def _rms(x, g):
    return x * lax.rsqrt(jnp.mean(x * x, axis=-1, keepdims=True) + EPS) * g


def _out_proj_kernel(a_ref, w_ref, x_ref, mod_ref, gpost_ref, gpre_ref, x1_ref, h2_ref):
    o = jnp.dot(a_ref[...], w_ref[...], preferred_element_type=F32)
    x1 = x_ref[...] + mod_ref[0, 2:3, :] * _rms(o, gpost_ref[...])
    x1_ref[...] = x1
    h2_ref[...] = _rms(x1, gpre_ref[...]) * (1.0 + mod_ref[0, 4:5, :]) + mod_ref[0, 3:4, :]


def _out_proj(yag, w_bf, x, modm, g_post, g_pre2, group0, rows_per_group):
    n, d = x.shape
    tm = 256
    grp = lambda i: (group0 + (i * tm) // rows_per_group, 0, 0)
    row = lambda i: (i, 0)
    fix = lambda i: (0, 0)
    shp = jax.ShapeDtypeStruct((n, d), F32)
    return pl.pallas_call(
        _out_proj_kernel,
        out_shape=(shp, shp),
        grid=(n // tm,),
        in_specs=[pl.BlockSpec((tm, yag.shape[1]), row), pl.BlockSpec(w_bf.shape, fix),
                  pl.BlockSpec((tm, d), row), pl.BlockSpec((1, 6, d), grp),
                  pl.BlockSpec((1, d), fix), pl.BlockSpec((1, d), fix)],
        out_specs=(pl.BlockSpec((tm, d), row), pl.BlockSpec((tm, d), row)),
        compiler_params=_params(("parallel",)),
        name="out_proj",
    )(yag, w_bf, x, modm, g_post, g_pre2)


def _query_kernel(h_ref, wq_ref, sk_ref, o_ref):
    q = jnp.dot(h_ref[...].astype(BF16), wq_ref[...], preferred_element_type=F32)
    for g in range(sk_ref.shape[0]):
        o_ref[g * N_KEYS:(g + 1) * N_KEYS, :] = _dot_nt(sk_ref[g], q[:, g * LANES:(g + 1) * LANES])


def _query_scores(h2, wq_bf, sk):
    n, d = h2.shape
    tm = 256
    groups = sk.shape[0]
    return pl.pallas_call(
        _query_kernel,
        out_shape=jax.ShapeDtypeStruct((groups * N_KEYS, n), F32),
        grid=(n // tm,),
        in_specs=[pl.BlockSpec((tm, d), lambda i: (i, 0)),
                  pl.BlockSpec(wq_bf.shape, lambda i: (0, 0)),
                  pl.BlockSpec(sk.shape, lambda i: (0, 0, 0))],
        out_specs=pl.BlockSpec((groups * N_KEYS, tm), lambda i: (0, i)),
        compiler_params=_params(("parallel",)),
        name="peer_query",
    )(h2, wq_bf, sk)


def _top16(s):
    rows = s.shape[0]
    iota = lax.broadcasted_iota(I32, s.shape, 0)
    vals, idxs = [], []
    for _ in range(TOPK):
        m = jnp.max(s, axis=0, keepdims=True)
        i = jnp.min(jnp.where(s == m, iota, rows), axis=0, keepdims=True)
        vals.append(m)
        idxs.append(i)
        s = jnp.where(iota == i, -jnp.inf, s)
    return jnp.concatenate(vals, axis=0), jnp.concatenate(idxs, axis=0)


def _topk_kernel(s_ref, idx_ref, gate_ref):
    def head(h, carry):
        base = pl.multiple_of(h * (2 * N_KEYS), 2 * N_KEYS)
        v1, i1 = _top16(s_ref[pl.ds(base, N_KEYS), :])
        v2, i2 = _top16(s_ref[pl.ds(base + N_KEYS, N_KEYS), :])
        cand = jnp.concatenate([v1[i:i + 1, :] + v2 for i in range(TOPK)], axis=0)
        top_s, pos = _top16(cand)
        pi = pos // TOPK
        pj = pos % TOPK
        e1 = jnp.zeros_like(pos)
        e2 = jnp.zeros_like(pos)
        for i in range(TOPK):
            e1 = e1 + jnp.where(pi == i, i1[i:i + 1, :], 0)
            e2 = e2 + jnp.where(pj == i, i2[i:i + 1, :], 0)
        ex = jnp.exp(top_s - top_s[0:1, :])
        out = pl.multiple_of(h * TOPK, TOPK)
        idx_ref[pl.ds(out, TOPK), :] = e1 * N_KEYS + e2
        gate_ref[pl.ds(out, TOPK), :] = ex / jnp.sum(ex, axis=0, keepdims=True)
        return carry

    lax.fori_loop(0, PK_HEADS, head, 0)


def _peer_topk(scores):
    rows, n = scores.shape
    tt = 256
    out_rows = PK_HEADS * TOPK
    return pl.pallas_call(
        _topk_kernel,
        out_shape=(jax.ShapeDtypeStruct((out_rows, n), I32), jax.ShapeDtypeStruct((out_rows, n), F32)),
        grid=(n // tt,),
        in_specs=[pl.BlockSpec((rows, tt), lambda i: (0, i))],
        out_specs=(pl.BlockSpec((out_rows, tt), lambda i: (0, i)),
                   pl.BlockSpec((out_rows, tt), lambda i: (0, i))),
        compiler_params=_params(("parallel",)),
        name="peer_topk",
    )(scores)


SC_CORES = 2
SC_SUBCORES = 16
SC_LANES = 16
PEER_GROUP = 16
PEER_BLOCK = 8
U_MASK = -65536


def _pack_experts(eu, ev):
    hi = lax.bitcast_convert_type(eu.astype(BF16), jnp.uint16).astype(jnp.uint32)
    lo = lax.bitcast_convert_type(ev.astype(BF16), jnp.uint16).astype(jnp.uint32)
    return lax.bitcast_convert_type((hi << 16) | lo, I32)


def _unpack_u(w):
    return lax.bitcast_convert_type(w & U_MASK, F32)


def _unpack_v(w):
    return lax.bitcast_convert_type(w << 16, F32)


def _peer_experts_sc(idx, x, gates, table):
    n, d = x.shape
    n_sel = idx.shape[1]
    per_w = n // (SC_CORES * SC_SUBCORES)
    L, G, TB = SC_LANES, PEER_GROUP, PEER_BLOCK
    n_groups = n_sel // G
    n_ch = d // L
    mesh = plsc.VectorSubcoreMesh(core_axis_name="c", subcore_axis_name="s")

    @functools.partial(
        pl.kernel, mesh=mesh, out_type=jax.ShapeDtypeStruct((n, d), F32),
        compiler_params=pltpu.CompilerParams(needs_layout_passes=False),
        scratch_types=[pltpu.VMEM((TB, n_sel), I32), pltpu.VMEM((TB, n_sel), F32),
                       pltpu.VMEM((2, d), F32), pltpu.VMEM((2, d), F32),
                       pltpu.VMEM((2, G, d), I32), pltpu.VMEM((G, 2 * L), F32), pltpu.VMEM((2 * L,), F32),
                       pltpu.SemaphoreType.DMA((2,)), pltpu.SemaphoreType.DMA((2,)), pltpu.SemaphoreType.DMA((2,))],
        name="peer_experts_sc")
    def k(idx_hbm, x_hbm, g_hbm, tab_hbm, o_hbm, idx_v, gate_v, x_v, o_v, buf, acc_v, coef_v, sem, xsem, osem):
        base = (lax.axis_index("s") * SC_CORES + lax.axis_index("c")) * per_w
        lane = lax.iota(I32, L)
        zero = jnp.zeros((L,), F32)

        def gather(t, g, slot):
            return pltpu.make_async_copy(tab_hbm.at[idx_v.at[t, pl.ds(g * G, G)]], buf.at[slot], sem.at[slot])

        def x_copy(j, par):
            return pltpu.make_async_copy(x_hbm.at[base + j], x_v.at[par], xsem.at[par])

        def o_copy(j, par):
            return pltpu.make_async_copy(o_v.at[par], o_hbm.at[base + j], osem.at[par])

        x_copy(0, 0).start()

        @pl.loop(0, per_w // TB)
        def _(blk):
            tok0 = base + blk * TB
            pltpu.sync_copy(idx_hbm.at[pl.ds(tok0, TB)], idx_v)
            pltpu.sync_copy(g_hbm.at[pl.ds(tok0, TB)], gate_v)

            @pl.loop(0, TB)
            def _(t):
                j = blk * TB + t
                par = j % 2
                gather(t, 0, 0).start()

                @pl.when(j + 1 < per_w)
                def _():
                    x_copy(j + 1, 1 - par).start()

                x_copy(j, par).wait()

                @pl.when(j >= 2)
                def _():
                    o_copy(j - 2, par).wait()

                for g in range(n_groups):
                    slot = g % 2
                    if g + 1 < n_groups:
                        gather(t, g + 1, 1 - slot).start()
                    gather(t, g, slot).wait()

                    def u_body(ch, accs):
                        xv = x_v[par, pl.ds(ch * L, L)]
                        return tuple(accs[e] + _unpack_u(buf[slot, e, pl.ds(ch * L, L)]) * xv for e in range(G))

                    accs = plsc.parallel_loop(0, n_ch, carry=tuple(zero for _ in range(G)))(u_body)
                    for e in range(G):
                        acc_v[e, pl.ds(L, L)] = accs[e]
                    act = zero
                    for l in range(L):
                        act = act + plsc.load_gather(acc_v, [lane, jnp.full((L,), L + l, I32)])
                    y = 0.7978845608028654 * (act + 0.044715 * (act * act * act))
                    coef_v[pl.ds(L, L)] = act / (1.0 + jnp.exp(-2.0 * y)) * gate_v[t, pl.ds(g * G, G)]
                    cs = [plsc.load_gather(coef_v, [jnp.full((L,), L + e, I32)]) for e in range(G)]

                    @plsc.parallel_loop(0, n_ch)
                    def _(ch):
                        o = zero if g == 0 else o_v[par, pl.ds(ch * L, L)]
                        for e in range(G):
                            o = o + cs[e] * _unpack_v(buf[slot, e, pl.ds(ch * L, L)])
                        o_v[par, pl.ds(ch * L, L)] = o

                o_copy(j, par).start()

        o_copy(per_w - 2, 0).wait()
        o_copy(per_w - 1, 1).wait()

    return k(idx, x, gates, table)


PEER_TC_TOKENS = 64
CTX_PARTS = 4
PEER_TC_SLOTS = 3
PEER_TC_CTX = 512
PEER_TC_LATENT = 3072


def _peer_tc_kernel(idx_ref, x_ref, g_ref, uv_ref, o_ref, buf, sem):
    slots, rows, n_sel = buf.shape[0], buf.shape[1], buf.shape[2]
    tokens = g_ref.shape[0]
    ahead = slots - 1

    def issue(t):
        slot = t % slots

        def body(e, carry):
            src = pl.multiple_of(idx_ref[t, e] * rows, rows)
            pltpu.make_async_copy(uv_ref.at[pl.ds(src, rows), :], buf.at[slot, :, e, :], sem.at[slot]).start()
            return carry
        lax.fori_loop(0, n_sel, body, 0, unroll=8)

    def wait(slot):
        pltpu.make_async_copy(buf.at[(slot + 1) % slots], buf.at[slot], sem.at[slot]).wait()

    for t0 in range(ahead):
        issue(t0)
    g_t = g_ref[...].T
    g_hi, g_mid = _split(g_t)
    g_lo = (g_t - g_hi.astype(F32) - g_mid.astype(F32)).astype(BF16)
    tok_iota = lax.broadcasted_iota(I32, (tokens, LANES), 0)
    dot = functools.partial(jnp.dot, preferred_element_type=F32)

    def token(t, carry):
        slot = t % slots

        @pl.when(t + ahead < tokens)
        def _():
            issue(t + ahead)

        wait(slot)
        pick = jnp.where(tok_iota == t, 1.0, 0.0).astype(BF16)
        gate = dot(g_hi, pick) + (dot(g_mid, pick) + dot(g_lo, pick))
        base = pl.multiple_of(t * rows, rows)
        xt = x_ref[pl.ds(base, rows), :]
        acc = _unpack_u(buf[slot, 0]) * xt[0:1, :]
        for c in range(1, rows):
            acc = acc + _unpack_u(buf[slot, c]) * xt[c:c + 1, :]
        act = jnp.sum(acc, axis=-1, keepdims=True)
        coef = _gelu(act) * gate
        o_ref[pl.ds(base, rows), :] = jnp.concatenate(
            [jnp.sum(coef * _unpack_v(buf[slot, c]), axis=0, keepdims=True) for c in range(rows)], axis=0)
        return carry

    lax.fori_loop(0, tokens, token, 0)


def _peer_experts_tc(idx, x, gates, table_rows):
    n, d = x.shape
    n_sel = idx.shape[1]
    tt = PEER_TC_TOKENS
    rows = d // LANES
    out = pl.pallas_call(
        _peer_tc_kernel,
        out_shape=jax.ShapeDtypeStruct((n * rows, LANES), F32),
        grid=(n // tt,),
        in_specs=[pl.BlockSpec((tt, n_sel), lambda i: (i, 0), memory_space=pltpu.SMEM),
                  pl.BlockSpec((tt * rows, LANES), lambda i: (i, 0)),
                  pl.BlockSpec((tt, n_sel), lambda i: (i, 0)),
                  pl.BlockSpec(memory_space=pl.ANY)],
        out_specs=pl.BlockSpec((tt * rows, LANES), lambda i: (i, 0)),
        scratch_shapes=[pltpu.VMEM((PEER_TC_SLOTS, rows, n_sel, LANES), I32),
                        pltpu.SemaphoreType.DMA((PEER_TC_SLOTS,))],
        compiler_params=pltpu.CompilerParams(dimension_semantics=("arbitrary",),
                                             vmem_limit_bytes=VMEM_LIMIT, disable_bounds_checks=True),
        name="peer_experts_tc",
    )(idx, x.reshape(n * rows, LANES), gates, table_rows)
    return out.reshape(n, d)


def _peer_experts(idx, x, gates, table, table_rows, m):
    out_tc = _peer_experts_tc(idx[:m], x[:m], gates[:m], table_rows)
    out_sc = _peer_experts_sc(idx[m:], x[m:], gates[m:], table)
    return jnp.concatenate([out_tc, out_sc], axis=0)


def _final_kernel(x1_ref, p_ref, mod_ref, g_ref, o_ref):
    o_ref[...] = x1_ref[...] + mod_ref[0, 5:6, :] * _rms(p_ref[...], g_ref[...])


def _final(x1, peer, modm, g_post2, group0, rows_per_group):
    n, d = x1.shape
    tm = 512
    row = lambda i: (i, 0)
    return pl.pallas_call(
        _final_kernel,
        out_shape=jax.ShapeDtypeStruct((n, d), F32),
        grid=(n // tm,),
        in_specs=[pl.BlockSpec((tm, d), row), pl.BlockSpec((tm, d), row),
                  pl.BlockSpec((1, 6, d), lambda i: (group0 + (i * tm) // rows_per_group, 0, 0)),
                  pl.BlockSpec((1, d), lambda i: (0, 0))],
        out_specs=pl.BlockSpec((tm, d), row),
        compiler_params=_params(("parallel",)),
        name="final_residual",
    )(x1, peer, modm, g_post2)


def _block_diag_state(s):
    b, h = s.shape[:2]
    s = s.reshape(b, h // 2, 2, HEAD_A, HEAD_A)
    z = jnp.zeros_like(s[:, :, 0])
    return jnp.concatenate([jnp.concatenate([s[:, :, 0], z], axis=-1),
                            jnp.concatenate([z, s[:, :, 1]], axis=-1)], axis=-2)


def _head_states(s2):
    b, p = s2.shape[:2]
    return jnp.stack([s2[:, :, :HEAD_A, :HEAD_A], s2[:, :, HEAD_A:, HEAD_A:]],
                     axis=2).reshape(b, 2 * p, HEAD_A, HEAD_A)


def _layer(x3, modm, group0, grid_mode, s0f, s0b, w, tc_tokens):
    b, t, d = x3.shape
    n = b * t
    x = x3.reshape(n, d)
    rows_per_group = t if grid_mode else n
    z = _in_proj(x, modm, w["g_pre1"], w["w_in"], group0, rows_per_group)
    zs = _shift_mix(z, w["mu_shift"], w["n_shift"], w["lora_block"], grid_mode, t)
    zero_init = s0f is None
    if zero_init:
        s0f = s0b = jnp.zeros((b, w["w0"].shape[1] // LANES, LANES, LANES), F32)
    yf, yb, sf, sb = _rwkv_scan(zs.reshape(b, t, -1), w["w0"], w["w2"], w["a0"], w["a2"], w["k_k"], w["k_a"],
                                s0f, s0b, zero_init)
    d_a = yf.shape[-1]
    yag = _post_mix(yf.reshape(n, d_a), yb.reshape(n, d_a), zs, z, w["a0"], w["a2"], w["k_a"], w["r_k"],
                    w["ln_x_w"], w["ln_x_b"], w["ln_v_w"], w["ln_v_b"], w["w_s"], w["b_st"])
    x1, h2 = _out_proj(yag, w["w_out"], x, modm, w["g_post1"], w["g_pre2"], group0, rows_per_group)
    scores = _query_scores(h2, w["w_query"], w["sub_keys"])
    idx_t, gates_t = _peer_topk(scores)
    peer = _peer_experts(idx_t.T, h2, gates_t.T, w["experts"], w["expert_rows"], tc_tokens)
    out = _final(x1, peer, modm, w["g_post2"], group0, rows_per_group)
    return out.reshape(b, t, d), sf, sb


def kernel(x_prompt, x_sample, c, state_fwd, state_bwd, c_ctx, w_ada, b_ada, g_pre1, g_post1, g_pre2, g_post2,
           w_in, mu_shift, w0, w2, a0, a2, k_k, k_a, r_k, ln_x_w, ln_x_b, ln_v_w, ln_v_b, w_s, b_s, w_out,
           w_query, sub_keys, expert_u, expert_v):
    depth = w_in.shape[0]
    d = x_prompt.shape[-1]
    d_a = w0.shape[-1]
    n_shift = mu_shift.shape[-1]
    dec_b = x_sample.shape[0]
    cvec = jnp.concatenate([c_ctx[None], c, jnp.zeros((8 - 1 - dec_b, d), F32)], axis=0)
    xp, xs = x_prompt, x_sample
    new_f, new_b = [], []
    for l in range(depth):
        wl_in = w_in[l]
        row = lambda a: a[l].reshape(1, -1)
        w = {
            "w_in": jnp.concatenate([wl_in[:, :4 * d_a], wl_in[:, n_shift:], wl_in[:, 4 * d_a:n_shift]],
                                    axis=1).astype(BF16),
            "n_shift": n_shift,
            "lora_block": (wl_in.shape[1] - (n_shift - 4 * d_a)) // (2 * LANES),
            "mu_shift": row(mu_shift),
            "g_pre1": row(g_pre1), "g_post1": row(g_post1), "g_pre2": row(g_pre2), "g_post2": row(g_post2),
            "w0": w0[l], "w2": w2[l].reshape(-1, d_a), "a0": a0[l], "a2": a2[l].reshape(-1, d_a),
            "k_k": row(k_k), "k_a": row(k_a), "r_k": row(r_k),
            "ln_x_w": row(ln_x_w), "ln_x_b": row(ln_x_b), "ln_v_w": row(ln_v_w), "ln_v_b": row(ln_v_b),
            "w_s": w_s[l].astype(BF16), "b_st": b_s[l].T,
            "w_out": w_out[l].astype(BF16), "w_query": w_query[l].astype(BF16),
            "sub_keys": jnp.swapaxes(sub_keys[l], 0, 1).reshape(-1, N_KEYS, sub_keys.shape[-1]).astype(BF16),
            "experts": _pack_experts(expert_u[l], expert_v[l]),
        }
        w["expert_rows"] = w["experts"].reshape(-1, LANES)
        modm = _modulation(cvec, w_ada[l], b_ada[l].reshape(1, -1)).reshape(8, 6, d)
        piece = xp.shape[0] // CTX_PARTS
        parts = [_layer(xp[i * piece:(i + 1) * piece], modm, 0, False, None, None, w, PEER_TC_CTX)
                 for i in range(CTX_PARTS)]
        xp = jnp.concatenate([p[0] for p in parts], axis=0)
        new_f.append(_head_states(jnp.concatenate([p[1] for p in parts], axis=0)))
        new_b.append(_head_states(jnp.concatenate([p[2] for p in parts], axis=0)))
        xs, _, _ = _layer(xs, modm, 1, True, _block_diag_state(state_fwd[:, l]),
                          _block_diag_state(state_bwd[:, l]), w, PEER_TC_LATENT)
    return (xp, xs, jnp.stack(new_f, axis=1), jnp.stack(new_b, axis=1))
```

```python
import functools

import jax
import jax.numpy as jnp
from jax import lax
from jax.experimental import pallas as pl
from jax.experimental.pallas import tpu as pltpu
from jax.experimental.pallas import tpu_sc as plsc

F32 = jnp.float32
BF16 = jnp.bfloat16
I32 = jnp.int32

EPS = 1e-6
GN_EPS = 64e-5
HEAD_A = 64
LANES = 128
GRID_W = 64
GMLP_CHUNK = 128
PK_HEADS = 8
N_KEYS = 128
TOPK = 16
SCAN_CHUNK = 64
SCAN_PAIRS = 8
DECAY_SCALE = 0.6065306597126334
VMEM_LIMIT = 48 * 1024 * 1024


def _params(sem):
    return pltpu.CompilerParams(dimension_semantics=sem, vmem_limit_bytes=VMEM_LIMIT)


def _sigmoid(x):
    return 1.0 / (1.0 + jnp.exp(-x))


def _gelu(x):
    return 0.5 * x * (1.0 + jnp.tanh(0.7978845608028654 * (x + 0.044715 * (x * x * x))))


def _dot(a, b):
    return jnp.dot(a.astype(BF16), b.astype(BF16), preferred_element_type=F32)


def _dot_nt(a, b):
    return lax.dot_general(a.astype(BF16), b.astype(BF16), (((1,), (1,)), ((), ())),
                           preferred_element_type=F32)


def _dot_tn(a, b):
    return lax.dot_general(a.astype(BF16), b.astype(BF16), (((0,), (0,)), ((), ())),
                           preferred_element_type=F32)


def _split(x):
    hi = x.astype(BF16)
    return hi, (x - hi.astype(F32)).astype(BF16)


def _dot_x3(a, b):
    a_hi, a_lo = _split(a)
    b_hi, b_lo = _split(b)
    dot = functools.partial(jnp.dot, preferred_element_type=F32)
    return dot(a_hi, b_hi) + (dot(a_lo, b_hi) + dot(a_hi, b_lo))


def _dot_split_rhs(a_bf, b):
    b_hi, b_mid = _split(b)
    b_lo = (b - b_hi.astype(F32) - b_mid.astype(F32)).astype(BF16)
    dot = functools.partial(jnp.dot, preferred_element_type=F32)
    return dot(a_bf, b_hi) + (dot(a_bf, b_mid) + dot(a_bf, b_lo))


def _head_sum(x, first_head):
    s_a = jnp.sum(jnp.where(first_head, x, 0.0), axis=-1, keepdims=True)
    s_b = jnp.sum(jnp.where(first_head, 0.0, x), axis=-1, keepdims=True)
    return jnp.where(first_head, s_a, s_b)


def _mod_kernel(c_ref, w_ref, b_ref, o_ref):
    c = c_ref[...]
    o_ref[...] = _dot(c * _sigmoid(c), w_ref[...]) + b_ref[...]


def _modulation(cvec, w_ada, b_ada):
    rows, d = cvec.shape
    n = w_ada.shape[1]
    tn = 1024
    return pl.pallas_call(
        _mod_kernel,
        out_shape=jax.ShapeDtypeStruct((rows, n), F32),
        grid=(n // tn,),
        in_specs=[pl.BlockSpec((rows, d), lambda j: (0, 0)),
                  pl.BlockSpec((d, tn), lambda j: (0, j)),
                  pl.BlockSpec((1, tn), lambda j: (0, j))],
        out_specs=pl.BlockSpec((rows, tn), lambda j: (0, j)),
        compiler_params=_params(("parallel",)),
        name="adaln_mod",
    )(cvec, w_ada, b_ada)


def _in_proj_kernel(x_ref, mod_ref, g_ref, w_ref, o_ref, h_ref):
    @pl.when(pl.program_id(1) == 0)
    def _():
        x = x_ref[...]
        y = x * lax.rsqrt(jnp.mean(x * x, axis=-1, keepdims=True) + EPS) * g_ref[...]
        h_ref[...] = (y * (1.0 + mod_ref[0, 1:2, :]) + mod_ref[0, 0:1, :]).astype(BF16)

    o_ref[...] = jnp.dot(h_ref[...], w_ref[...], preferred_element_type=F32)


def _in_proj(x, modm, g_pre, w_bf, group0, rows_per_group):
    n, d = x.shape
    p = w_bf.shape[1]
    tm, tn = 512, 1280
    grp = lambda i, j: (group0 + (i * tm) // rows_per_group, 0, 0)
    return pl.pallas_call(
        _in_proj_kernel,
        out_shape=jax.ShapeDtypeStruct((n, p), F32),
        grid=(n // tm, p // tn),
        in_specs=[pl.BlockSpec((tm, d), lambda i, j: (i, 0)),
                  pl.BlockSpec((1, 6, d), grp),
                  pl.BlockSpec((1, d), lambda i, j: (0, 0)),
                  pl.BlockSpec((d, tn), lambda i, j: (0, j))],
        out_specs=pl.BlockSpec((tm, tn), lambda i, j: (i, j)),
        scratch_shapes=[pltpu.VMEM((tm, d), BF16)],
        compiler_params=_params(("parallel", "arbitrary")),
        name="in_proj",
    )(x, modm, g_pre, w_bf)


def _shift_kernel(z_ref, mu_ref, o_ref, *, grid_mode, period):
    z = z_ref[...]
    rows = z.shape[0]
    t = lax.broadcasted_iota(I32, (rows, 1), 0) % period
    prev = jnp.where(t % (GRID_W if grid_mode else period) != 0, pltpu.roll(z, 1, 0), 0.0)
    nxt = jnp.where(t % (GRID_W if grid_mode else period) != (GRID_W if grid_mode else period) - 1,
                    pltpu.roll(z, rows - 1, 0), 0.0)
    if grid_mode:
        up = jnp.where(t >= GRID_W, pltpu.roll(z, GRID_W, 0), 0.0)
        down = jnp.where(t < period - GRID_W, pltpu.roll(z, rows - GRID_W, 0), 0.0)
        nb = 0.25 * (up + down + prev + nxt)
    else:
        nb = 0.5 * (prev + nxt)
    o_ref[...] = z + mu_ref[...] * (nb - z)


def _shift_mix(z, mu, n_shift, lora_block, grid_mode, period):
    n = z.shape[0]
    tr, tc = 2048, 256
    main_blocks = (n_shift // tc) - 1
    col = lambda i, j: (i, jnp.where(j < main_blocks, j, lora_block))
    return pl.pallas_call(
        functools.partial(_shift_kernel, grid_mode=grid_mode, period=period),
        out_shape=jax.ShapeDtypeStruct((n, n_shift), F32),
        grid=(n // tr, n_shift // tc),
        in_specs=[pl.BlockSpec((tr, tc), col),
                  pl.BlockSpec((1, tc), lambda i, j: (0, j))],
        out_specs=pl.BlockSpec((tr, tc), lambda i, j: (i, j)),
        compiler_params=_params(("parallel", "parallel")),
        name="token_shift",
    )(z, mu)


def _scan_chunks(chains):
    c = chains[0][0].shape[0]
    c2 = 2 * c
    n = len(chains)
    fwd = [ch[9] for ch in chains]
    first_head = lax.broadcasted_iota(I32, (1, LANES), 1) < HEAD_A
    row = lax.broadcasted_iota(I32, (c2, c2), 0)
    col = lax.broadcasted_iota(I32, (c2, c2), 1)
    eye = jnp.where(row == col, 1.0, 0.0)

    def stack(x):
        return jnp.concatenate([jnp.where(first_head, x, 0.0), jnp.where(first_head, 0.0, x)],
                               axis=0).astype(BF16)

    lhs, rhs, v2, total = [], [], [], []
    for r, k, v, logw, cl, a, k_k, k_a, _, forward in chains:
        kkr = k * k_k
        kk = kkr / jnp.maximum(jnp.sqrt(_head_sum(kkr * kkr, first_head)), 1e-12)
        kd = k * (1.0 + (a - 1.0) * k_a)
        gi = jnp.exp(-cl)
        lhs.append(jnp.concatenate([stack(kk * jnp.exp(cl - logw)), stack(r * jnp.exp(cl))], axis=0))
        rhs.append(jnp.concatenate([stack(kk * a * gi), stack(kd * gi)], axis=0))
        v2.append(stack(v))
        total.append(cl[c - 1:c, :] if forward else cl[0:1, :])

    res = [_dot_nt(lhs[i], rhs[i]) for i in range(n)]
    pr = [_dot_nt(lhs[i], chains[i][8]) for i in range(n)]
    strict = [(row > col) if f else (row < col) for f in fwd]
    incl = [(row >= col) if f else (row <= col) for f in fwd]
    ab = [jnp.where(strict[i], res[i][:c2, :c2], 0.0) for i in range(n)]
    ak = [jnp.where(strict[i], res[i][:c2, c2:], 0.0) for i in range(n)]
    gb_gk = [jnp.concatenate([jnp.where(incl[i], res[i][c2:, :c2], 0.0),
                              jnp.where(incl[i], res[i][c2:, c2:], 0.0)], axis=1).astype(BF16) for i in range(n)]
    akv = [_dot(ak[i], v2[i]) for i in range(n)]

    tinv = [eye - ab[i] for i in range(n)]
    pw = [-ab[i] for i in range(n)]
    for _ in range(c.bit_length() - 2):
        pw = [_dot(pw[i], pw[i]) for i in range(n)]
        tinv = [tinv[i] + _dot(tinv[i], pw[i]) for i in range(n)]

    u2 = [_dot(tinv[i], -pr[i][:c2, :] - akv[i]) for i in range(n)]
    uv = [jnp.concatenate([u2[i].astype(BF16), v2[i]], axis=0) for i in range(n)]
    y2 = [pr[i][c2:, :] + _dot(gb_gk[i], uv[i]) for i in range(n)]
    s_new = [(chains[i][8] + _dot_tn(uv[i], rhs[i])) * jnp.exp(total[i]) for i in range(n)]
    return [(y2[i][:c, :] + y2[i][c:, :], s_new[i]) for i in range(n)]


def _scan_direction(r_ref, k_ref, v_ref, l_ref, d, s_ref, w0, w2, a0, a2, kk, ka, forward):
    lora = l_ref[0]
    c = lora.shape[0]
    dir_rows = (lax.broadcasted_iota(I32, (LANES, 1), 0) // HEAD_A) == d
    wl = w0[d:d + 1, :] + _dot_x3(jnp.tanh(lora[:, :LANES]), jnp.where(dir_rows, w2[...], 0.0))
    logw = -DECAY_SCALE * _sigmoid(wl)
    a = _sigmoid(a0[d:d + 1, :] + _dot_x3(lora[:, LANES:], jnp.where(dir_rows, a2[...], 0.0)))
    ti = lax.broadcasted_iota(I32, (c, c), 0)
    tj = lax.broadcasted_iota(I32, (c, c), 1)
    cum = jnp.where((tj <= ti) if forward else (tj >= ti), 1.0, 0.0).astype(BF16)
    cl = _dot_split_rhs(cum, logw)
    chains = []
    for p in range(s_ref.shape[0]):
        sl = slice(p * LANES, (p + 1) * LANES)
        chains.append((r_ref[0, :, sl], k_ref[0, :, sl], v_ref[0, :, sl], logw[:, sl], cl[:, sl],
                       a[:, sl], kk[:, sl], ka[:, sl], s_ref[p], forward))
    return chains


def _scan_store(out, y_ref, s_ref):
    for p, (y, s_new) in enumerate(out):
        y_ref[0, :, p * LANES:(p + 1) * LANES] = y
        s_ref[p] = s_new


def _scan_kernel(rf, kf, vf, lf, rb, kb, vb, lb, w0, w2, a0, a2, kk, ka, s0f, s0b,
                 yf, yb, sf, sb, s2f, s2b, *, zero_init):
    c = pl.program_id(2)

    @pl.when(c == 0)
    def _():
        if zero_init:
            s2f[...] = jnp.zeros_like(s2f)
            s2b[...] = jnp.zeros_like(s2b)
        else:
            s2f[...] = s0f[0]
            s2b[...] = s0b[0]

    chains_f = _scan_direction(rf, kf, vf, lf, 0, s2f, w0, w2, a0, a2, kk, ka, True)
    chains_b = _scan_direction(rb, kb, vb, lb, 1, s2b, w0, w2, a0, a2, kk, ka, False)
    out = _scan_chunks(chains_f + chains_b)
    _scan_store(out[:len(chains_f)], yf, s2f)
    _scan_store(out[len(chains_f):], yb, s2b)

    @pl.when(c == pl.num_programs(2) - 1)
    def _():
        sf[0] = s2f[...]
        sb[0] = s2b[...]


def _rwkv_scan(zs, w0, w2r, a0, a2r, k_k, k_a, s0f, s0b, zero_init):
    b, t, _ = zs.shape
    d_a = w0.shape[1]
    pp = SCAN_PAIRS
    wide = pp * LANES
    groups = d_a // wide
    c = SCAN_CHUNK
    nc = t // c
    lora_blk = (4 * d_a) // (2 * LANES)
    fw = lambda off: (lambda i, q, j: (i, j, off + q))
    bw = lambda off: (lambda i, q, j: (i, nc - 1 - j, off + q))
    par = lambda i, q, j: (0, q)
    st = lambda i, q, j: (i, q, 0, 0)
    blk = (1, c, wide)
    lblk = (1, c, 2 * LANES)
    sblk = (1, pp, LANES, LANES)
    yshape = jax.ShapeDtypeStruct((b, t, d_a), F32)
    sshape = jax.ShapeDtypeStruct((b, d_a // LANES, LANES, LANES), F32)
    return pl.pallas_call(
        functools.partial(_scan_kernel, zero_init=zero_init),
        out_shape=(yshape, yshape, sshape, sshape),
        grid=(b, groups, nc),
        in_specs=[pl.BlockSpec(blk, fw(0)), pl.BlockSpec(blk, fw(groups)), pl.BlockSpec(blk, fw(2 * groups)),
                  pl.BlockSpec(lblk, lambda i, q, j: (i, j, lora_blk)),
                  pl.BlockSpec(blk, bw(0)), pl.BlockSpec(blk, bw(groups)), pl.BlockSpec(blk, bw(2 * groups)),
                  pl.BlockSpec(lblk, lambda i, q, j: (i, nc - 1 - j, lora_blk)),
                  pl.BlockSpec((2, wide), par), pl.BlockSpec((LANES, wide), par),
                  pl.BlockSpec((2, wide), par), pl.BlockSpec((LANES, wide), par),
                  pl.BlockSpec((1, wide), par), pl.BlockSpec((1, wide), par),
                  pl.BlockSpec(sblk, st), pl.BlockSpec(sblk, st)],
        out_specs=(pl.BlockSpec(blk, fw(0)), pl.BlockSpec(blk, bw(0)),
                   pl.BlockSpec(sblk, st), pl.BlockSpec(sblk, st)),
        scratch_shapes=[pltpu.VMEM((pp, LANES, LANES), F32), pltpu.VMEM((pp, LANES, LANES), F32)],
        compiler_params=_params(("parallel", "parallel", "arbitrary")),
        name="rwkv7_scan",
    )(zs, zs, zs, zs, zs, zs, zs, zs, w0, w2r, a0, a2r, k_k, k_a, s0f, s0b)


def _post_kernel(yf, yb, r, k, v, g, lora, u, vg, a0, a2, ka, rk, lxw, lxb, lvw, lvb, ws, bst, o_ref):
    tm = yf.shape[0]
    d_a = yf.shape[1]
    first_head = lax.broadcasted_iota(I32, (1, LANES), 1) < HEAD_A
    dir_row = lax.broadcasted_iota(I32, (LANES, 1), 0) // HEAD_A
    la = lora[:, LANES:]
    a_sum = jnp.zeros((tm, d_a), F32)
    for d in range(2):
        a_sum = a_sum + _sigmoid(a0[d:d + 1, :] + _dot_x3(la, jnp.where(dir_row == d, a2[...], 0.0)))
    rkk = r[...] * k[...] * (2.0 + (a_sum - 2.0) * ka[...]) * rk[...]
    y = yf[...] + yb[...]
    inv = 1.0 / HEAD_A
    for j in range(d_a // LANES):
        sl = slice(j * LANES, (j + 1) * LANES)
        yj = y[:, sl]
        mu = _head_sum(yj, first_head) * inv
        dl = yj - mu
        var = _head_sum(dl * dl, first_head) * inv
        yn = dl * lax.rsqrt(var + GN_EPS) * lxw[:, sl] + lxb[:, sl]
        bonus = _head_sum(rkk[:, sl], first_head) * v[:, sl]
        o_ref[:, sl] = ((yn + bonus) * _sigmoid(g[:, sl])).astype(BF16)

    uu = _gelu(u[...])
    vv = _gelu(vg[...])
    mu = jnp.mean(vv, axis=-1, keepdims=True)
    dv = vv - mu
    vn = dv * lax.rsqrt(jnp.mean(dv * dv, axis=-1, keepdims=True) + EPS) * lvw[...] + lvb[...]
    for ch in range(tm // GMLP_CHUNK):
        rows = slice(ch * GMLP_CHUNK, (ch + 1) * GMLP_CHUNK)
        for h in range(ws.shape[0]):
            cols = slice(h * LANES, (h + 1) * LANES)
            sp = _dot(ws[h], vn[rows, cols]) + bst[:, h:h + 1]
            o_ref[rows, d_a + h * LANES:d_a + (h + 1) * LANES] = (uu[rows, cols] * sp).astype(BF16)


def _post_mix(yf, yb, zs, z, a0, a2r, k_a, r_k, lxw, lxb, lvw, lvb, w_s, b_st):
    n, d_a = yf.shape
    tm = 256
    wide = lambda j: (lambda i: (i, j))
    full = lambda shape: pl.BlockSpec(shape, lambda i: (0,) * len(shape))
    lora_blk = (4 * d_a) // (2 * LANES)
    return pl.pallas_call(
        _post_kernel,
        out_shape=jax.ShapeDtypeStruct((n, 2 * d_a), BF16),
        grid=(n // tm,),
        in_specs=[pl.BlockSpec((tm, d_a), wide(0)), pl.BlockSpec((tm, d_a), wide(0)),
                  pl.BlockSpec((tm, d_a), wide(0)), pl.BlockSpec((tm, d_a), wide(1)),
                  pl.BlockSpec((tm, d_a), wide(2)), pl.BlockSpec((tm, d_a), wide(3)),
                  pl.BlockSpec((tm, 2 * LANES), wide(lora_blk)),
                  pl.BlockSpec((tm, d_a), wide(4)), pl.BlockSpec((tm, d_a), wide(5)),
                  full((2, d_a)), full((LANES, d_a)), full((1, d_a)), full((1, d_a)),
                  full((1, d_a)), full((1, d_a)), full((1, d_a)), full((1, d_a)),
                  full(w_s.shape), full(b_st.shape)],
        out_specs=pl.BlockSpec((tm, 2 * d_a), wide(0)),
        compiler_params=_params(("parallel",)),
        name="mix_post",
    )(yf, yb, zs, zs, zs, zs, zs, z, z, a0, a2r, k_a, r_k, lxw, lxb, lvw, lvb, w_s, b_st)


def _rms(x, g):
    return x * lax.rsqrt(jnp.mean(x * x, axis=-1, keepdims=True) + EPS) * g


def _out_proj_kernel(a_ref, w_ref, x_ref, mod_ref, gpost_ref, gpre_ref, x1_ref, h2_ref):
    o = jnp.dot(a_ref[...], w_ref[...], preferred_element_type=F32)
    x1 = x_ref[...] + mod_ref[0, 2:3, :] * _rms(o, gpost_ref[...])
    x1_ref[...] = x1
    h2_ref[...] = _rms(x1, gpre_ref[...]) * (1.0 + mod_ref[0, 4:5, :]) + mod_ref[0, 3:4, :]


def _out_proj(yag, w_bf, x, modm, g_post, g_pre2, group0, rows_per_group):
    n, d = x.shape
    tm = 256
    grp = lambda i: (group0 + (i * tm) // rows_per_group, 0, 0)
    row = lambda i: (i, 0)
    fix = lambda i: (0, 0)
    shp = jax.ShapeDtypeStruct((n, d), F32)
    return pl.pallas_call(
        _out_proj_kernel,
        out_shape=(shp, shp),
        grid=(n // tm,),
        in_specs=[pl.BlockSpec((tm, yag.shape[1]), row), pl.BlockSpec(w_bf.shape, fix),
                  pl.BlockSpec((tm, d), row), pl.BlockSpec((1, 6, d), grp),
                  pl.BlockSpec((1, d), fix), pl.BlockSpec((1, d), fix)],
        out_specs=(pl.BlockSpec((tm, d), row), pl.BlockSpec((tm, d), row)),
        compiler_params=_params(("parallel",)),
        name="out_proj",
    )(yag, w_bf, x, modm, g_post, g_pre2)


def _query_kernel(h_ref, wq_ref, sk_ref, o_ref):
    q = jnp.dot(h_ref[...].astype(BF16), wq_ref[...], preferred_element_type=F32)
    for g in range(sk_ref.shape[0]):
        o_ref[g * N_KEYS:(g + 1) * N_KEYS, :] = _dot_nt(sk_ref[g], q[:, g * LANES:(g + 1) * LANES])


def _query_scores(h2, wq_bf, sk):
    n, d = h2.shape
    tm = 256
    groups = sk.shape[0]
    return pl.pallas_call(
        _query_kernel,
        out_shape=jax.ShapeDtypeStruct((groups * N_KEYS, n), F32),
        grid=(n // tm,),
        in_specs=[pl.BlockSpec((tm, d), lambda i: (i, 0)),
                  pl.BlockSpec(wq_bf.shape, lambda i: (0, 0)),
                  pl.BlockSpec(sk.shape, lambda i: (0, 0, 0))],
        out_specs=pl.BlockSpec((groups * N_KEYS, tm), lambda i: (0, i)),
        compiler_params=_params(("parallel",)),
        name="peer_query",
    )(h2, wq_bf, sk)


def _top16(s, pos=None):
    if pos is None:
        pos = lax.broadcasted_iota(I32, s.shape, 0)
    big = jnp.int32(2 ** 30)
    vals, idxs = [], []
    for _ in range(TOPK):
        m = jnp.max(s, axis=0, keepdims=True)
        i = jnp.min(jnp.where(s == m, pos, big), axis=0, keepdims=True)
        vals.append(m)
        idxs.append(i)
        s = jnp.where(pos == i, -jnp.inf, s)
    return jnp.concatenate(vals, axis=0), jnp.concatenate(idxs, axis=0)


def _pair_candidates(v1, v2):
    r8 = lax.broadcasted_iota(I32, (8, 1), 0)
    r16 = lax.broadcasted_iota(I32, (TOPK, 1), 0)
    ninf = -jnp.inf
    blocks = [
        (v1[0:1, :] + v2, r16),
        (v1[1:2, :] + v2[0:8, :], TOPK + r8),
        (jnp.where(r8 < 5, v1[2:3, :] + v2[0:8, :], ninf), 2 * TOPK + r8),
        (jnp.where(r8 < 4, v1[3:4, :] + v2[0:8, :], ninf), 3 * TOPK + r8),
        (jnp.where(r16 >= 4, v1 + v2[0:1, :], ninf), r16 * TOPK),
        (jnp.where(r8 >= 4, v1[0:8, :] + v2[1:2, :], ninf), r8 * TOPK + 1),
        (jnp.where(r8 == 4, v1[0:8, :] + v2[2:3, :], ninf), r8 * TOPK + 2),
    ]
    cand = jnp.concatenate([b for b, _ in blocks], axis=0)
    pos = jnp.concatenate([p for _, p in blocks], axis=0)
    return cand, jnp.broadcast_to(pos, cand.shape)


def _topk_kernel(s_ref, idx_ref, gate_ref):
    def head(h, carry):
        base = pl.multiple_of(h * (2 * N_KEYS), 2 * N_KEYS)
        v1, i1 = _top16(s_ref[pl.ds(base, N_KEYS), :])
        v2, i2 = _top16(s_ref[pl.ds(base + N_KEYS, N_KEYS), :])
        top_s, pos = _top16(*_pair_candidates(v1, v2))
        pi = pos // TOPK
        pj = pos % TOPK
        e1 = jnp.zeros_like(pos)
        e2 = jnp.zeros_like(pos)
        for i in range(TOPK):
            e1 = e1 + jnp.where(pi == i, i1[i:i + 1, :], 0)
            e2 = e2 + jnp.where(pj == i, i2[i:i + 1, :], 0)
        ex = jnp.exp(top_s - top_s[0:1, :])
        out = pl.multiple_of(h * TOPK, TOPK)
        idx_ref[pl.ds(out, TOPK), :] = e1 * N_KEYS + e2
        gate_ref[pl.ds(out, TOPK), :] = ex / jnp.sum(ex, axis=0, keepdims=True)
        return carry

    lax.fori_loop(0, PK_HEADS, head, 0)


def _peer_topk(scores):
    rows, n = scores.shape
    tt = 256
    out_rows = PK_HEADS * TOPK
    return pl.pallas_call(
        _topk_kernel,
        out_shape=(jax.ShapeDtypeStruct((out_rows, n), I32), jax.ShapeDtypeStruct((out_rows, n), F32)),
        grid=(n // tt,),
        in_specs=[pl.BlockSpec((rows, tt), lambda i: (0, i))],
        out_specs=(pl.BlockSpec((out_rows, tt), lambda i: (0, i)),
                   pl.BlockSpec((out_rows, tt), lambda i: (0, i))),
        compiler_params=_params(("parallel",)),
        name="peer_topk",
    )(scores)


SC_CORES = 2
SC_SUBCORES = 16
SC_LANES = 16
PEER_GROUP = 16
PEER_BLOCK = 8
U_MASK = -65536


def _pack_experts(eu, ev):
    hi = lax.bitcast_convert_type(eu.astype(BF16), jnp.uint16).astype(jnp.uint32)
    lo = lax.bitcast_convert_type(ev.astype(BF16), jnp.uint16).astype(jnp.uint32)
    return lax.bitcast_convert_type((hi << 16) | lo, I32)


def _unpack_u(w):
    return lax.bitcast_convert_type(w & U_MASK, F32)


def _unpack_v(w):
    return lax.bitcast_convert_type(w << 16, F32)


def _peer_experts_sc(idx, x, gates, table):
    n, d = x.shape
    n_sel = idx.shape[1]
    per_w = n // (SC_CORES * SC_SUBCORES)
    L, G, TB = SC_LANES, PEER_GROUP, PEER_BLOCK
    n_groups = n_sel // G
    n_ch = d // L
    mesh = plsc.VectorSubcoreMesh(core_axis_name="c", subcore_axis_name="s")

    @functools.partial(
        pl.kernel, mesh=mesh, out_type=jax.ShapeDtypeStruct((n, d), F32),
        compiler_params=pltpu.CompilerParams(needs_layout_passes=False),
        scratch_types=[pltpu.VMEM((TB, n_sel), I32), pltpu.VMEM((TB, n_sel), F32),
                       pltpu.VMEM((2, d), F32), pltpu.VMEM((2, d), F32),
                       pltpu.VMEM((2, G, d), I32), pltpu.VMEM((G, 2 * L), F32), pltpu.VMEM((2 * L,), F32),
                       pltpu.SemaphoreType.DMA((2,)), pltpu.SemaphoreType.DMA((2,)), pltpu.SemaphoreType.DMA((2,))],
        name="peer_experts_sc")
    def k(idx_hbm, x_hbm, g_hbm, tab_hbm, o_hbm, idx_v, gate_v, x_v, o_v, buf, acc_v, coef_v, sem, xsem, osem):
        base = (lax.axis_index("s") * SC_CORES + lax.axis_index("c")) * per_w
        lane = lax.iota(I32, L)
        zero = jnp.zeros((L,), F32)

        def gather(t, g, slot):
            return pltpu.make_async_copy(tab_hbm.at[idx_v.at[t, pl.ds(g * G, G)]], buf.at[slot], sem.at[slot])

        def x_copy(j, par):
            return pltpu.make_async_copy(x_hbm.at[base + j], x_v.at[par], xsem.at[par])

        def o_copy(j, par):
            return pltpu.make_async_copy(o_v.at[par], o_hbm.at[base + j], osem.at[par])

        x_copy(0, 0).start()

        @pl.loop(0, per_w // TB)
        def _(blk):
            tok0 = base + blk * TB
            pltpu.sync_copy(idx_hbm.at[pl.ds(tok0, TB)], idx_v)
            pltpu.sync_copy(g_hbm.at[pl.ds(tok0, TB)], gate_v)

            @pl.loop(0, TB)
            def _(t):
                j = blk * TB + t
                par = j % 2
                gather(t, 0, 0).start()

                @pl.when(j + 1 < per_w)
                def _():
                    x_copy(j + 1, 1 - par).start()

                x_copy(j, par).wait()

                @pl.when(j >= 2)
                def _():
                    o_copy(j - 2, par).wait()

                for g in range(n_groups):
                    slot = g % 2
                    if g + 1 < n_groups:
                        gather(t, g + 1, 1 - slot).start()
                    gather(t, g, slot).wait()

                    def u_body(ch, accs):
                        xv = x_v[par, pl.ds(ch * L, L)]
                        return tuple(accs[e] + _unpack_u(buf[slot, e, pl.ds(ch * L, L)]) * xv for e in range(G))

                    accs = plsc.parallel_loop(0, n_ch, carry=tuple(zero for _ in range(G)))(u_body)
                    for e in range(G):
                        acc_v[e, pl.ds(L, L)] = accs[e]
                    act = zero
                    for l in range(L):
                        act = act + plsc.load_gather(acc_v, [lane, jnp.full((L,), L + l, I32)])
                    y = 0.7978845608028654 * (act + 0.044715 * (act * act * act))
                    coef_v[pl.ds(L, L)] = act / (1.0 + jnp.exp(-2.0 * y)) * gate_v[t, pl.ds(g * G, G)]
                    cs = [plsc.load_gather(coef_v, [jnp.full((L,), L + e, I32)]) for e in range(G)]

                    @plsc.parallel_loop(0, n_ch)
                    def _(ch):
                        o = zero if g == 0 else o_v[par, pl.ds(ch * L, L)]
                        for e in range(G):
                            o = o + cs[e] * _unpack_v(buf[slot, e, pl.ds(ch * L, L)])
                        o_v[par, pl.ds(ch * L, L)] = o

                o_copy(j, par).start()

        o_copy(per_w - 2, 0).wait()
        o_copy(per_w - 1, 1).wait()

    return k(idx, x, gates, table)


PEER_TC_TOKENS = 64
CTX_PARTS = 4
PEER_TC_SLOTS = 3
PEER_TC_CTX = 512
PEER_TC_LATENT = 3584


def _peer_tc_kernel(idx_ref, x_ref, g_ref, uv_ref, o_ref, buf, sem):
    slots, rows, n_sel = buf.shape[0], buf.shape[1], buf.shape[2]
    tokens = g_ref.shape[0]
    ahead = slots - 1

    def issue(t):
        slot = t % slots

        def body(e, carry):
            src = pl.multiple_of(idx_ref[t, e] * rows, rows)
            pltpu.make_async_copy(uv_ref.at[pl.ds(src, rows), :], buf.at[slot, :, e, :], sem.at[slot]).start()
            return carry
        lax.fori_loop(0, n_sel, body, 0, unroll=8)

    def wait(slot):
        pltpu.make_async_copy(buf.at[(slot + 1) % slots], buf.at[slot], sem.at[slot]).wait()

    for t0 in range(ahead):
        issue(t0)
    g_t = g_ref[...].T
    g_hi, g_mid = _split(g_t)
    g_lo = (g_t - g_hi.astype(F32) - g_mid.astype(F32)).astype(BF16)
    tok_iota = lax.broadcasted_iota(I32, (tokens, LANES), 0)
    dot = functools.partial(jnp.dot, preferred_element_type=F32)

    def token(t, carry):
        slot = t % slots

        @pl.when(t + ahead < tokens)
        def _():
            issue(t + ahead)

        wait(slot)
        pick = jnp.where(tok_iota == t, 1.0, 0.0).astype(BF16)
        gate = dot(g_hi, pick) + (dot(g_mid, pick) + dot(g_lo, pick))
        base = pl.multiple_of(t * rows, rows)
        xt = x_ref[pl.ds(base, rows), :]
        acc = _unpack_u(buf[slot, 0]) * xt[0:1, :]
        for c in range(1, rows):
            acc = acc + _unpack_u(buf[slot, c]) * xt[c:c + 1, :]
        act = jnp.sum(acc, axis=-1, keepdims=True)
        coef = _gelu(act) * gate
        o_ref[pl.ds(base, rows), :] = jnp.concatenate(
            [jnp.sum(coef * _unpack_v(buf[slot, c]), axis=0, keepdims=True) for c in range(rows)], axis=0)
        return carry

    lax.fori_loop(0, tokens, token, 0)


def _peer_experts_tc(idx, x, gates, table_rows):
    n, d = x.shape
    n_sel = idx.shape[1]
    tt = PEER_TC_TOKENS
    rows = d // LANES
    out = pl.pallas_call(
        _peer_tc_kernel,
        out_shape=jax.ShapeDtypeStruct((n * rows, LANES), F32),
        grid=(n // tt,),
        in_specs=[pl.BlockSpec((tt, n_sel), lambda i: (i, 0), memory_space=pltpu.SMEM),
                  pl.BlockSpec((tt * rows, LANES), lambda i: (i, 0)),
                  pl.BlockSpec((tt, n_sel), lambda i: (i, 0)),
                  pl.BlockSpec(memory_space=pl.ANY)],
        out_specs=pl.BlockSpec((tt * rows, LANES), lambda i: (i, 0)),
        scratch_shapes=[pltpu.VMEM((PEER_TC_SLOTS, rows, n_sel, LANES), I32),
                        pltpu.SemaphoreType.DMA((PEER_TC_SLOTS,))],
        compiler_params=pltpu.CompilerParams(dimension_semantics=("arbitrary",),
                                             vmem_limit_bytes=VMEM_LIMIT, disable_bounds_checks=True),
        name="peer_experts_tc",
    )(idx, x.reshape(n * rows, LANES), gates, table_rows)
    return out.reshape(n, d)


def _peer_experts(idx, x, gates, table, table_rows, m):
    out_tc = _peer_experts_tc(idx[:m], x[:m], gates[:m], table_rows)
    out_sc = _peer_experts_sc(idx[m:], x[m:], gates[m:], table)
    return jnp.concatenate([out_tc, out_sc], axis=0)


def _final_kernel(x1_ref, p_ref, mod_ref, g_ref, o_ref):
    o_ref[...] = x1_ref[...] + mod_ref[0, 5:6, :] * _rms(p_ref[...], g_ref[...])


def _final(x1, peer, modm, g_post2, group0, rows_per_group):
    n, d = x1.shape
    tm = 512
    row = lambda i: (i, 0)
    return pl.pallas_call(
        _final_kernel,
        out_shape=jax.ShapeDtypeStruct((n, d), F32),
        grid=(n // tm,),
        in_specs=[pl.BlockSpec((tm, d), row), pl.BlockSpec((tm, d), row),
                  pl.BlockSpec((1, 6, d), lambda i: (group0 + (i * tm) // rows_per_group, 0, 0)),
                  pl.BlockSpec((1, d), lambda i: (0, 0))],
        out_specs=pl.BlockSpec((tm, d), row),
        compiler_params=_params(("parallel",)),
        name="final_residual",
    )(x1, peer, modm, g_post2)


def _block_diag_state(s):
    b, h = s.shape[:2]
    s = s.reshape(b, h // 2, 2, HEAD_A, HEAD_A)
    z = jnp.zeros_like(s[:, :, 0])
    return jnp.concatenate([jnp.concatenate([s[:, :, 0], z], axis=-1),
                            jnp.concatenate([z, s[:, :, 1]], axis=-1)], axis=-2)


def _head_states(s2):
    b, p = s2.shape[:2]
    return jnp.stack([s2[:, :, :HEAD_A, :HEAD_A], s2[:, :, HEAD_A:, HEAD_A:]],
                     axis=2).reshape(b, 2 * p, HEAD_A, HEAD_A)


def _layer(x3, modm, group0, grid_mode, s0f, s0b, w, tc_tokens):
    b, t, d = x3.shape
    n = b * t
    x = x3.reshape(n, d)
    rows_per_group = t if grid_mode else n
    z = _in_proj(x, modm, w["g_pre1"], w["w_in"], group0, rows_per_group)
    zs = _shift_mix(z, w["mu_shift"], w["n_shift"], w["lora_block"], grid_mode, t)
    zero_init = s0f is None
    if zero_init:
        s0f = s0b = jnp.zeros((b, w["w0"].shape[1] // LANES, LANES, LANES), F32)
    yf, yb, sf, sb = _rwkv_scan(zs.reshape(b, t, -1), w["w0"], w["w2"], w["a0"], w["a2"], w["k_k"], w["k_a"],
                                s0f, s0b, zero_init)
    d_a = yf.shape[-1]
    yag = _post_mix(yf.reshape(n, d_a), yb.reshape(n, d_a), zs, z, w["a0"], w["a2"], w["k_a"], w["r_k"],
                    w["ln_x_w"], w["ln_x_b"], w["ln_v_w"], w["ln_v_b"], w["w_s"], w["b_st"])
    x1, h2 = _out_proj(yag, w["w_out"], x, modm, w["g_post1"], w["g_pre2"], group0, rows_per_group)
    scores = _query_scores(h2, w["w_query"], w["sub_keys"])
    idx_t, gates_t = _peer_topk(scores)
    peer = _peer_experts(idx_t.T, h2, gates_t.T, w["experts"], w["expert_rows"], tc_tokens)
    out = _final(x1, peer, modm, w["g_post2"], group0, rows_per_group)
    return out.reshape(b, t, d), sf, sb


def kernel(x_prompt, x_sample, c, state_fwd, state_bwd, c_ctx, w_ada, b_ada, g_pre1, g_post1, g_pre2, g_post2,
           w_in, mu_shift, w0, w2, a0, a2, k_k, k_a, r_k, ln_x_w, ln_x_b, ln_v_w, ln_v_b, w_s, b_s, w_out,
           w_query, sub_keys, expert_u, expert_v):
    depth = w_in.shape[0]
    d = x_prompt.shape[-1]
    d_a = w0.shape[-1]
    n_shift = mu_shift.shape[-1]
    dec_b = x_sample.shape[0]
    cvec = jnp.concatenate([c_ctx[None], c, jnp.zeros((8 - 1 - dec_b, d), F32)], axis=0)
    xp, xs = x_prompt, x_sample
    new_f, new_b = [], []
    for l in range(depth):
        wl_in = w_in[l]
        row = lambda a: a[l].reshape(1, -1)
        w = {
            "w_in": jnp.concatenate([wl_in[:, :4 * d_a], wl_in[:, n_shift:], wl_in[:, 4 * d_a:n_shift]],
                                    axis=1).astype(BF16),
            "n_shift": n_shift,
            "lora_block": (wl_in.shape[1] - (n_shift - 4 * d_a)) // (2 * LANES),
            "mu_shift": row(mu_shift),
            "g_pre1": row(g_pre1), "g_post1": row(g_post1), "g_pre2": row(g_pre2), "g_post2": row(g_post2),
            "w0": w0[l], "w2": w2[l].reshape(-1, d_a), "a0": a0[l], "a2": a2[l].reshape(-1, d_a),
            "k_k": row(k_k), "k_a": row(k_a), "r_k": row(r_k),
            "ln_x_w": row(ln_x_w), "ln_x_b": row(ln_x_b), "ln_v_w": row(ln_v_w), "ln_v_b": row(ln_v_b),
            "w_s": w_s[l].astype(BF16), "b_st": b_s[l].T,
            "w_out": w_out[l].astype(BF16), "w_query": w_query[l].astype(BF16),
            "sub_keys": jnp.swapaxes(sub_keys[l], 0, 1).reshape(-1, N_KEYS, sub_keys.shape[-1]).astype(BF16),
            "experts": _pack_experts(expert_u[l], expert_v[l]),
        }
        w["expert_rows"] = w["experts"].reshape(-1, LANES)
        modm = _modulation(cvec, w_ada[l], b_ada[l].reshape(1, -1)).reshape(8, 6, d)
        piece = xp.shape[0] // CTX_PARTS
        parts = [_layer(xp[i * piece:(i + 1) * piece], modm, 0, False, None, None, w, PEER_TC_CTX)
                 for i in range(CTX_PARTS)]
        xp = jnp.concatenate([p[0] for p in parts], axis=0)
        new_f.append(_head_states(jnp.concatenate([p[1] for p in parts], axis=0)))
        new_b.append(_head_states(jnp.concatenate([p[2] for p in parts], axis=0)))
        xs, _, _ = _layer(xs, modm, 1, True, _block_diag_state(state_fwd[:, l]),
                          _block_diag_state(state_bwd[:, l]), w, PEER_TC_LATENT)
    return (xp, xs, jnp.stack(new_f, axis=1), jnp.stack(new_b, axis=1))
```

```python
import functools

import jax
import jax.numpy as jnp
from jax import lax
from jax.experimental import pallas as pl
from jax.experimental.pallas import tpu as pltpu
from jax.experimental.pallas import tpu_sc as plsc

F32 = jnp.float32
BF16 = jnp.bfloat16
I32 = jnp.int32

EPS = 1e-6
GN_EPS = 64e-5
HEAD_A = 64
LANES = 128
GRID_W = 64
GMLP_CHUNK = 128
PK_HEADS = 8
N_KEYS = 128
TOPK = 16
SCAN_CHUNK = 64
SCAN_PAIRS = 8
DECAY_SCALE = 0.6065306597126334
VMEM_LIMIT = 48 * 1024 * 1024


def _params(sem):
    return pltpu.CompilerParams(dimension_semantics=sem, vmem_limit_bytes=VMEM_LIMIT)


def _sigmoid(x):
    return 1.0 / (1.0 + jnp.exp(-x))


def _gelu(x):
    return 0.5 * x * (1.0 + jnp.tanh(0.7978845608028654 * (x + 0.044715 * (x * x * x))))


def _dot(a, b):
    return jnp.dot(a.astype(BF16), b.astype(BF16), preferred_element_type=F32)


def _dot_nt(a, b):
    return lax.dot_general(a.astype(BF16), b.astype(BF16), (((1,), (1,)), ((), ())),
                           preferred_element_type=F32)


def _dot_tn(a, b):
    return lax.dot_general(a.astype(BF16), b.astype(BF16), (((0,), (0,)), ((), ())),
                           preferred_element_type=F32)


def _split(x):
    hi = x.astype(BF16)
    return hi, (x - hi.astype(F32)).astype(BF16)


def _dot_x3(a, b):
    a_hi, a_lo = _split(a)
    b_hi, b_lo = _split(b)
    dot = functools.partial(jnp.dot, preferred_element_type=F32)
    return dot(a_hi, b_hi) + (dot(a_lo, b_hi) + dot(a_hi, b_lo))


def _dot_split_rhs(a_bf, b):
    b_hi, b_mid = _split(b)
    b_lo = (b - b_hi.astype(F32) - b_mid.astype(F32)).astype(BF16)
    dot = functools.partial(jnp.dot, preferred_element_type=F32)
    return dot(a_bf, b_hi) + (dot(a_bf, b_mid) + dot(a_bf, b_lo))


def _head_sum(x, first_head):
    s_a = jnp.sum(jnp.where(first_head, x, 0.0), axis=-1, keepdims=True)
    s_b = jnp.sum(jnp.where(first_head, 0.0, x), axis=-1, keepdims=True)
    return jnp.where(first_head, s_a, s_b)


def _mod_kernel(c_ref, w_ref, b_ref, o_ref):
    c = c_ref[...]
    o_ref[...] = _dot(c * _sigmoid(c), w_ref[...]) + b_ref[...]


def _modulation(cvec, w_ada, b_ada):
    rows, d = cvec.shape
    n = w_ada.shape[1]
    tn = 1024
    return pl.pallas_call(
        _mod_kernel,
        out_shape=jax.ShapeDtypeStruct((rows, n), F32),
        grid=(n // tn,),
        in_specs=[pl.BlockSpec((rows, d), lambda j: (0, 0)),
                  pl.BlockSpec((d, tn), lambda j: (0, j)),
                  pl.BlockSpec((1, tn), lambda j: (0, j))],
        out_specs=pl.BlockSpec((rows, tn), lambda j: (0, j)),
        compiler_params=_params(("parallel",)),
        name="adaln_mod",
    )(cvec, w_ada, b_ada)


def _in_proj_kernel(x_ref, mod_ref, g_ref, w_ref, o_ref, h_ref):
    @pl.when(pl.program_id(1) == 0)
    def _():
        x = x_ref[...]
        y = x * lax.rsqrt(jnp.mean(x * x, axis=-1, keepdims=True) + EPS) * g_ref[...]
        h_ref[...] = (y * (1.0 + mod_ref[0, 1:2, :]) + mod_ref[0, 0:1, :]).astype(BF16)

    o_ref[...] = jnp.dot(h_ref[...], w_ref[...], preferred_element_type=F32)


def _in_proj(x, modm, g_pre, w_bf, group0, rows_per_group):
    n, d = x.shape
    p = w_bf.shape[1]
    tm, tn = 512, 1280
    grp = lambda i, j: (group0 + (i * tm) // rows_per_group, 0, 0)
    return pl.pallas_call(
        _in_proj_kernel,
        out_shape=jax.ShapeDtypeStruct((n, p), F32),
        grid=(n // tm, p // tn),
        in_specs=[pl.BlockSpec((tm, d), lambda i, j: (i, 0)),
                  pl.BlockSpec((1, 6, d), grp),
                  pl.BlockSpec((1, d), lambda i, j: (0, 0)),
                  pl.BlockSpec((d, tn), lambda i, j: (0, j))],
        out_specs=pl.BlockSpec((tm, tn), lambda i, j: (i, j)),
        scratch_shapes=[pltpu.VMEM((tm, d), BF16)],
        compiler_params=_params(("parallel", "arbitrary")),
        name="in_proj",
    )(x, modm, g_pre, w_bf)


def _shift_kernel(z_ref, mu_ref, o_ref, *, grid_mode, period):
    z = z_ref[...]
    rows = z.shape[0]
    t = lax.broadcasted_iota(I32, (rows, 1), 0) % period
    prev = jnp.where(t % (GRID_W if grid_mode else period) != 0, pltpu.roll(z, 1, 0), 0.0)
    nxt = jnp.where(t % (GRID_W if grid_mode else period) != (GRID_W if grid_mode else period) - 1,
                    pltpu.roll(z, rows - 1, 0), 0.0)
    if grid_mode:
        up = jnp.where(t >= GRID_W, pltpu.roll(z, GRID_W, 0), 0.0)
        down = jnp.where(t < period - GRID_W, pltpu.roll(z, rows - GRID_W, 0), 0.0)
        nb = 0.25 * (up + down + prev + nxt)
    else:
        nb = 0.5 * (prev + nxt)
    o_ref[...] = z + mu_ref[...] * (nb - z)


def _shift_mix(z, mu, n_shift, lora_block, grid_mode, period):
    n = z.shape[0]
    tr, tc = 2048, 256
    main_blocks = (n_shift // tc) - 1
    col = lambda i, j: (i, jnp.where(j < main_blocks, j, lora_block))
    return pl.pallas_call(
        functools.partial(_shift_kernel, grid_mode=grid_mode, period=period),
        out_shape=jax.ShapeDtypeStruct((n, n_shift), F32),
        grid=(n // tr, n_shift // tc),
        in_specs=[pl.BlockSpec((tr, tc), col),
                  pl.BlockSpec((1, tc), lambda i, j: (0, j))],
        out_specs=pl.BlockSpec((tr, tc), lambda i, j: (i, j)),
        compiler_params=_params(("parallel", "parallel")),
        name="token_shift",
    )(z, mu)


def _scan_chunks(chains):
    c = chains[0][0].shape[0]
    c2 = 2 * c
    n = len(chains)
    fwd = [ch[9] for ch in chains]
    first_head = lax.broadcasted_iota(I32, (1, LANES), 1) < HEAD_A
    row = lax.broadcasted_iota(I32, (c2, c2), 0)
    col = lax.broadcasted_iota(I32, (c2, c2), 1)
    eye = jnp.where(row == col, 1.0, 0.0)

    def stack(x):
        return jnp.concatenate([jnp.where(first_head, x, 0.0), jnp.where(first_head, 0.0, x)],
                               axis=0).astype(BF16)

    lhs, rhs, v2, total = [], [], [], []
    for r, k, v, logw, cl, a, k_k, k_a, _, forward in chains:
        kkr = k * k_k
        kk = kkr / jnp.maximum(jnp.sqrt(_head_sum(kkr * kkr, first_head)), 1e-12)
        kd = k * (1.0 + (a - 1.0) * k_a)
        gi = jnp.exp(-cl)
        lhs.append(jnp.concatenate([stack(kk * jnp.exp(cl - logw)), stack(r * jnp.exp(cl))], axis=0))
        rhs.append(jnp.concatenate([stack(kk * a * gi), stack(kd * gi)], axis=0))
        v2.append(stack(v))
        total.append(cl[c - 1:c, :] if forward else cl[0:1, :])

    res = [_dot_nt(lhs[i], rhs[i]) for i in range(n)]
    pr = [_dot_nt(lhs[i], chains[i][8]) for i in range(n)]
    strict = [(row > col) if f else (row < col) for f in fwd]
    incl = [(row >= col) if f else (row <= col) for f in fwd]
    ab = [jnp.where(strict[i], res[i][:c2, :c2], 0.0) for i in range(n)]
    ak = [jnp.where(strict[i], res[i][:c2, c2:], 0.0) for i in range(n)]
    gb_gk = [jnp.concatenate([jnp.where(incl[i], res[i][c2:, :c2], 0.0),
                              jnp.where(incl[i], res[i][c2:, c2:], 0.0)], axis=1).astype(BF16) for i in range(n)]
    akv = [_dot(ak[i], v2[i]) for i in range(n)]

    tinv = [eye - ab[i] for i in range(n)]
    pw = [-ab[i] for i in range(n)]
    for _ in range(c.bit_length() - 2):
        pw = [_dot(pw[i], pw[i]) for i in range(n)]
        tinv = [tinv[i] + _dot(tinv[i], pw[i]) for i in range(n)]

    u2 = [_dot(tinv[i], -pr[i][:c2, :] - akv[i]) for i in range(n)]
    uv = [jnp.concatenate([u2[i].astype(BF16), v2[i]], axis=0) for i in range(n)]
    y2 = [pr[i][c2:, :] + _dot(gb_gk[i], uv[i]) for i in range(n)]
    s_new = [(chains[i][8] + _dot_tn(uv[i], rhs[i])) * jnp.exp(total[i]) for i in range(n)]
    return [(y2[i][:c, :] + y2[i][c:, :], s_new[i]) for i in range(n)]


def _scan_direction(r_ref, k_ref, v_ref, l_ref, d, s_ref, w0, w2, a0, a2, kk, ka, forward):
    lora = l_ref[0]
    c = lora.shape[0]
    dir_rows = (lax.broadcasted_iota(I32, (LANES, 1), 0) // HEAD_A) == d
    wl = w0[d:d + 1, :] + _dot_x3(jnp.tanh(lora[:, :LANES]), jnp.where(dir_rows, w2[...], 0.0))
    logw = -DECAY_SCALE * _sigmoid(wl)
    a = _sigmoid(a0[d:d + 1, :] + _dot_x3(lora[:, LANES:], jnp.where(dir_rows, a2[...], 0.0)))
    ti = lax.broadcasted_iota(I32, (c, c), 0)
    tj = lax.broadcasted_iota(I32, (c, c), 1)
    cum = jnp.where((tj <= ti) if forward else (tj >= ti), 1.0, 0.0).astype(BF16)
    cl = _dot_split_rhs(cum, logw)
    chains = []
    for p in range(s_ref.shape[0]):
        sl = slice(p * LANES, (p + 1) * LANES)
        chains.append((r_ref[0, :, sl], k_ref[0, :, sl], v_ref[0, :, sl], logw[:, sl], cl[:, sl],
                       a[:, sl], kk[:, sl], ka[:, sl], s_ref[p], forward))
    return chains


def _scan_store(out, y_ref, s_ref):
    for p, (y, s_new) in enumerate(out):
        y_ref[0, :, p * LANES:(p + 1) * LANES] = y
        s_ref[p] = s_new


def _scan_kernel(rf, kf, vf, lf, rb, kb, vb, lb, w0, w2, a0, a2, kk, ka, s0f, s0b,
                 yf, yb, sf, sb, s2f, s2b, *, zero_init):
    c = pl.program_id(2)

    @pl.when(c == 0)
    def _():
        if zero_init:
            s2f[...] = jnp.zeros_like(s2f)
            s2b[...] = jnp.zeros_like(s2b)
        else:
            s2f[...] = s0f[0]
            s2b[...] = s0b[0]

    chains_f = _scan_direction(rf, kf, vf, lf, 0, s2f, w0, w2, a0, a2, kk, ka, True)
    chains_b = _scan_direction(rb, kb, vb, lb, 1, s2b, w0, w2, a0, a2, kk, ka, False)
    out = _scan_chunks(chains_f + chains_b)
    _scan_store(out[:len(chains_f)], yf, s2f)
    _scan_store(out[len(chains_f):], yb, s2b)

    @pl.when(c == pl.num_programs(2) - 1)
    def _():
        sf[0] = s2f[...]
        sb[0] = s2b[...]


def _rwkv_scan(zs, w0, w2r, a0, a2r, k_k, k_a, s0f, s0b, zero_init):
    b, t, _ = zs.shape
    d_a = w0.shape[1]
    pp = SCAN_PAIRS
    wide = pp * LANES
    groups = d_a // wide
    c = SCAN_CHUNK
    nc = t // c
    lora_blk = (4 * d_a) // (2 * LANES)
    fw = lambda off: (lambda i, q, j: (i, j, off + q))
    bw = lambda off: (lambda i, q, j: (i, nc - 1 - j, off + q))
    par = lambda i, q, j: (0, q)
    st = lambda i, q, j: (i, q, 0, 0)
    blk = (1, c, wide)
    lblk = (1, c, 2 * LANES)
    sblk = (1, pp, LANES, LANES)
    yshape = jax.ShapeDtypeStruct((b, t, d_a), F32)
    sshape = jax.ShapeDtypeStruct((b, d_a // LANES, LANES, LANES), F32)
    return pl.pallas_call(
        functools.partial(_scan_kernel, zero_init=zero_init),
        out_shape=(yshape, yshape, sshape, sshape),
        grid=(b, groups, nc),
        in_specs=[pl.BlockSpec(blk, fw(0)), pl.BlockSpec(blk, fw(groups)), pl.BlockSpec(blk, fw(2 * groups)),
                  pl.BlockSpec(lblk, lambda i, q, j: (i, j, lora_blk)),
                  pl.BlockSpec(blk, bw(0)), pl.BlockSpec(blk, bw(groups)), pl.BlockSpec(blk, bw(2 * groups)),
                  pl.BlockSpec(lblk, lambda i, q, j: (i, nc - 1 - j, lora_blk)),
                  pl.BlockSpec((2, wide), par), pl.BlockSpec((LANES, wide), par),
                  pl.BlockSpec((2, wide), par), pl.BlockSpec((LANES, wide), par),
                  pl.BlockSpec((1, wide), par), pl.BlockSpec((1, wide), par),
                  pl.BlockSpec(sblk, st), pl.BlockSpec(sblk, st)],
        out_specs=(pl.BlockSpec(blk, fw(0)), pl.BlockSpec(blk, bw(0)),
                   pl.BlockSpec(sblk, st), pl.BlockSpec(sblk, st)),
        scratch_shapes=[pltpu.VMEM((pp, LANES, LANES), F32), pltpu.VMEM((pp, LANES, LANES), F32)],
        compiler_params=_params(("parallel", "parallel", "arbitrary")),
        name="rwkv7_scan",
    )(zs, zs, zs, zs, zs, zs, zs, zs, w0, w2r, a0, a2r, k_k, k_a, s0f, s0b)


def _post_kernel(yf, yb, r, k, v, g, lora, u, vg, a0, a2, ka, rk, lxw, lxb, lvw, lvb, ws, bst, o_ref):
    tm = yf.shape[0]
    d_a = yf.shape[1]
    first_head = lax.broadcasted_iota(I32, (1, LANES), 1) < HEAD_A
    dir_row = lax.broadcasted_iota(I32, (LANES, 1), 0) // HEAD_A
    la = lora[:, LANES:]
    a_sum = jnp.zeros((tm, d_a), F32)
    for d in range(2):
        a_sum = a_sum + _sigmoid(a0[d:d + 1, :] + _dot_x3(la, jnp.where(dir_row == d, a2[...], 0.0)))
    rkk = r[...] * k[...] * (2.0 + (a_sum - 2.0) * ka[...]) * rk[...]
    y = yf[...] + yb[...]
    inv = 1.0 / HEAD_A
    for j in range(d_a // LANES):
        sl = slice(j * LANES, (j + 1) * LANES)
        yj = y[:, sl]
        mu = _head_sum(yj, first_head) * inv
        dl = yj - mu
        var = _head_sum(dl * dl, first_head) * inv
        yn = dl * lax.rsqrt(var + GN_EPS) * lxw[:, sl] + lxb[:, sl]
        bonus = _head_sum(rkk[:, sl], first_head) * v[:, sl]
        o_ref[:, sl] = ((yn + bonus) * _sigmoid(g[:, sl])).astype(BF16)

    uu = _gelu(u[...])
    vv = _gelu(vg[...])
    mu = jnp.mean(vv, axis=-1, keepdims=True)
    dv = vv - mu
    vn = dv * lax.rsqrt(jnp.mean(dv * dv, axis=-1, keepdims=True) + EPS) * lvw[...] + lvb[...]
    for ch in range(tm // GMLP_CHUNK):
        rows = slice(ch * GMLP_CHUNK, (ch + 1) * GMLP_CHUNK)
        for h in range(ws.shape[0]):
            cols = slice(h * LANES, (h + 1) * LANES)
            sp = _dot(ws[h], vn[rows, cols]) + bst[:, h:h + 1]
            o_ref[rows, d_a + h * LANES:d_a + (h + 1) * LANES] = (uu[rows, cols] * sp).astype(BF16)


def _post_mix(yf, yb, zs, z, a0, a2r, k_a, r_k, lxw, lxb, lvw, lvb, w_s, b_st):
    n, d_a = yf.shape
    tm = 256
    wide = lambda j: (lambda i: (i, j))
    full = lambda shape: pl.BlockSpec(shape, lambda i: (0,) * len(shape))
    lora_blk = (4 * d_a) // (2 * LANES)
    return pl.pallas_call(
        _post_kernel,
        out_shape=jax.ShapeDtypeStruct((n, 2 * d_a), BF16),
        grid=(n // tm,),
        in_specs=[pl.BlockSpec((tm, d_a), wide(0)), pl.BlockSpec((tm, d_a), wide(0)),
                  pl.BlockSpec((tm, d_a), wide(0)), pl.BlockSpec((tm, d_a), wide(1)),
                  pl.BlockSpec((tm, d_a), wide(2)), pl.BlockSpec((tm, d_a), wide(3)),
                  pl.BlockSpec((tm, 2 * LANES), wide(lora_blk)),
                  pl.BlockSpec((tm, d_a), wide(4)), pl.BlockSpec((tm, d_a), wide(5)),
                  full((2, d_a)), full((LANES, d_a)), full((1, d_a)), full((1, d_a)),
                  full((1, d_a)), full((1, d_a)), full((1, d_a)), full((1, d_a)),
                  full(w_s.shape), full(b_st.shape)],
        out_specs=pl.BlockSpec((tm, 2 * d_a), wide(0)),
        compiler_params=_params(("parallel",)),
        name="mix_post",
    )(yf, yb, zs, zs, zs, zs, zs, z, z, a0, a2r, k_a, r_k, lxw, lxb, lvw, lvb, w_s, b_st)


def _rms(x, g):
    return x * lax.rsqrt(jnp.mean(x * x, axis=-1, keepdims=True) + EPS) * g


def _out_proj_kernel(a_ref, w_ref, x_ref, mod_ref, gpost_ref, gpre_ref, x1_ref, h2_ref):
    o = jnp.dot(a_ref[...], w_ref[...], preferred_element_type=F32)
    x1 = x_ref[...] + mod_ref[0, 2:3, :] * _rms(o, gpost_ref[...])
    x1_ref[...] = x1
    h2_ref[...] = _rms(x1, gpre_ref[...]) * (1.0 + mod_ref[0, 4:5, :]) + mod_ref[0, 3:4, :]


def _out_proj(yag, w_bf, x, modm, g_post, g_pre2, group0, rows_per_group):
    n, d = x.shape
    tm = 256
    grp = lambda i: (group0 + (i * tm) // rows_per_group, 0, 0)
    row = lambda i: (i, 0)
    fix = lambda i: (0, 0)
    shp = jax.ShapeDtypeStruct((n, d), F32)
    return pl.pallas_call(
        _out_proj_kernel,
        out_shape=(shp, shp),
        grid=(n // tm,),
        in_specs=[pl.BlockSpec((tm, yag.shape[1]), row), pl.BlockSpec(w_bf.shape, fix),
                  pl.BlockSpec((tm, d), row), pl.BlockSpec((1, 6, d), grp),
                  pl.BlockSpec((1, d), fix), pl.BlockSpec((1, d), fix)],
        out_specs=(pl.BlockSpec((tm, d), row), pl.BlockSpec((tm, d), row)),
        compiler_params=_params(("parallel",)),
        name="out_proj",
    )(yag, w_bf, x, modm, g_post, g_pre2)


def _query_kernel(h_ref, wq_ref, sk_ref, o_ref):
    q = jnp.dot(h_ref[...].astype(BF16), wq_ref[...], preferred_element_type=F32)
    for g in range(sk_ref.shape[0]):
        o_ref[g * N_KEYS:(g + 1) * N_KEYS, :] = _dot_nt(sk_ref[g], q[:, g * LANES:(g + 1) * LANES])


def _query_scores(h2, wq_bf, sk):
    n, d = h2.shape
    tm = 256
    groups = sk.shape[0]
    return pl.pallas_call(
        _query_kernel,
        out_shape=jax.ShapeDtypeStruct((groups * N_KEYS, n), F32),
        grid=(n // tm,),
        in_specs=[pl.BlockSpec((tm, d), lambda i: (i, 0)),
                  pl.BlockSpec(wq_bf.shape, lambda i: (0, 0)),
                  pl.BlockSpec(sk.shape, lambda i: (0, 0, 0))],
        out_specs=pl.BlockSpec((groups * N_KEYS, tm), lambda i: (0, i)),
        compiler_params=_params(("parallel",)),
        name="peer_query",
    )(h2, wq_bf, sk)


def _top16(s, pos=None):
    if pos is None:
        pos = lax.broadcasted_iota(I32, s.shape, 0)
    big = jnp.int32(2 ** 30)
    vals, idxs = [], []
    for _ in range(TOPK):
        m = jnp.max(s, axis=0, keepdims=True)
        i = jnp.min(jnp.where(s == m, pos, big), axis=0, keepdims=True)
        vals.append(m)
        idxs.append(i)
        s = jnp.where(pos == i, -jnp.inf, s)
    return jnp.concatenate(vals, axis=0), jnp.concatenate(idxs, axis=0)


def _pair_candidates(v1, v2):
    r8 = lax.broadcasted_iota(I32, (8, 1), 0)
    r16 = lax.broadcasted_iota(I32, (TOPK, 1), 0)
    ninf = -jnp.inf
    blocks = [
        (v1[0:1, :] + v2, r16),
        (v1[1:2, :] + v2[0:8, :], TOPK + r8),
        (jnp.where(r8 < 5, v1[2:3, :] + v2[0:8, :], ninf), 2 * TOPK + r8),
        (jnp.where(r8 < 4, v1[3:4, :] + v2[0:8, :], ninf), 3 * TOPK + r8),
        (jnp.where(r16 >= 4, v1 + v2[0:1, :], ninf), r16 * TOPK),
        (jnp.where(r8 >= 4, v1[0:8, :] + v2[1:2, :], ninf), r8 * TOPK + 1),
        (jnp.where(r8 == 4, v1[0:8, :] + v2[2:3, :], ninf), r8 * TOPK + 2),
    ]
    cand = jnp.concatenate([b for b, _ in blocks], axis=0)
    pos = jnp.concatenate([p for _, p in blocks], axis=0)
    return cand, jnp.broadcast_to(pos, cand.shape)


def _topk_kernel(s_ref, idx_ref, gate_ref):
    def head(h, carry):
        base = pl.multiple_of(h * (2 * N_KEYS), 2 * N_KEYS)
        v1, i1 = _top16(s_ref[pl.ds(base, N_KEYS), :])
        v2, i2 = _top16(s_ref[pl.ds(base + N_KEYS, N_KEYS), :])
        top_s, pos = _top16(*_pair_candidates(v1, v2))
        pi = pos // TOPK
        pj = pos % TOPK
        e1 = jnp.zeros_like(pos)
        e2 = jnp.zeros_like(pos)
        for i in range(TOPK):
            e1 = e1 + jnp.where(pi == i, i1[i:i + 1, :], 0)
            e2 = e2 + jnp.where(pj == i, i2[i:i + 1, :], 0)
        ex = jnp.exp(top_s - top_s[0:1, :])
        out = pl.multiple_of(h * TOPK, TOPK)
        idx_ref[pl.ds(out, TOPK), :] = e1 * N_KEYS + e2
        gate_ref[pl.ds(out, TOPK), :] = ex / jnp.sum(ex, axis=0, keepdims=True)
        return carry

    lax.fori_loop(0, PK_HEADS, head, 0)


def _peer_topk(scores):
    rows, n = scores.shape
    tt = 256
    out_rows = PK_HEADS * TOPK
    return pl.pallas_call(
        _topk_kernel,
        out_shape=(jax.ShapeDtypeStruct((out_rows, n), I32), jax.ShapeDtypeStruct((out_rows, n), F32)),
        grid=(n // tt,),
        in_specs=[pl.BlockSpec((rows, tt), lambda i: (0, i))],
        out_specs=(pl.BlockSpec((out_rows, tt), lambda i: (0, i)),
                   pl.BlockSpec((out_rows, tt), lambda i: (0, i))),
        compiler_params=_params(("parallel",)),
        name="peer_topk",
    )(scores)


SC_CORES = 2
SC_SUBCORES = 16
SC_LANES = 16
PEER_GROUP = 16
PEER_BLOCK = 8
U_MASK = -65536


def _pack_experts(eu, ev):
    hi = lax.bitcast_convert_type(eu.astype(BF16), jnp.uint16).astype(jnp.uint32)
    lo = lax.bitcast_convert_type(ev.astype(BF16), jnp.uint16).astype(jnp.uint32)
    return lax.bitcast_convert_type((hi << 16) | lo, I32)


def _unpack_u(w):
    return lax.bitcast_convert_type(w & U_MASK, F32)


def _unpack_v(w):
    return lax.bitcast_convert_type(w << 16, F32)


def _peer_experts_sc(idx, x, gates, table, start):
    d = x.shape[1]
    n = x.shape[0] - start
    n_sel = idx.shape[1]
    per_w = n // (SC_CORES * SC_SUBCORES)
    L, G, TB = SC_LANES, PEER_GROUP, PEER_BLOCK
    n_groups = n_sel // G
    n_ch = d // L
    mesh = plsc.VectorSubcoreMesh(core_axis_name="c", subcore_axis_name="s")

    @functools.partial(
        pl.kernel, mesh=mesh, out_type=jax.ShapeDtypeStruct((n, d), F32),
        compiler_params=pltpu.CompilerParams(needs_layout_passes=False),
        scratch_types=[pltpu.VMEM((TB, n_sel), I32), pltpu.VMEM((TB, n_sel), F32),
                       pltpu.VMEM((2, d), F32), pltpu.VMEM((2, d), F32),
                       pltpu.VMEM((2, G, d), I32), pltpu.VMEM((G, 2 * L), F32), pltpu.VMEM((2 * L,), F32),
                       pltpu.SemaphoreType.DMA((2,)), pltpu.SemaphoreType.DMA((2,)), pltpu.SemaphoreType.DMA((2,))],
        name="peer_experts_sc")
    def k(idx_hbm, x_hbm, g_hbm, tab_hbm, o_hbm, idx_v, gate_v, x_v, o_v, buf, acc_v, coef_v, sem, xsem, osem):
        out_base = (lax.axis_index("s") * SC_CORES + lax.axis_index("c")) * per_w
        base = start + out_base
        lane = lax.iota(I32, L)
        zero = jnp.zeros((L,), F32)

        def gather(t, g, slot):
            return pltpu.make_async_copy(tab_hbm.at[idx_v.at[t, pl.ds(g * G, G)]], buf.at[slot], sem.at[slot])

        def x_copy(j, par):
            return pltpu.make_async_copy(x_hbm.at[base + j], x_v.at[par], xsem.at[par])

        def o_copy(j, par):
            return pltpu.make_async_copy(o_v.at[par], o_hbm.at[out_base + j], osem.at[par])

        x_copy(0, 0).start()

        @pl.loop(0, per_w // TB)
        def _(blk):
            tok0 = base + blk * TB
            pltpu.sync_copy(idx_hbm.at[pl.ds(tok0, TB)], idx_v)
            pltpu.sync_copy(g_hbm.at[pl.ds(tok0, TB)], gate_v)

            @pl.loop(0, TB)
            def _(t):
                j = blk * TB + t
                par = j % 2
                gather(t, 0, 0).start()

                @pl.when(j + 1 < per_w)
                def _():
                    x_copy(j + 1, 1 - par).start()

                x_copy(j, par).wait()

                @pl.when(j >= 2)
                def _():
                    o_copy(j - 2, par).wait()

                for g in range(n_groups):
                    slot = g % 2
                    if g + 1 < n_groups:
                        gather(t, g + 1, 1 - slot).start()
                    gather(t, g, slot).wait()

                    def u_body(ch, accs):
                        xv = x_v[par, pl.ds(ch * L, L)]
                        return tuple(accs[e] + _unpack_u(buf[slot, e, pl.ds(ch * L, L)]) * xv for e in range(G))

                    accs = plsc.parallel_loop(0, n_ch, carry=tuple(zero for _ in range(G)))(u_body)
                    for e in range(G):
                        acc_v[e, pl.ds(L, L)] = accs[e]
                    act = zero
                    for l in range(L):
                        act = act + plsc.load_gather(acc_v, [lane, jnp.full((L,), L + l, I32)])
                    y = 0.7978845608028654 * (act + 0.044715 * (act * act * act))
                    coef_v[pl.ds(L, L)] = act / (1.0 + jnp.exp(-2.0 * y)) * gate_v[t, pl.ds(g * G, G)]
                    cs = [plsc.load_gather(coef_v, [jnp.full((L,), L + e, I32)]) for e in range(G)]

                    @plsc.parallel_loop(0, n_ch)
                    def _(ch):
                        o = zero if g == 0 else o_v[par, pl.ds(ch * L, L)]
                        for e in range(G):
                            o = o + cs[e] * _unpack_v(buf[slot, e, pl.ds(ch * L, L)])
                        o_v[par, pl.ds(ch * L, L)] = o

                o_copy(j, par).start()

        o_copy(per_w - 2, 0).wait()
        o_copy(per_w - 1, 1).wait()

    return k(idx, x, gates, table)


PEER_TC_TOKENS = 64
CTX_PARTS = 4
PEER_TC_SLOTS = 3
PEER_TC_CTX = 512
PEER_TC_LATENT = 3584


def _peer_tc_kernel(idx_ref, x_ref, g_ref, uv_ref, o_ref, buf, sem):
    slots, rows, n_sel = buf.shape[0], buf.shape[1], buf.shape[2]
    tokens = g_ref.shape[0]
    ahead = slots - 1

    def issue(t):
        slot = t % slots

        def body(e, carry):
            src = pl.multiple_of(idx_ref[t, e] * rows, rows)
            pltpu.make_async_copy(uv_ref.at[pl.ds(src, rows), :], buf.at[slot, :, e, :], sem.at[slot]).start()
            return carry
        lax.fori_loop(0, n_sel, body, 0, unroll=8)

    def wait(slot):
        pltpu.make_async_copy(buf.at[(slot + 1) % slots], buf.at[slot], sem.at[slot]).wait()

    for t0 in range(ahead):
        issue(t0)
    g_t = g_ref[...].T
    g_hi, g_mid = _split(g_t)
    g_lo = (g_t - g_hi.astype(F32) - g_mid.astype(F32)).astype(BF16)
    tok_iota = lax.broadcasted_iota(I32, (tokens, LANES), 0)
    dot = functools.partial(jnp.dot, preferred_element_type=F32)

    def token(t, carry):
        slot = t % slots

        @pl.when(t + ahead < tokens)
        def _():
            issue(t + ahead)

        wait(slot)
        pick = jnp.where(tok_iota == t, 1.0, 0.0).astype(BF16)
        gate = dot(g_hi, pick) + (dot(g_mid, pick) + dot(g_lo, pick))
        xt = x_ref[pl.ds(t, 1), :]
        acc = _unpack_u(buf[slot, 0]) * xt[:, 0:LANES]
        for c in range(1, rows):
            acc = acc + _unpack_u(buf[slot, c]) * xt[:, c * LANES:(c + 1) * LANES]
        act = jnp.sum(acc, axis=-1, keepdims=True)
        coef = _gelu(act) * gate
        o_ref[pl.ds(t, 1), :] = jnp.concatenate(
            [jnp.sum(coef * _unpack_v(buf[slot, c]), axis=0, keepdims=True) for c in range(rows)], axis=1)
        return carry

    lax.fori_loop(0, tokens, token, 0)


def _peer_experts_tc(idx, x, gates, table_rows, m):
    d = x.shape[1]
    n_sel = idx.shape[1]
    tt = PEER_TC_TOKENS
    return pl.pallas_call(
        _peer_tc_kernel,
        out_shape=jax.ShapeDtypeStruct((m, d), F32),
        grid=(m // tt,),
        in_specs=[pl.BlockSpec((tt, n_sel), lambda i: (i, 0), memory_space=pltpu.SMEM),
                  pl.BlockSpec((tt, d), lambda i: (i, 0)),
                  pl.BlockSpec((tt, n_sel), lambda i: (i, 0)),
                  pl.BlockSpec(memory_space=pl.ANY)],
        out_specs=pl.BlockSpec((tt, d), lambda i: (i, 0)),
        scratch_shapes=[pltpu.VMEM((PEER_TC_SLOTS, d // LANES, n_sel, LANES), I32),
                        pltpu.SemaphoreType.DMA((PEER_TC_SLOTS,))],
        compiler_params=pltpu.CompilerParams(dimension_semantics=("arbitrary",),
                                             vmem_limit_bytes=VMEM_LIMIT, disable_bounds_checks=True),
        name="peer_experts_tc",
    )(idx, x, gates, table_rows)


def _final_kernel(x1_ref, ptc_ref, psc_ref, mod_ref, g_ref, o_ref, *, tc_tiles):
    p = jnp.where(pl.program_id(0) < tc_tiles, ptc_ref[...], psc_ref[...])
    o_ref[...] = x1_ref[...] + mod_ref[0, 5:6, :] * _rms(p, g_ref[...])


def _final(x1, peer_tc, peer_sc, modm, g_post2, group0, rows_per_group):
    n, d = x1.shape
    tm = 512
    tc_tiles = peer_tc.shape[0] // tm
    row = lambda i: (i, 0)
    return pl.pallas_call(
        functools.partial(_final_kernel, tc_tiles=tc_tiles),
        out_shape=jax.ShapeDtypeStruct((n, d), F32),
        grid=(n // tm,),
        in_specs=[pl.BlockSpec((tm, d), row),
                  pl.BlockSpec((tm, d), lambda i: (jnp.minimum(i, tc_tiles - 1), 0)),
                  pl.BlockSpec((tm, d), lambda i: (jnp.maximum(i - tc_tiles, 0), 0)),
                  pl.BlockSpec((1, 6, d), lambda i: (group0 + (i * tm) // rows_per_group, 0, 0)),
                  pl.BlockSpec((1, d), lambda i: (0, 0))],
        out_specs=pl.BlockSpec((tm, d), row),
        compiler_params=_params(("parallel",)),
        name="final_residual",
    )(x1, peer_tc, peer_sc, modm, g_post2)


def _block_diag_state(s):
    b, h = s.shape[:2]
    s = s.reshape(b, h // 2, 2, HEAD_A, HEAD_A)
    z = jnp.zeros_like(s[:, :, 0])
    return jnp.concatenate([jnp.concatenate([s[:, :, 0], z], axis=-1),
                            jnp.concatenate([z, s[:, :, 1]], axis=-1)], axis=-2)


def _head_states(s2):
    b, p = s2.shape[:2]
    return jnp.stack([s2[:, :, :HEAD_A, :HEAD_A], s2[:, :, HEAD_A:, HEAD_A:]],
                     axis=2).reshape(b, 2 * p, HEAD_A, HEAD_A)


def _layer(x3, modm, group0, grid_mode, s0f, s0b, w, tc_tokens):
    b, t, d = x3.shape
    n = b * t
    x = x3.reshape(n, d)
    rows_per_group = t if grid_mode else n
    z = _in_proj(x, modm, w["g_pre1"], w["w_in"], group0, rows_per_group)
    zs = _shift_mix(z, w["mu_shift"], w["n_shift"], w["lora_block"], grid_mode, t)
    zero_init = s0f is None
    if zero_init:
        s0f = s0b = jnp.zeros((b, w["w0"].shape[1] // LANES, LANES, LANES), F32)
    yf, yb, sf, sb = _rwkv_scan(zs.reshape(b, t, -1), w["w0"], w["w2"], w["a0"], w["a2"], w["k_k"], w["k_a"],
                                s0f, s0b, zero_init)
    d_a = yf.shape[-1]
    yag = _post_mix(yf.reshape(n, d_a), yb.reshape(n, d_a), zs, z, w["a0"], w["a2"], w["k_a"], w["r_k"],
                    w["ln_x_w"], w["ln_x_b"], w["ln_v_w"], w["ln_v_b"], w["w_s"], w["b_st"])
    x1, h2 = _out_proj(yag, w["w_out"], x, modm, w["g_post1"], w["g_pre2"], group0, rows_per_group)
    scores = _query_scores(h2, w["w_query"], w["sub_keys"])
    idx_t, gates_t = _peer_topk(scores)
    idx, gates = idx_t.T, gates_t.T
    peer_tc = _peer_experts_tc(idx, h2, gates, w["expert_rows"], tc_tokens)
    peer_sc = _peer_experts_sc(idx, h2, gates, w["experts"], tc_tokens)
    out = _final(x1, peer_tc, peer_sc, modm, w["g_post2"], group0, rows_per_group)
    return out.reshape(b, t, d), sf, sb


def kernel(x_prompt, x_sample, c, state_fwd, state_bwd, c_ctx, w_ada, b_ada, g_pre1, g_post1, g_pre2, g_post2,
           w_in, mu_shift, w0, w2, a0, a2, k_k, k_a, r_k, ln_x_w, ln_x_b, ln_v_w, ln_v_b, w_s, b_s, w_out,
           w_query, sub_keys, expert_u, expert_v):
    depth = w_in.shape[0]
    d = x_prompt.shape[-1]
    d_a = w0.shape[-1]
    n_shift = mu_shift.shape[-1]
    dec_b = x_sample.shape[0]
    cvec = jnp.concatenate([c_ctx[None], c, jnp.zeros((8 - 1 - dec_b, d), F32)], axis=0)
    xp, xs = x_prompt, x_sample
    new_f, new_b = [], []
    for l in range(depth):
        wl_in = w_in[l]
        row = lambda a: a[l].reshape(1, -1)
        w = {
            "w_in": jnp.concatenate([wl_in[:, :4 * d_a], wl_in[:, n_shift:], wl_in[:, 4 * d_a:n_shift]],
                                    axis=1).astype(BF16),
            "n_shift": n_shift,
            "lora_block": (wl_in.shape[1] - (n_shift - 4 * d_a)) // (2 * LANES),
            "mu_shift": row(mu_shift),
            "g_pre1": row(g_pre1), "g_post1": row(g_post1), "g_pre2": row(g_pre2), "g_post2": row(g_post2),
            "w0": w0[l], "w2": w2[l].reshape(-1, d_a), "a0": a0[l], "a2": a2[l].reshape(-1, d_a),
            "k_k": row(k_k), "k_a": row(k_a), "r_k": row(r_k),
            "ln_x_w": row(ln_x_w), "ln_x_b": row(ln_x_b), "ln_v_w": row(ln_v_w), "ln_v_b": row(ln_v_b),
            "w_s": w_s[l].astype(BF16), "b_st": b_s[l].T,
            "w_out": w_out[l].astype(BF16), "w_query": w_query[l].astype(BF16),
            "sub_keys": jnp.swapaxes(sub_keys[l], 0, 1).reshape(-1, N_KEYS, sub_keys.shape[-1]).astype(BF16),
            "experts": _pack_experts(expert_u[l], expert_v[l]),
        }
        w["expert_rows"] = w["experts"].reshape(-1, LANES)
        modm = _modulation(cvec, w_ada[l], b_ada[l].reshape(1, -1)).reshape(8, 6, d)
        piece = xp.shape[0] // CTX_PARTS
        parts = [_layer(xp[i * piece:(i + 1) * piece], modm, 0, False, None, None, w, PEER_TC_CTX)
                 for i in range(CTX_PARTS)]
        xp = jnp.concatenate([p[0] for p in parts], axis=0)
        new_f.append(_head_states(jnp.concatenate([p[1] for p in parts], axis=0)))
        new_b.append(_head_states(jnp.concatenate([p[2] for p in parts], axis=0)))
        xs, _, _ = _layer(xs, modm, 1, True, _block_diag_state(state_fwd[:, l]),
                          _block_diag_state(state_bwd[:, l]), w, PEER_TC_LATENT)
    return (xp, xs, jnp.stack(new_f, axis=1), jnp.stack(new_b, axis=1))
```

```python
import functools

import jax
import jax.numpy as jnp
from jax import lax
from jax.experimental import pallas as pl
from jax.experimental.pallas import tpu as pltpu
from jax.experimental.pallas import tpu_sc as plsc

F32 = jnp.float32
BF16 = jnp.bfloat16
I32 = jnp.int32

EPS = 1e-6
GN_EPS = 64e-5
HEAD_A = 64
LANES = 128
GRID_W = 64
GMLP_CHUNK = 128
PK_HEADS = 8
N_KEYS = 128
TOPK = 16
SCAN_CHUNK = 64
SCAN_PAIRS = 8
DECAY_SCALE = 0.6065306597126334
VMEM_LIMIT = 48 * 1024 * 1024


def _params(sem):
    return pltpu.CompilerParams(dimension_semantics=sem, vmem_limit_bytes=VMEM_LIMIT)


def _sigmoid(x):
    return 1.0 / (1.0 + jnp.exp(-x))


def _gelu(x):
    return 0.5 * x * (1.0 + jnp.tanh(0.7978845608028654 * (x + 0.044715 * (x * x * x))))


def _dot(a, b):
    return jnp.dot(a.astype(BF16), b.astype(BF16), preferred_element_type=F32)


def _dot_nt(a, b):
    return lax.dot_general(a.astype(BF16), b.astype(BF16), (((1,), (1,)), ((), ())),
                           preferred_element_type=F32)


def _dot_tn(a, b):
    return lax.dot_general(a.astype(BF16), b.astype(BF16), (((0,), (0,)), ((), ())),
                           preferred_element_type=F32)


def _split(x):
    hi = x.astype(BF16)
    return hi, (x - hi.astype(F32)).astype(BF16)


def _dot_x3(a, b):
    a_hi, a_lo = _split(a)
    b_hi, b_lo = _split(b)
    dot = functools.partial(jnp.dot, preferred_element_type=F32)
    return dot(a_hi, b_hi) + (dot(a_lo, b_hi) + dot(a_hi, b_lo))


def _dot_split_rhs(a_bf, b):
    b_hi, b_mid = _split(b)
    b_lo = (b - b_hi.astype(F32) - b_mid.astype(F32)).astype(BF16)
    dot = functools.partial(jnp.dot, preferred_element_type=F32)
    return dot(a_bf, b_hi) + (dot(a_bf, b_mid) + dot(a_bf, b_lo))


def _head_sum(x, first_head):
    s_a = jnp.sum(jnp.where(first_head, x, 0.0), axis=-1, keepdims=True)
    s_b = jnp.sum(jnp.where(first_head, 0.0, x), axis=-1, keepdims=True)
    return jnp.where(first_head, s_a, s_b)


def _mod_kernel(c_ref, w_ref, b_ref, o_ref):
    c = c_ref[...]
    o_ref[...] = _dot(c * _sigmoid(c), w_ref[...]) + b_ref[...]


def _modulation(cvec, w_ada, b_ada):
    rows, d = cvec.shape
    n = w_ada.shape[1]
    tn = 1024
    return pl.pallas_call(
        _mod_kernel,
        out_shape=jax.ShapeDtypeStruct((rows, n), F32),
        grid=(n // tn,),
        in_specs=[pl.BlockSpec((rows, d), lambda j: (0, 0)),
                  pl.BlockSpec((d, tn), lambda j: (0, j)),
                  pl.BlockSpec((1, tn), lambda j: (0, j))],
        out_specs=pl.BlockSpec((rows, tn), lambda j: (0, j)),
        compiler_params=_params(("parallel",)),
        name="adaln_mod",
    )(cvec, w_ada, b_ada)


def _in_proj_kernel(x_ref, mod_ref, g_ref, w_ref, o_ref, h_ref):
    @pl.when(pl.program_id(1) == 0)
    def _():
        x = x_ref[...]
        y = x * lax.rsqrt(jnp.mean(x * x, axis=-1, keepdims=True) + EPS) * g_ref[...]
        h_ref[...] = (y * (1.0 + mod_ref[0, 1:2, :]) + mod_ref[0, 0:1, :]).astype(BF16)

    o_ref[...] = jnp.dot(h_ref[...], w_ref[...], preferred_element_type=F32)


def _in_proj(x, modm, g_pre, w_bf, group0, rows_per_group):
    n, d = x.shape
    p = w_bf.shape[1]
    tm, tn = 512, 1280
    grp = lambda i, j: (group0 + (i * tm) // rows_per_group, 0, 0)
    return pl.pallas_call(
        _in_proj_kernel,
        out_shape=jax.ShapeDtypeStruct((n, p), F32),
        grid=(n // tm, p // tn),
        in_specs=[pl.BlockSpec((tm, d), lambda i, j: (i, 0)),
                  pl.BlockSpec((1, 6, d), grp),
                  pl.BlockSpec((1, d), lambda i, j: (0, 0)),
                  pl.BlockSpec((d, tn), lambda i, j: (0, j))],
        out_specs=pl.BlockSpec((tm, tn), lambda i, j: (i, j)),
        scratch_shapes=[pltpu.VMEM((tm, d), BF16)],
        compiler_params=_params(("parallel", "arbitrary")),
        name="in_proj",
    )(x, modm, g_pre, w_bf)


def _shift_kernel(z_ref, mu_ref, o_ref, *, grid_mode, period):
    z = z_ref[...]
    rows = z.shape[0]
    t = lax.broadcasted_iota(I32, (rows, 1), 0) % period
    prev = jnp.where(t % (GRID_W if grid_mode else period) != 0, pltpu.roll(z, 1, 0), 0.0)
    nxt = jnp.where(t % (GRID_W if grid_mode else period) != (GRID_W if grid_mode else period) - 1,
                    pltpu.roll(z, rows - 1, 0), 0.0)
    if grid_mode:
        up = jnp.where(t >= GRID_W, pltpu.roll(z, GRID_W, 0), 0.0)
        down = jnp.where(t < period - GRID_W, pltpu.roll(z, rows - GRID_W, 0), 0.0)
        nb = 0.25 * (up + down + prev + nxt)
    else:
        nb = 0.5 * (prev + nxt)
    o_ref[...] = z + mu_ref[...] * (nb - z)


def _shift_mix(z, mu, n_shift, lora_block, grid_mode, period):
    n = z.shape[0]
    tr, tc = min(n, 2048), 256
    main_blocks = (n_shift // tc) - 1
    col = lambda i, j: (i, jnp.where(j < main_blocks, j, lora_block))
    return pl.pallas_call(
        functools.partial(_shift_kernel, grid_mode=grid_mode, period=period),
        out_shape=jax.ShapeDtypeStruct((n, n_shift), F32),
        grid=(n // tr, n_shift // tc),
        in_specs=[pl.BlockSpec((tr, tc), col),
                  pl.BlockSpec((1, tc), lambda i, j: (0, j))],
        out_specs=pl.BlockSpec((tr, tc), lambda i, j: (i, j)),
        compiler_params=_params(("parallel", "parallel")),
        name="token_shift",
    )(z, mu)


def _scan_chunks(chains):
    c = chains[0][0].shape[0]
    c2 = 2 * c
    n = len(chains)
    fwd = [ch[9] for ch in chains]
    first_head = lax.broadcasted_iota(I32, (1, LANES), 1) < HEAD_A
    row = lax.broadcasted_iota(I32, (c2, c2), 0)
    col = lax.broadcasted_iota(I32, (c2, c2), 1)
    eye = jnp.where(row == col, 1.0, 0.0)

    def stack(x):
        return jnp.concatenate([jnp.where(first_head, x, 0.0), jnp.where(first_head, 0.0, x)],
                               axis=0).astype(BF16)

    lhs, rhs, v2, total = [], [], [], []
    for r, k, v, logw, cl, a, k_k, k_a, _, forward in chains:
        kkr = k * k_k
        kk = kkr / jnp.maximum(jnp.sqrt(_head_sum(kkr * kkr, first_head)), 1e-12)
        kd = k * (1.0 + (a - 1.0) * k_a)
        gi = jnp.exp(-cl)
        lhs.append(jnp.concatenate([stack(kk * jnp.exp(cl - logw)), stack(r * jnp.exp(cl))], axis=0))
        rhs.append(jnp.concatenate([stack(kk * a * gi), stack(kd * gi)], axis=0))
        v2.append(stack(v))
        total.append(cl[c - 1:c, :] if forward else cl[0:1, :])

    res = [_dot_nt(lhs[i], rhs[i]) for i in range(n)]
    pr = [_dot_nt(lhs[i], chains[i][8]) for i in range(n)]
    strict = [(row > col) if f else (row < col) for f in fwd]
    incl = [(row >= col) if f else (row <= col) for f in fwd]
    ab = [jnp.where(strict[i], res[i][:c2, :c2], 0.0) for i in range(n)]
    ak = [jnp.where(strict[i], res[i][:c2, c2:], 0.0) for i in range(n)]
    gb_gk = [jnp.concatenate([jnp.where(incl[i], res[i][c2:, :c2], 0.0),
                              jnp.where(incl[i], res[i][c2:, c2:], 0.0)], axis=1).astype(BF16) for i in range(n)]
    akv = [_dot(ak[i], v2[i]) for i in range(n)]

    tinv = [eye - ab[i] for i in range(n)]
    pw = [-ab[i] for i in range(n)]
    for _ in range(c.bit_length() - 2):
        pw = [_dot(pw[i], pw[i]) for i in range(n)]
        tinv = [tinv[i] + _dot(tinv[i], pw[i]) for i in range(n)]

    u2 = [_dot(tinv[i], -pr[i][:c2, :] - akv[i]) for i in range(n)]
    uv = [jnp.concatenate([u2[i].astype(BF16), v2[i]], axis=0) for i in range(n)]
    y2 = [pr[i][c2:, :] + _dot(gb_gk[i], uv[i]) for i in range(n)]
    s_new = [(chains[i][8] + _dot_tn(uv[i], rhs[i])) * jnp.exp(total[i]) for i in range(n)]
    return [(y2[i][:c, :] + y2[i][c:, :], s_new[i]) for i in range(n)]


def _scan_direction(r_ref, k_ref, v_ref, l_ref, d, s_ref, w0, w2, a0, a2, kk, ka, forward):
    lora = l_ref[0]
    c = lora.shape[0]
    dir_rows = (lax.broadcasted_iota(I32, (LANES, 1), 0) // HEAD_A) == d
    wl = w0[d:d + 1, :] + _dot_x3(jnp.tanh(lora[:, :LANES]), jnp.where(dir_rows, w2[...], 0.0))
    logw = -DECAY_SCALE * _sigmoid(wl)
    a = _sigmoid(a0[d:d + 1, :] + _dot_x3(lora[:, LANES:], jnp.where(dir_rows, a2[...], 0.0)))
    ti = lax.broadcasted_iota(I32, (c, c), 0)
    tj = lax.broadcasted_iota(I32, (c, c), 1)
    cum = jnp.where((tj <= ti) if forward else (tj >= ti), 1.0, 0.0).astype(BF16)
    cl = _dot_split_rhs(cum, logw)
    chains = []
    for p in range(s_ref.shape[0]):
        sl = slice(p * LANES, (p + 1) * LANES)
        chains.append((r_ref[0, :, sl], k_ref[0, :, sl], v_ref[0, :, sl], logw[:, sl], cl[:, sl],
                       a[:, sl], kk[:, sl], ka[:, sl], s_ref[p], forward))
    return chains


def _scan_store(out, y_ref, s_ref):
    for p, (y, s_new) in enumerate(out):
        y_ref[0, :, p * LANES:(p + 1) * LANES] = y
        s_ref[p] = s_new


def _scan_kernel(rf, kf, vf, lf, rb, kb, vb, lb, w0, w2, a0, a2, kk, ka, s0f, s0b,
                 yf, yb, sf, sb, s2f, s2b, *, zero_init):
    c = pl.program_id(2)

    @pl.when(c == 0)
    def _():
        if zero_init:
            s2f[...] = jnp.zeros_like(s2f)
            s2b[...] = jnp.zeros_like(s2b)
        else:
            s2f[...] = s0f[0]
            s2b[...] = s0b[0]

    chains_f = _scan_direction(rf, kf, vf, lf, 0, s2f, w0, w2, a0, a2, kk, ka, True)
    chains_b = _scan_direction(rb, kb, vb, lb, 1, s2b, w0, w2, a0, a2, kk, ka, False)
    out = _scan_chunks(chains_f + chains_b)
    _scan_store(out[:len(chains_f)], yf, s2f)
    _scan_store(out[len(chains_f):], yb, s2b)

    @pl.when(c == pl.num_programs(2) - 1)
    def _():
        sf[0] = s2f[...]
        sb[0] = s2b[...]


def _rwkv_scan(zs, w0, w2r, a0, a2r, k_k, k_a, s0f, s0b, zero_init):
    b, t, _ = zs.shape
    d_a = w0.shape[1]
    pp = SCAN_PAIRS
    wide = pp * LANES
    groups = d_a // wide
    c = SCAN_CHUNK
    nc = t // c
    lora_blk = (4 * d_a) // (2 * LANES)
    fw = lambda off: (lambda i, q, j: (i, j, off + q))
    bw = lambda off: (lambda i, q, j: (i, nc - 1 - j, off + q))
    par = lambda i, q, j: (0, q)
    st = lambda i, q, j: (i, q, 0, 0)
    blk = (1, c, wide)
    lblk = (1, c, 2 * LANES)
    sblk = (1, pp, LANES, LANES)
    yshape = jax.ShapeDtypeStruct((b, t, d_a), F32)
    sshape = jax.ShapeDtypeStruct((b, d_a // LANES, LANES, LANES), F32)
    return pl.pallas_call(
        functools.partial(_scan_kernel, zero_init=zero_init),
        out_shape=(yshape, yshape, sshape, sshape),
        grid=(b, groups, nc),
        in_specs=[pl.BlockSpec(blk, fw(0)), pl.BlockSpec(blk, fw(groups)), pl.BlockSpec(blk, fw(2 * groups)),
                  pl.BlockSpec(lblk, lambda i, q, j: (i, j, lora_blk)),
                  pl.BlockSpec(blk, bw(0)), pl.BlockSpec(blk, bw(groups)), pl.BlockSpec(blk, bw(2 * groups)),
                  pl.BlockSpec(lblk, lambda i, q, j: (i, nc - 1 - j, lora_blk)),
                  pl.BlockSpec((2, wide), par), pl.BlockSpec((LANES, wide), par),
                  pl.BlockSpec((2, wide), par), pl.BlockSpec((LANES, wide), par),
                  pl.BlockSpec((1, wide), par), pl.BlockSpec((1, wide), par),
                  pl.BlockSpec(sblk, st), pl.BlockSpec(sblk, st)],
        out_specs=(pl.BlockSpec(blk, fw(0)), pl.BlockSpec(blk, bw(0)),
                   pl.BlockSpec(sblk, st), pl.BlockSpec(sblk, st)),
        scratch_shapes=[pltpu.VMEM((pp, LANES, LANES), F32), pltpu.VMEM((pp, LANES, LANES), F32)],
        compiler_params=_params(("parallel", "parallel", "arbitrary")),
        name="rwkv7_scan",
    )(zs, zs, zs, zs, zs, zs, zs, zs, w0, w2r, a0, a2r, k_k, k_a, s0f, s0b)


def _post_kernel(yf, yb, r, k, v, g, lora, u, vg, a0, a2, ka, rk, lxw, lxb, lvw, lvb, ws, bst, o_ref):
    tm = yf.shape[0]
    d_a = yf.shape[1]
    first_head = lax.broadcasted_iota(I32, (1, LANES), 1) < HEAD_A
    dir_row = lax.broadcasted_iota(I32, (LANES, 1), 0) // HEAD_A
    la = lora[:, LANES:]
    a_sum = jnp.zeros((tm, d_a), F32)
    for d in range(2):
        a_sum = a_sum + _sigmoid(a0[d:d + 1, :] + _dot_x3(la, jnp.where(dir_row == d, a2[...], 0.0)))
    rkk = r[...] * k[...] * (2.0 + (a_sum - 2.0) * ka[...]) * rk[...]
    y = yf[...] + yb[...]
    inv = 1.0 / HEAD_A
    for j in range(d_a // LANES):
        sl = slice(j * LANES, (j + 1) * LANES)
        yj = y[:, sl]
        mu = _head_sum(yj, first_head) * inv
        dl = yj - mu
        var = _head_sum(dl * dl, first_head) * inv
        yn = dl * lax.rsqrt(var + GN_EPS) * lxw[:, sl] + lxb[:, sl]
        bonus = _head_sum(rkk[:, sl], first_head) * v[:, sl]
        o_ref[:, sl] = ((yn + bonus) * _sigmoid(g[:, sl])).astype(BF16)

    uu = _gelu(u[...])
    vv = _gelu(vg[...])
    mu = jnp.mean(vv, axis=-1, keepdims=True)
    dv = vv - mu
    vn = dv * lax.rsqrt(jnp.mean(dv * dv, axis=-1, keepdims=True) + EPS) * lvw[...] + lvb[...]
    for ch in range(tm // GMLP_CHUNK):
        rows = slice(ch * GMLP_CHUNK, (ch + 1) * GMLP_CHUNK)
        for h in range(ws.shape[0]):
            cols = slice(h * LANES, (h + 1) * LANES)
            sp = _dot(ws[h], vn[rows, cols]) + bst[:, h:h + 1]
            o_ref[rows, d_a + h * LANES:d_a + (h + 1) * LANES] = (uu[rows, cols] * sp).astype(BF16)


def _post_mix(yf, yb, zs, z, a0, a2r, k_a, r_k, lxw, lxb, lvw, lvb, w_s, b_st):
    n, d_a = yf.shape
    tm = 256
    wide = lambda j: (lambda i: (i, j))
    full = lambda shape: pl.BlockSpec(shape, lambda i: (0,) * len(shape))
    lora_blk = (4 * d_a) // (2 * LANES)
    return pl.pallas_call(
        _post_kernel,
        out_shape=jax.ShapeDtypeStruct((n, 2 * d_a), BF16),
        grid=(n // tm,),
        in_specs=[pl.BlockSpec((tm, d_a), wide(0)), pl.BlockSpec((tm, d_a), wide(0)),
                  pl.BlockSpec((tm, d_a), wide(0)), pl.BlockSpec((tm, d_a), wide(1)),
                  pl.BlockSpec((tm, d_a), wide(2)), pl.BlockSpec((tm, d_a), wide(3)),
                  pl.BlockSpec((tm, 2 * LANES), wide(lora_blk)),
                  pl.BlockSpec((tm, d_a), wide(4)), pl.BlockSpec((tm, d_a), wide(5)),
                  full((2, d_a)), full((LANES, d_a)), full((1, d_a)), full((1, d_a)),
                  full((1, d_a)), full((1, d_a)), full((1, d_a)), full((1, d_a)),
                  full(w_s.shape), full(b_st.shape)],
        out_specs=pl.BlockSpec((tm, 2 * d_a), wide(0)),
        compiler_params=_params(("parallel",)),
        name="mix_post",
    )(yf, yb, zs, zs, zs, zs, zs, z, z, a0, a2r, k_a, r_k, lxw, lxb, lvw, lvb, w_s, b_st)


def _rms(x, g):
    return x * lax.rsqrt(jnp.mean(x * x, axis=-1, keepdims=True) + EPS) * g


def _out_proj_kernel(a_ref, w_ref, x_ref, mod_ref, gpost_ref, gpre_ref, x1_ref, h2_ref):
    o = jnp.dot(a_ref[...], w_ref[...], preferred_element_type=F32)
    x1 = x_ref[...] + mod_ref[0, 2:3, :] * _rms(o, gpost_ref[...])
    x1_ref[...] = x1
    h2_ref[...] = _rms(x1, gpre_ref[...]) * (1.0 + mod_ref[0, 4:5, :]) + mod_ref[0, 3:4, :]


def _out_proj(yag, w_bf, x, modm, g_post, g_pre2, group0, rows_per_group):
    n, d = x.shape
    tm = 256
    grp = lambda i: (group0 + (i * tm) // rows_per_group, 0, 0)
    row = lambda i: (i, 0)
    fix = lambda i: (0, 0)
    shp = jax.ShapeDtypeStruct((n, d), F32)
    return pl.pallas_call(
        _out_proj_kernel,
        out_shape=(shp, shp),
        grid=(n // tm,),
        in_specs=[pl.BlockSpec((tm, yag.shape[1]), row), pl.BlockSpec(w_bf.shape, fix),
                  pl.BlockSpec((tm, d), row), pl.BlockSpec((1, 6, d), grp),
                  pl.BlockSpec((1, d), fix), pl.BlockSpec((1, d), fix)],
        out_specs=(pl.BlockSpec((tm, d), row), pl.BlockSpec((tm, d), row)),
        compiler_params=_params(("parallel",)),
        name="out_proj",
    )(yag, w_bf, x, modm, g_post, g_pre2)


def _query_kernel(h_ref, wq_ref, sk_ref, o_ref):
    q = jnp.dot(h_ref[...].astype(BF16), wq_ref[...], preferred_element_type=F32)
    for g in range(sk_ref.shape[0]):
        o_ref[g * N_KEYS:(g + 1) * N_KEYS, :] = _dot_nt(sk_ref[g], q[:, g * LANES:(g + 1) * LANES])


def _query_scores(h2, wq_bf, sk):
    n, d = h2.shape
    tm = 256
    groups = sk.shape[0]
    return pl.pallas_call(
        _query_kernel,
        out_shape=jax.ShapeDtypeStruct((groups * N_KEYS, n), F32),
        grid=(n // tm,),
        in_specs=[pl.BlockSpec((tm, d), lambda i: (i, 0)),
                  pl.BlockSpec(wq_bf.shape, lambda i: (0, 0)),
                  pl.BlockSpec(sk.shape, lambda i: (0, 0, 0))],
        out_specs=pl.BlockSpec((groups * N_KEYS, tm), lambda i: (0, i)),
        compiler_params=_params(("parallel",)),
        name="peer_query",
    )(h2, wq_bf, sk)


def _top16(s, pos=None):
    if pos is None:
        pos = lax.broadcasted_iota(I32, s.shape, 0)
    big = jnp.int32(2 ** 30)
    vals, idxs = [], []
    for _ in range(TOPK):
        m = jnp.max(s, axis=0, keepdims=True)
        i = jnp.min(jnp.where(s == m, pos, big), axis=0, keepdims=True)
        vals.append(m)
        idxs.append(i)
        s = jnp.where(pos == i, -jnp.inf, s)
    return jnp.concatenate(vals, axis=0), jnp.concatenate(idxs, axis=0)


def _pair_candidates(v1, v2):
    r8 = lax.broadcasted_iota(I32, (8, 1), 0)
    r16 = lax.broadcasted_iota(I32, (TOPK, 1), 0)
    ninf = -jnp.inf
    blocks = [
        (v1[0:1, :] + v2, r16),
        (v1[1:2, :] + v2[0:8, :], TOPK + r8),
        (jnp.where(r8 < 5, v1[2:3, :] + v2[0:8, :], ninf), 2 * TOPK + r8),
        (jnp.where(r8 < 4, v1[3:4, :] + v2[0:8, :], ninf), 3 * TOPK + r8),
        (jnp.where(r16 >= 4, v1 + v2[0:1, :], ninf), r16 * TOPK),
        (jnp.where(r8 >= 4, v1[0:8, :] + v2[1:2, :], ninf), r8 * TOPK + 1),
        (jnp.where(r8 == 4, v1[0:8, :] + v2[2:3, :], ninf), r8 * TOPK + 2),
    ]
    cand = jnp.concatenate([b for b, _ in blocks], axis=0)
    pos = jnp.concatenate([p for _, p in blocks], axis=0)
    return cand, jnp.broadcast_to(pos, cand.shape)


def _topk_kernel(s_ref, idx_ref, gate_ref):
    def head(h, carry):
        base = pl.multiple_of(h * (2 * N_KEYS), 2 * N_KEYS)
        v1, i1 = _top16(s_ref[pl.ds(base, N_KEYS), :])
        v2, i2 = _top16(s_ref[pl.ds(base + N_KEYS, N_KEYS), :])
        top_s, pos = _top16(*_pair_candidates(v1, v2))
        pi = pos // TOPK
        pj = pos % TOPK
        e1 = jnp.zeros_like(pos)
        e2 = jnp.zeros_like(pos)
        for i in range(TOPK):
            e1 = e1 + jnp.where(pi == i, i1[i:i + 1, :], 0)
            e2 = e2 + jnp.where(pj == i, i2[i:i + 1, :], 0)
        ex = jnp.exp(top_s - top_s[0:1, :])
        out = pl.multiple_of(h * TOPK, TOPK)
        idx_ref[pl.ds(out, TOPK), :] = e1 * N_KEYS + e2
        gate_ref[pl.ds(out, TOPK), :] = ex / jnp.sum(ex, axis=0, keepdims=True)
        return carry

    lax.fori_loop(0, PK_HEADS, head, 0)


def _peer_topk(scores):
    rows, n = scores.shape
    tt = 256
    out_rows = PK_HEADS * TOPK
    return pl.pallas_call(
        _topk_kernel,
        out_shape=(jax.ShapeDtypeStruct((out_rows, n), I32), jax.ShapeDtypeStruct((out_rows, n), F32)),
        grid=(n // tt,),
        in_specs=[pl.BlockSpec((rows, tt), lambda i: (0, i))],
        out_specs=(pl.BlockSpec((out_rows, tt), lambda i: (0, i)),
                   pl.BlockSpec((out_rows, tt), lambda i: (0, i))),
        compiler_params=_params(("parallel",)),
        name="peer_topk",
    )(scores)


SC_CORES = 2
SC_SUBCORES = 16
SC_LANES = 16
PEER_GROUP = 16
PEER_BLOCK = 8
U_MASK = -65536


def _pack_experts(eu, ev):
    hi = lax.bitcast_convert_type(eu.astype(BF16), jnp.uint16).astype(jnp.uint32)
    lo = lax.bitcast_convert_type(ev.astype(BF16), jnp.uint16).astype(jnp.uint32)
    return lax.bitcast_convert_type((hi << 16) | lo, I32)


def _unpack_u(w):
    return lax.bitcast_convert_type(w & U_MASK, F32)


def _unpack_v(w):
    return lax.bitcast_convert_type(w << 16, F32)


def _peer_experts_sc(idx, x, gates, table, start):
    d = x.shape[1]
    n = x.shape[0] - start
    n_sel = idx.shape[1]
    per_w = n // (SC_CORES * SC_SUBCORES)
    L, G, TB = SC_LANES, PEER_GROUP, PEER_BLOCK
    n_groups = n_sel // G
    n_ch = d // L
    mesh = plsc.VectorSubcoreMesh(core_axis_name="c", subcore_axis_name="s")

    @functools.partial(
        pl.kernel, mesh=mesh, out_type=jax.ShapeDtypeStruct((n, d), F32),
        compiler_params=pltpu.CompilerParams(needs_layout_passes=False),
        scratch_types=[pltpu.VMEM((TB, n_sel), I32), pltpu.VMEM((TB, n_sel), F32),
                       pltpu.VMEM((2, d), F32), pltpu.VMEM((2, d), F32),
                       pltpu.VMEM((2, G, d), I32), pltpu.VMEM((G, 2 * L), F32), pltpu.VMEM((2 * L,), F32),
                       pltpu.SemaphoreType.DMA((2,)), pltpu.SemaphoreType.DMA((2,)), pltpu.SemaphoreType.DMA((2,))],
        name="peer_experts_sc")
    def k(idx_hbm, x_hbm, g_hbm, tab_hbm, o_hbm, idx_v, gate_v, x_v, o_v, buf, acc_v, coef_v, sem, xsem, osem):
        out_base = (lax.axis_index("s") * SC_CORES + lax.axis_index("c")) * per_w
        base = start + out_base
        lane = lax.iota(I32, L)
        zero = jnp.zeros((L,), F32)

        def gather(t, g, slot):
            return pltpu.make_async_copy(tab_hbm.at[idx_v.at[t, pl.ds(g * G, G)]], buf.at[slot], sem.at[slot])

        def x_copy(j, par):
            return pltpu.make_async_copy(x_hbm.at[base + j], x_v.at[par], xsem.at[par])

        def o_copy(j, par):
            return pltpu.make_async_copy(o_v.at[par], o_hbm.at[out_base + j], osem.at[par])

        x_copy(0, 0).start()

        @pl.loop(0, per_w // TB)
        def _(blk):
            tok0 = base + blk * TB
            pltpu.sync_copy(idx_hbm.at[pl.ds(tok0, TB)], idx_v)
            pltpu.sync_copy(g_hbm.at[pl.ds(tok0, TB)], gate_v)

            @pl.loop(0, TB)
            def _(t):
                j = blk * TB + t
                par = j % 2
                gather(t, 0, 0).start()

                @pl.when(j + 1 < per_w)
                def _():
                    x_copy(j + 1, 1 - par).start()

                x_copy(j, par).wait()

                @pl.when(j >= 2)
                def _():
                    o_copy(j - 2, par).wait()

                for g in range(n_groups):
                    slot = g % 2
                    if g + 1 < n_groups:
                        gather(t, g + 1, 1 - slot).start()
                    gather(t, g, slot).wait()

                    def u_body(ch, accs):
                        xv = x_v[par, pl.ds(ch * L, L)]
                        return tuple(accs[e] + _unpack_u(buf[slot, e, pl.ds(ch * L, L)]) * xv for e in range(G))

                    accs = plsc.parallel_loop(0, n_ch, carry=tuple(zero for _ in range(G)))(u_body)
                    for e in range(G):
                        acc_v[e, pl.ds(L, L)] = accs[e]
                    act = zero
                    for l in range(L):
                        act = act + plsc.load_gather(acc_v, [lane, jnp.full((L,), L + l, I32)])
                    y = 0.7978845608028654 * (act + 0.044715 * (act * act * act))
                    coef_v[pl.ds(L, L)] = act / (1.0 + jnp.exp(-2.0 * y)) * gate_v[t, pl.ds(g * G, G)]
                    cs = [plsc.load_gather(coef_v, [jnp.full((L,), L + e, I32)]) for e in range(G)]

                    @plsc.parallel_loop(0, n_ch)
                    def _(ch):
                        o = zero if g == 0 else o_v[par, pl.ds(ch * L, L)]
                        for e in range(G):
                            o = o + cs[e] * _unpack_v(buf[slot, e, pl.ds(ch * L, L)])
                        o_v[par, pl.ds(ch * L, L)] = o

                o_copy(j, par).start()

        o_copy(per_w - 2, 0).wait()
        o_copy(per_w - 1, 1).wait()

    return k(idx, x, gates, table)


PEER_TC_TOKENS = 64
CTX_PARTS = 8
PEER_TC_SLOTS = 3
PEER_TC_CTX = 256
PEER_TC_LATENT = 3584


def _peer_tc_kernel(idx_ref, x_ref, g_ref, uv_ref, o_ref, buf, sem):
    slots, rows, n_sel = buf.shape[0], buf.shape[1], buf.shape[2]
    tokens = g_ref.shape[0]
    ahead = slots - 1

    def issue(t):
        slot = t % slots

        def body(e, carry):
            src = pl.multiple_of(idx_ref[t, e] * rows, rows)
            pltpu.make_async_copy(uv_ref.at[pl.ds(src, rows), :], buf.at[slot, :, e, :], sem.at[slot]).start()
            return carry
        lax.fori_loop(0, n_sel, body, 0, unroll=8)

    def wait(slot):
        pltpu.make_async_copy(buf.at[(slot + 1) % slots], buf.at[slot], sem.at[slot]).wait()

    for t0 in range(ahead):
        issue(t0)
    g_t = g_ref[...].T
    g_hi, g_mid = _split(g_t)
    g_lo = (g_t - g_hi.astype(F32) - g_mid.astype(F32)).astype(BF16)
    tok_iota = lax.broadcasted_iota(I32, (tokens, LANES), 0)
    dot = functools.partial(jnp.dot, preferred_element_type=F32)

    def token(t, carry):
        slot = t % slots

        @pl.when(t + ahead < tokens)
        def _():
            issue(t + ahead)

        wait(slot)
        pick = jnp.where(tok_iota == t, 1.0, 0.0).astype(BF16)
        gate = dot(g_hi, pick) + (dot(g_mid, pick) + dot(g_lo, pick))
        xt = x_ref[pl.ds(t, 1), :]
        acc = _unpack_u(buf[slot, 0]) * xt[:, 0:LANES]
        for c in range(1, rows):
            acc = acc + _unpack_u(buf[slot, c]) * xt[:, c * LANES:(c + 1) * LANES]
        act = jnp.sum(acc, axis=-1, keepdims=True)
        coef = _gelu(act) * gate
        o_ref[pl.ds(t, 1), :] = jnp.concatenate(
            [jnp.sum(coef * _unpack_v(buf[slot, c]), axis=0, keepdims=True) for c in range(rows)], axis=1)
        return carry

    lax.fori_loop(0, tokens, token, 0)


def _peer_experts_tc(idx, x, gates, table_rows, m):
    d = x.shape[1]
    n_sel = idx.shape[1]
    tt = PEER_TC_TOKENS
    return pl.pallas_call(
        _peer_tc_kernel,
        out_shape=jax.ShapeDtypeStruct((m, d), F32),
        grid=(m // tt,),
        in_specs=[pl.BlockSpec((tt, n_sel), lambda i: (i, 0), memory_space=pltpu.SMEM),
                  pl.BlockSpec((tt, d), lambda i: (i, 0)),
                  pl.BlockSpec((tt, n_sel), lambda i: (i, 0)),
                  pl.BlockSpec(memory_space=pl.ANY)],
        out_specs=pl.BlockSpec((tt, d), lambda i: (i, 0)),
        scratch_shapes=[pltpu.VMEM((PEER_TC_SLOTS, d // LANES, n_sel, LANES), I32),
                        pltpu.SemaphoreType.DMA((PEER_TC_SLOTS,))],
        compiler_params=pltpu.CompilerParams(dimension_semantics=("arbitrary",),
                                             vmem_limit_bytes=VMEM_LIMIT, disable_bounds_checks=True),
        name="peer_experts_tc",
    )(idx, x, gates, table_rows)


def _final_kernel(x1_ref, ptc_ref, psc_ref, mod_ref, g_ref, o_ref, *, tc_tiles):
    p = jnp.where(pl.program_id(0) < tc_tiles, ptc_ref[...], psc_ref[...])
    o_ref[...] = x1_ref[...] + mod_ref[0, 5:6, :] * _rms(p, g_ref[...])


def _final(x1, peer_tc, peer_sc, modm, g_post2, group0, rows_per_group):
    n, d = x1.shape
    tm = 256
    tc_tiles = peer_tc.shape[0] // tm
    row = lambda i: (i, 0)
    return pl.pallas_call(
        functools.partial(_final_kernel, tc_tiles=tc_tiles),
        out_shape=jax.ShapeDtypeStruct((n, d), F32),
        grid=(n // tm,),
        in_specs=[pl.BlockSpec((tm, d), row),
                  pl.BlockSpec((tm, d), lambda i: (jnp.minimum(i, tc_tiles - 1), 0)),
                  pl.BlockSpec((tm, d), lambda i: (jnp.maximum(i - tc_tiles, 0), 0)),
                  pl.BlockSpec((1, 6, d), lambda i: (group0 + (i * tm) // rows_per_group, 0, 0)),
                  pl.BlockSpec((1, d), lambda i: (0, 0))],
        out_specs=pl.BlockSpec((tm, d), row),
        compiler_params=_params(("parallel",)),
        name="final_residual",
    )(x1, peer_tc, peer_sc, modm, g_post2)


def _block_diag_state(s):
    b, h = s.shape[:2]
    s = s.reshape(b, h // 2, 2, HEAD_A, HEAD_A)
    z = jnp.zeros_like(s[:, :, 0])
    return jnp.concatenate([jnp.concatenate([s[:, :, 0], z], axis=-1),
                            jnp.concatenate([z, s[:, :, 1]], axis=-1)], axis=-2)


def _head_states(s2):
    b, p = s2.shape[:2]
    return jnp.stack([s2[:, :, :HEAD_A, :HEAD_A], s2[:, :, HEAD_A:, HEAD_A:]],
                     axis=2).reshape(b, 2 * p, HEAD_A, HEAD_A)


def _layer(x3, modm, group0, grid_mode, s0f, s0b, w, tc_tokens):
    b, t, d = x3.shape
    n = b * t
    x = x3.reshape(n, d)
    rows_per_group = t if grid_mode else n
    z = _in_proj(x, modm, w["g_pre1"], w["w_in"], group0, rows_per_group)
    zs = _shift_mix(z, w["mu_shift"], w["n_shift"], w["lora_block"], grid_mode, t)
    zero_init = s0f is None
    if zero_init:
        s0f = s0b = jnp.zeros((b, w["w0"].shape[1] // LANES, LANES, LANES), F32)
    yf, yb, sf, sb = _rwkv_scan(zs.reshape(b, t, -1), w["w0"], w["w2"], w["a0"], w["a2"], w["k_k"], w["k_a"],
                                s0f, s0b, zero_init)
    d_a = yf.shape[-1]
    yag = _post_mix(yf.reshape(n, d_a), yb.reshape(n, d_a), zs, z, w["a0"], w["a2"], w["k_a"], w["r_k"],
                    w["ln_x_w"], w["ln_x_b"], w["ln_v_w"], w["ln_v_b"], w["w_s"], w["b_st"])
    x1, h2 = _out_proj(yag, w["w_out"], x, modm, w["g_post1"], w["g_pre2"], group0, rows_per_group)
    scores = _query_scores(h2, w["w_query"], w["sub_keys"])
    idx_t, gates_t = _peer_topk(scores)
    idx, gates = idx_t.T, gates_t.T
    peer_tc = _peer_experts_tc(idx, h2, gates, w["expert_rows"], tc_tokens)
    peer_sc = _peer_experts_sc(idx, h2, gates, w["experts"], tc_tokens)
    out = _final(x1, peer_tc, peer_sc, modm, w["g_post2"], group0, rows_per_group)
    return out.reshape(b, t, d), sf, sb


def kernel(x_prompt, x_sample, c, state_fwd, state_bwd, c_ctx, w_ada, b_ada, g_pre1, g_post1, g_pre2, g_post2,
           w_in, mu_shift, w0, w2, a0, a2, k_k, k_a, r_k, ln_x_w, ln_x_b, ln_v_w, ln_v_b, w_s, b_s, w_out,
           w_query, sub_keys, expert_u, expert_v):
    depth = w_in.shape[0]
    d = x_prompt.shape[-1]
    d_a = w0.shape[-1]
    n_shift = mu_shift.shape[-1]
    dec_b = x_sample.shape[0]
    cvec = jnp.concatenate([c_ctx[None], c, jnp.zeros((8 - 1 - dec_b, d), F32)], axis=0)
    xp, xs = x_prompt, x_sample
    new_f, new_b = [], []
    for l in range(depth):
        wl_in = w_in[l]
        row = lambda a: a[l].reshape(1, -1)
        w = {
            "w_in": jnp.concatenate([wl_in[:, :4 * d_a], wl_in[:, n_shift:], wl_in[:, 4 * d_a:n_shift]],
                                    axis=1).astype(BF16),
            "n_shift": n_shift,
            "lora_block": (wl_in.shape[1] - (n_shift - 4 * d_a)) // (2 * LANES),
            "mu_shift": row(mu_shift),
            "g_pre1": row(g_pre1), "g_post1": row(g_post1), "g_pre2": row(g_pre2), "g_post2": row(g_post2),
            "w0": w0[l], "w2": w2[l].reshape(-1, d_a), "a0": a0[l], "a2": a2[l].reshape(-1, d_a),
            "k_k": row(k_k), "k_a": row(k_a), "r_k": row(r_k),
            "ln_x_w": row(ln_x_w), "ln_x_b": row(ln_x_b), "ln_v_w": row(ln_v_w), "ln_v_b": row(ln_v_b),
            "w_s": w_s[l].astype(BF16), "b_st": b_s[l].T,
            "w_out": w_out[l].astype(BF16), "w_query": w_query[l].astype(BF16),
            "sub_keys": jnp.swapaxes(sub_keys[l], 0, 1).reshape(-1, N_KEYS, sub_keys.shape[-1]).astype(BF16),
            "experts": _pack_experts(expert_u[l], expert_v[l]),
        }
        w["expert_rows"] = w["experts"].reshape(-1, LANES)
        modm = _modulation(cvec, w_ada[l], b_ada[l].reshape(1, -1)).reshape(8, 6, d)
        piece = xp.shape[0] // CTX_PARTS
        parts = [_layer(xp[i * piece:(i + 1) * piece], modm, 0, False, None, None, w, PEER_TC_CTX)
                 for i in range(CTX_PARTS)]
        xp = jnp.concatenate([p[0] for p in parts], axis=0)
        new_f.append(_head_states(jnp.concatenate([p[1] for p in parts], axis=0)))
        new_b.append(_head_states(jnp.concatenate([p[2] for p in parts], axis=0)))
        xs, _, _ = _layer(xs, modm, 1, True, _block_diag_state(state_fwd[:, l]),
                          _block_diag_state(state_bwd[:, l]), w, PEER_TC_LATENT)
    return (xp, xs, jnp.stack(new_f, axis=1), jnp.stack(new_b, axis=1))
```

```python
import functools

import jax
import jax.numpy as jnp
from jax import lax
from jax.experimental import pallas as pl
from jax.experimental.pallas import tpu as pltpu
from jax.experimental.pallas import tpu_sc as plsc

F32 = jnp.float32
BF16 = jnp.bfloat16
I32 = jnp.int32

EPS = 1e-6
GN_EPS = 64e-5
HEAD_A = 64
LANES = 128
GRID_W = 64
GMLP_CHUNK = 128
PK_HEADS = 8
N_KEYS = 128
TOPK = 16
SCAN_CHUNK = 64
SCAN_PAIRS = 8
DECAY_SCALE = 0.6065306597126334
VMEM_LIMIT = 48 * 1024 * 1024


def _params(sem):
    return pltpu.CompilerParams(dimension_semantics=sem, vmem_limit_bytes=VMEM_LIMIT)


def _sigmoid(x):
    return 1.0 / (1.0 + jnp.exp(-x))


def _gelu(x):
    return 0.5 * x * (1.0 + jnp.tanh(0.7978845608028654 * (x + 0.044715 * (x * x * x))))


def _dot(a, b):
    return jnp.dot(a.astype(BF16), b.astype(BF16), preferred_element_type=F32)


def _dot_nt(a, b):
    return lax.dot_general(a.astype(BF16), b.astype(BF16), (((1,), (1,)), ((), ())),
                           preferred_element_type=F32)


def _dot_tn(a, b):
    return lax.dot_general(a.astype(BF16), b.astype(BF16), (((0,), (0,)), ((), ())),
                           preferred_element_type=F32)


def _split(x):
    hi = x.astype(BF16)
    return hi, (x - hi.astype(F32)).astype(BF16)


def _dot_x3(a, b):
    a_hi, a_lo = _split(a)
    b_hi, b_lo = _split(b)
    dot = functools.partial(jnp.dot, preferred_element_type=F32)
    return dot(a_hi, b_hi) + (dot(a_lo, b_hi) + dot(a_hi, b_lo))


def _dot_split_rhs(a_bf, b):
    b_hi, b_mid = _split(b)
    b_lo = (b - b_hi.astype(F32) - b_mid.astype(F32)).astype(BF16)
    dot = functools.partial(jnp.dot, preferred_element_type=F32)
    return dot(a_bf, b_hi) + (dot(a_bf, b_mid) + dot(a_bf, b_lo))


def _head_sum(x, first_head):
    s_a = jnp.sum(jnp.where(first_head, x, 0.0), axis=-1, keepdims=True)
    s_b = jnp.sum(jnp.where(first_head, 0.0, x), axis=-1, keepdims=True)
    return jnp.where(first_head, s_a, s_b)


def _mod_kernel(c_ref, w_ref, b_ref, o_ref):
    c = c_ref[...]
    o_ref[...] = _dot(c * _sigmoid(c), w_ref[...]) + b_ref[...]


def _modulation(cvec, w_ada, b_ada):
    rows, d = cvec.shape
    n = w_ada.shape[1]
    tn = 1024
    return pl.pallas_call(
        _mod_kernel,
        out_shape=jax.ShapeDtypeStruct((rows, n), F32),
        grid=(n // tn,),
        in_specs=[pl.BlockSpec((rows, d), lambda j: (0, 0)),
                  pl.BlockSpec((d, tn), lambda j: (0, j)),
                  pl.BlockSpec((1, tn), lambda j: (0, j))],
        out_specs=pl.BlockSpec((rows, tn), lambda j: (0, j)),
        compiler_params=_params(("parallel",)),
        name="adaln_mod",
    )(cvec, w_ada, b_ada)


def _in_proj_kernel(x_ref, mod_ref, g_ref, w_ref, o_ref, h_ref):
    @pl.when(pl.program_id(1) == 0)
    def _():
        x = x_ref[...]
        y = x * lax.rsqrt(jnp.mean(x * x, axis=-1, keepdims=True) + EPS) * g_ref[...]
        h_ref[...] = (y * (1.0 + mod_ref[0, 1:2, :]) + mod_ref[0, 0:1, :]).astype(BF16)

    o_ref[...] = jnp.dot(h_ref[...], w_ref[...], preferred_element_type=F32)


def _in_proj(x, modm, g_pre, w_bf, group0, rows_per_group):
    n, d = x.shape
    p = w_bf.shape[1]
    tm, tn = 512, 1280
    grp = lambda i, j: (group0 + (i * tm) // rows_per_group, 0, 0)
    return pl.pallas_call(
        _in_proj_kernel,
        out_shape=jax.ShapeDtypeStruct((n, p), F32),
        grid=(n // tm, p // tn),
        in_specs=[pl.BlockSpec((tm, d), lambda i, j: (i, 0)),
                  pl.BlockSpec((1, 6, d), grp),
                  pl.BlockSpec((1, d), lambda i, j: (0, 0)),
                  pl.BlockSpec((d, tn), lambda i, j: (0, j))],
        out_specs=pl.BlockSpec((tm, tn), lambda i, j: (i, j)),
        scratch_shapes=[pltpu.VMEM((tm, d), BF16)],
        compiler_params=_params(("parallel", "arbitrary")),
        name="in_proj",
    )(x, modm, g_pre, w_bf)


def _shift_kernel(z_ref, mu_ref, o_ref, *, grid_mode, period):
    z = z_ref[...]
    rows = z.shape[0]
    t = lax.broadcasted_iota(I32, (rows, 1), 0) % period
    prev = jnp.where(t % (GRID_W if grid_mode else period) != 0, pltpu.roll(z, 1, 0), 0.0)
    nxt = jnp.where(t % (GRID_W if grid_mode else period) != (GRID_W if grid_mode else period) - 1,
                    pltpu.roll(z, rows - 1, 0), 0.0)
    if grid_mode:
        up = jnp.where(t >= GRID_W, pltpu.roll(z, GRID_W, 0), 0.0)
        down = jnp.where(t < period - GRID_W, pltpu.roll(z, rows - GRID_W, 0), 0.0)
        nb = 0.25 * (up + down + prev + nxt)
    else:
        nb = 0.5 * (prev + nxt)
    o_ref[...] = z + mu_ref[...] * (nb - z)


def _shift_mix(z, mu, n_shift, lora_block, grid_mode, period):
    n = z.shape[0]
    tr, tc = 2048, 256
    main_blocks = (n_shift // tc) - 1
    col = lambda i, j: (i, jnp.where(j < main_blocks, j, lora_block))
    return pl.pallas_call(
        functools.partial(_shift_kernel, grid_mode=grid_mode, period=period),
        out_shape=jax.ShapeDtypeStruct((n, n_shift), F32),
        grid=(n // tr, n_shift // tc),
        in_specs=[pl.BlockSpec((tr, tc), col),
                  pl.BlockSpec((1, tc), lambda i, j: (0, j))],
        out_specs=pl.BlockSpec((tr, tc), lambda i, j: (i, j)),
        compiler_params=_params(("parallel", "parallel")),
        name="token_shift",
    )(z, mu)


def _scan_chunks(chains):
    c = chains[0][0].shape[0]
    c2 = 2 * c
    n = len(chains)
    fwd = [ch[9] for ch in chains]
    first_head = lax.broadcasted_iota(I32, (1, LANES), 1) < HEAD_A
    row = lax.broadcasted_iota(I32, (c2, c2), 0)
    col = lax.broadcasted_iota(I32, (c2, c2), 1)
    eye = jnp.where(row == col, 1.0, 0.0)

    def stack(x):
        return jnp.concatenate([jnp.where(first_head, x, 0.0), jnp.where(first_head, 0.0, x)],
                               axis=0).astype(BF16)

    lhs, rhs, v2, total = [], [], [], []
    for r, k, v, logw, cl, a, k_k, k_a, _, forward in chains:
        kkr = k * k_k
        kk = kkr / jnp.maximum(jnp.sqrt(_head_sum(kkr * kkr, first_head)), 1e-12)
        kd = k * (1.0 + (a - 1.0) * k_a)
        gi = jnp.exp(-cl)
        lhs.append(jnp.concatenate([stack(kk * jnp.exp(cl - logw)), stack(r * jnp.exp(cl))], axis=0))
        rhs.append(jnp.concatenate([stack(kk * a * gi), stack(kd * gi)], axis=0))
        v2.append(stack(v))
        total.append(cl[c - 1:c, :] if forward else cl[0:1, :])

    res = [_dot_nt(lhs[i], rhs[i]) for i in range(n)]
    pr = [_dot_nt(lhs[i], chains[i][8]) for i in range(n)]
    strict = [(row > col) if f else (row < col) for f in fwd]
    incl = [(row >= col) if f else (row <= col) for f in fwd]
    ab = [jnp.where(strict[i], res[i][:c2, :c2], 0.0) for i in range(n)]
    ak = [jnp.where(strict[i], res[i][:c2, c2:], 0.0) for i in range(n)]
    gb_gk = [jnp.concatenate([jnp.where(incl[i], res[i][c2:, :c2], 0.0),
                              jnp.where(incl[i], res[i][c2:, c2:], 0.0)], axis=1).astype(BF16) for i in range(n)]
    akv = [_dot(ak[i], v2[i]) for i in range(n)]

    tinv = [eye - ab[i] for i in range(n)]
    pw = [-ab[i] for i in range(n)]
    for _ in range(c.bit_length() - 2):
        pw = [_dot(pw[i], pw[i]) for i in range(n)]
        tinv = [tinv[i] + _dot(tinv[i], pw[i]) for i in range(n)]

    u2 = [_dot(tinv[i], -pr[i][:c2, :] - akv[i]) for i in range(n)]
    uv = [jnp.concatenate([u2[i].astype(BF16), v2[i]], axis=0) for i in range(n)]
    y2 = [pr[i][c2:, :] + _dot(gb_gk[i], uv[i]) for i in range(n)]
    s_new = [(chains[i][8] + _dot_tn(uv[i], rhs[i])) * jnp.exp(total[i]) for i in range(n)]
    return [(y2[i][:c, :] + y2[i][c:, :], s_new[i]) for i in range(n)]


def _scan_direction(r_ref, k_ref, v_ref, l_ref, d, s_ref, w0, w2, a0, a2, kk, ka, forward):
    lora = l_ref[0]
    c = lora.shape[0]
    dir_rows = (lax.broadcasted_iota(I32, (LANES, 1), 0) // HEAD_A) == d
    wl = w0[d:d + 1, :] + _dot_x3(jnp.tanh(lora[:, :LANES]), jnp.where(dir_rows, w2[...], 0.0))
    logw = -DECAY_SCALE * _sigmoid(wl)
    a = _sigmoid(a0[d:d + 1, :] + _dot_x3(lora[:, LANES:], jnp.where(dir_rows, a2[...], 0.0)))
    ti = lax.broadcasted_iota(I32, (c, c), 0)
    tj = lax.broadcasted_iota(I32, (c, c), 1)
    cum = jnp.where((tj <= ti) if forward else (tj >= ti), 1.0, 0.0).astype(BF16)
    cl = _dot_split_rhs(cum, logw)
    chains = []
    for p in range(s_ref.shape[0]):
        sl = slice(p * LANES, (p + 1) * LANES)
        chains.append((r_ref[0, :, sl], k_ref[0, :, sl], v_ref[0, :, sl], logw[:, sl], cl[:, sl],
                       a[:, sl], kk[:, sl], ka[:, sl], s_ref[p], forward))
    return chains


def _scan_store(out, y_ref, s_ref):
    for p, (y, s_new) in enumerate(out):
        y_ref[0, :, p * LANES:(p + 1) * LANES] = y
        s_ref[p] = s_new


def _scan_kernel(rf, kf, vf, lf, rb, kb, vb, lb, w0, w2, a0, a2, kk, ka, s0f, s0b,
                 yf, yb, sf, sb, s2f, s2b, *, zero_init):
    c = pl.program_id(2)

    @pl.when(c == 0)
    def _():
        if zero_init:
            s2f[...] = jnp.zeros_like(s2f)
            s2b[...] = jnp.zeros_like(s2b)
        else:
            s2f[...] = s0f[0]
            s2b[...] = s0b[0]

    chains_f = _scan_direction(rf, kf, vf, lf, 0, s2f, w0, w2, a0, a2, kk, ka, True)
    chains_b = _scan_direction(rb, kb, vb, lb, 1, s2b, w0, w2, a0, a2, kk, ka, False)
    out = _scan_chunks(chains_f + chains_b)
    _scan_store(out[:len(chains_f)], yf, s2f)
    _scan_store(out[len(chains_f):], yb, s2b)

    @pl.when(c == pl.num_programs(2) - 1)
    def _():
        sf[0] = s2f[...]
        sb[0] = s2b[...]


def _rwkv_scan(zs, w0, w2r, a0, a2r, k_k, k_a, s0f, s0b, zero_init):
    b, t, _ = zs.shape
    d_a = w0.shape[1]
    pp = SCAN_PAIRS
    wide = pp * LANES
    groups = d_a // wide
    c = SCAN_CHUNK
    nc = t // c
    lora_blk = (4 * d_a) // (2 * LANES)
    fw = lambda off: (lambda i, q, j: (i, j, off + q))
    bw = lambda off: (lambda i, q, j: (i, nc - 1 - j, off + q))
    par = lambda i, q, j: (0, q)
    st = lambda i, q, j: (i, q, 0, 0)
    blk = (1, c, wide)
    lblk = (1, c, 2 * LANES)
    sblk = (1, pp, LANES, LANES)
    yshape = jax.ShapeDtypeStruct((b, t, d_a), F32)
    sshape = jax.ShapeDtypeStruct((b, d_a // LANES, LANES, LANES), F32)
    return pl.pallas_call(
        functools.partial(_scan_kernel, zero_init=zero_init),
        out_shape=(yshape, yshape, sshape, sshape),
        grid=(b, groups, nc),
        in_specs=[pl.BlockSpec(blk, fw(0)), pl.BlockSpec(blk, fw(groups)), pl.BlockSpec(blk, fw(2 * groups)),
                  pl.BlockSpec(lblk, lambda i, q, j: (i, j, lora_blk)),
                  pl.BlockSpec(blk, bw(0)), pl.BlockSpec(blk, bw(groups)), pl.BlockSpec(blk, bw(2 * groups)),
                  pl.BlockSpec(lblk, lambda i, q, j: (i, nc - 1 - j, lora_blk)),
                  pl.BlockSpec((2, wide), par), pl.BlockSpec((LANES, wide), par),
                  pl.BlockSpec((2, wide), par), pl.BlockSpec((LANES, wide), par),
                  pl.BlockSpec((1, wide), par), pl.BlockSpec((1, wide), par),
                  pl.BlockSpec(sblk, st), pl.BlockSpec(sblk, st)],
        out_specs=(pl.BlockSpec(blk, fw(0)), pl.BlockSpec(blk, bw(0)),
                   pl.BlockSpec(sblk, st), pl.BlockSpec(sblk, st)),
        scratch_shapes=[pltpu.VMEM((pp, LANES, LANES), F32), pltpu.VMEM((pp, LANES, LANES), F32)],
        compiler_params=_params(("parallel", "parallel", "arbitrary")),
        name="rwkv7_scan",
    )(zs, zs, zs, zs, zs, zs, zs, zs, w0, w2r, a0, a2r, k_k, k_a, s0f, s0b)


def _post_kernel(yf, yb, r, k, v, g, lora, u, vg, a0, a2, ka, rk, lxw, lxb, lvw, lvb, ws, bst, o_ref):
    tm = yf.shape[0]
    d_a = yf.shape[1]
    first_head = lax.broadcasted_iota(I32, (1, LANES), 1) < HEAD_A
    dir_row = lax.broadcasted_iota(I32, (LANES, 1), 0) // HEAD_A
    la = lora[:, LANES:]
    a_sum = jnp.zeros((tm, d_a), F32)
    for d in range(2):
        a_sum = a_sum + _sigmoid(a0[d:d + 1, :] + _dot_x3(la, jnp.where(dir_row == d, a2[...], 0.0)))
    rkk = r[...] * k[...] * (2.0 + (a_sum - 2.0) * ka[...]) * rk[...]
    y = yf[...] + yb[...]
    inv = 1.0 / HEAD_A
    for j in range(d_a // LANES):
        sl = slice(j * LANES, (j + 1) * LANES)
        yj = y[:, sl]
        mu = _head_sum(yj, first_head) * inv
        dl = yj - mu
        var = _head_sum(dl * dl, first_head) * inv
        yn = dl * lax.rsqrt(var + GN_EPS) * lxw[:, sl] + lxb[:, sl]
        bonus = _head_sum(rkk[:, sl], first_head) * v[:, sl]
        o_ref[:, sl] = ((yn + bonus) * _sigmoid(g[:, sl])).astype(BF16)

    uu = _gelu(u[...])
    vv = _gelu(vg[...])
    mu = jnp.mean(vv, axis=-1, keepdims=True)
    dv = vv - mu
    vn = dv * lax.rsqrt(jnp.mean(dv * dv, axis=-1, keepdims=True) + EPS) * lvw[...] + lvb[...]
    for ch in range(tm // GMLP_CHUNK):
        rows = slice(ch * GMLP_CHUNK, (ch + 1) * GMLP_CHUNK)
        for h in range(ws.shape[0]):
            cols = slice(h * LANES, (h + 1) * LANES)
            sp = _dot(ws[h], vn[rows, cols]) + bst[:, h:h + 1]
            o_ref[rows, d_a + h * LANES:d_a + (h + 1) * LANES] = (uu[rows, cols] * sp).astype(BF16)


def _post_mix(yf, yb, zs, z, a0, a2r, k_a, r_k, lxw, lxb, lvw, lvb, w_s, b_st):
    n, d_a = yf.shape
    tm = 256
    wide = lambda j: (lambda i: (i, j))
    full = lambda shape: pl.BlockSpec(shape, lambda i: (0,) * len(shape))
    lora_blk = (4 * d_a) // (2 * LANES)
    return pl.pallas_call(
        _post_kernel,
        out_shape=jax.ShapeDtypeStruct((n, 2 * d_a), BF16),
        grid=(n // tm,),
        in_specs=[pl.BlockSpec((tm, d_a), wide(0)), pl.BlockSpec((tm, d_a), wide(0)),
                  pl.BlockSpec((tm, d_a), wide(0)), pl.BlockSpec((tm, d_a), wide(1)),
                  pl.BlockSpec((tm, d_a), wide(2)), pl.BlockSpec((tm, d_a), wide(3)),
                  pl.BlockSpec((tm, 2 * LANES), wide(lora_blk)),
                  pl.BlockSpec((tm, d_a), wide(4)), pl.BlockSpec((tm, d_a), wide(5)),
                  full((2, d_a)), full((LANES, d_a)), full((1, d_a)), full((1, d_a)),
                  full((1, d_a)), full((1, d_a)), full((1, d_a)), full((1, d_a)),
                  full(w_s.shape), full(b_st.shape)],
        out_specs=pl.BlockSpec((tm, 2 * d_a), wide(0)),
        compiler_params=_params(("parallel",)),
        name="mix_post",
    )(yf, yb, zs, zs, zs, zs, zs, z, z, a0, a2r, k_a, r_k, lxw, lxb, lvw, lvb, w_s, b_st)


def _rms(x, g):
    return x * lax.rsqrt(jnp.mean(x * x, axis=-1, keepdims=True) + EPS) * g


def _out_proj_kernel(a_ref, w_ref, x_ref, mod_ref, gpost_ref, gpre_ref, x1_ref, h2_ref):
    o = jnp.dot(a_ref[...], w_ref[...], preferred_element_type=F32)
    x1 = x_ref[...] + mod_ref[0, 2:3, :] * _rms(o, gpost_ref[...])
    x1_ref[...] = x1
    h2_ref[...] = _rms(x1, gpre_ref[...]) * (1.0 + mod_ref[0, 4:5, :]) + mod_ref[0, 3:4, :]


def _out_proj(yag, w_bf, x, modm, g_post, g_pre2, group0, rows_per_group):
    n, d = x.shape
    tm = 256
    grp = lambda i: (group0 + (i * tm) // rows_per_group, 0, 0)
    row = lambda i: (i, 0)
    fix = lambda i: (0, 0)
    shp = jax.ShapeDtypeStruct((n, d), F32)
    return pl.pallas_call(
        _out_proj_kernel,
        out_shape=(shp, shp),
        grid=(n // tm,),
        in_specs=[pl.BlockSpec((tm, yag.shape[1]), row), pl.BlockSpec(w_bf.shape, fix),
                  pl.BlockSpec((tm, d), row), pl.BlockSpec((1, 6, d), grp),
                  pl.BlockSpec((1, d), fix), pl.BlockSpec((1, d), fix)],
        out_specs=(pl.BlockSpec((tm, d), row), pl.BlockSpec((tm, d), row)),
        compiler_params=_params(("parallel",)),
        name="out_proj",
    )(yag, w_bf, x, modm, g_post, g_pre2)


def _query_kernel(h_ref, wq_ref, sk_ref, o_ref):
    q = jnp.dot(h_ref[...].astype(BF16), wq_ref[...], preferred_element_type=F32)
    for g in range(sk_ref.shape[0]):
        o_ref[g * N_KEYS:(g + 1) * N_KEYS, :] = _dot_nt(sk_ref[g], q[:, g * LANES:(g + 1) * LANES])


def _query_scores(h2, wq_bf, sk):
    n, d = h2.shape
    tm = 256
    groups = sk.shape[0]
    return pl.pallas_call(
        _query_kernel,
        out_shape=jax.ShapeDtypeStruct((groups * N_KEYS, n), F32),
        grid=(n // tm,),
        in_specs=[pl.BlockSpec((tm, d), lambda i: (i, 0)),
                  pl.BlockSpec(wq_bf.shape, lambda i: (0, 0)),
                  pl.BlockSpec(sk.shape, lambda i: (0, 0, 0))],
        out_specs=pl.BlockSpec((groups * N_KEYS, tm), lambda i: (0, i)),
        compiler_params=_params(("parallel",)),
        name="peer_query",
    )(h2, wq_bf, sk)


def _top16(s, pos=None):
    if pos is None:
        pos = lax.broadcasted_iota(I32, s.shape, 0)
    big = jnp.int32(2 ** 30)
    vals, idxs = [], []
    for _ in range(TOPK):
        m = jnp.max(s, axis=0, keepdims=True)
        i = jnp.min(jnp.where(s == m, pos, big), axis=0, keepdims=True)
        vals.append(m)
        idxs.append(i)
        s = jnp.where(pos == i, -jnp.inf, s)
    return jnp.concatenate(vals, axis=0), jnp.concatenate(idxs, axis=0)


def _pair_candidates(v1, v2):
    r8 = lax.broadcasted_iota(I32, (8, 1), 0)
    r16 = lax.broadcasted_iota(I32, (TOPK, 1), 0)
    ninf = -jnp.inf
    blocks = [
        (v1[0:1, :] + v2, r16),
        (v1[1:2, :] + v2[0:8, :], TOPK + r8),
        (jnp.where(r8 < 5, v1[2:3, :] + v2[0:8, :], ninf), 2 * TOPK + r8),
        (jnp.where(r8 < 4, v1[3:4, :] + v2[0:8, :], ninf), 3 * TOPK + r8),
        (jnp.where(r16 >= 4, v1 + v2[0:1, :], ninf), r16 * TOPK),
        (jnp.where(r8 >= 4, v1[0:8, :] + v2[1:2, :], ninf), r8 * TOPK + 1),
        (jnp.where(r8 == 4, v1[0:8, :] + v2[2:3, :], ninf), r8 * TOPK + 2),
    ]
    cand = jnp.concatenate([b for b, _ in blocks], axis=0)
    pos = jnp.concatenate([p for _, p in blocks], axis=0)
    return cand, jnp.broadcast_to(pos, cand.shape)


def _topk_kernel(s_ref, idx_ref, gate_ref):
    def head(h, carry):
        base = pl.multiple_of(h * (2 * N_KEYS), 2 * N_KEYS)
        v1, i1 = _top16(s_ref[pl.ds(base, N_KEYS), :])
        v2, i2 = _top16(s_ref[pl.ds(base + N_KEYS, N_KEYS), :])
        top_s, pos = _top16(*_pair_candidates(v1, v2))
        pi = pos // TOPK
        pj = pos % TOPK
        e1 = jnp.zeros_like(pos)
        e2 = jnp.zeros_like(pos)
        for i in range(TOPK):
            e1 = e1 + jnp.where(pi == i, i1[i:i + 1, :], 0)
            e2 = e2 + jnp.where(pj == i, i2[i:i + 1, :], 0)
        ex = jnp.exp(top_s - top_s[0:1, :])
        out = pl.multiple_of(h * TOPK, TOPK)
        idx_ref[pl.ds(out, TOPK), :] = e1 * N_KEYS + e2
        gate_ref[pl.ds(out, TOPK), :] = ex / jnp.sum(ex, axis=0, keepdims=True)
        return carry

    lax.fori_loop(0, PK_HEADS, head, 0)


def _peer_topk(scores):
    rows, n = scores.shape
    tt = 256
    out_rows = PK_HEADS * TOPK
    return pl.pallas_call(
        _topk_kernel,
        out_shape=(jax.ShapeDtypeStruct((out_rows, n), I32), jax.ShapeDtypeStruct((out_rows, n), F32)),
        grid=(n // tt,),
        in_specs=[pl.BlockSpec((rows, tt), lambda i: (0, i))],
        out_specs=(pl.BlockSpec((out_rows, tt), lambda i: (0, i)),
                   pl.BlockSpec((out_rows, tt), lambda i: (0, i))),
        compiler_params=_params(("parallel",)),
        name="peer_topk",
    )(scores)


SC_CORES = 2
SC_SUBCORES = 16
SC_LANES = 16
PEER_GROUP = 16
PEER_BLOCK = 16
U_MASK = -65536


def _pack_experts(eu, ev):
    hi = lax.bitcast_convert_type(eu.astype(BF16), jnp.uint16).astype(jnp.uint32)
    lo = lax.bitcast_convert_type(ev.astype(BF16), jnp.uint16).astype(jnp.uint32)
    return lax.bitcast_convert_type((hi << 16) | lo, I32)


def _unpack_u(w):
    return lax.bitcast_convert_type(w & U_MASK, F32)


def _unpack_v(w):
    return lax.bitcast_convert_type(w << 16, F32)


def _peer_experts_sc(idx, x, gates, table, start):
    d = x.shape[1]
    n = x.shape[0] - start
    n_sel = idx.shape[1]
    per_w = n // (SC_CORES * SC_SUBCORES)
    L, G, TB = SC_LANES, PEER_GROUP, PEER_BLOCK
    n_groups = n_sel // G
    n_ch = d // L
    mesh = plsc.VectorSubcoreMesh(core_axis_name="c", subcore_axis_name="s")

    @functools.partial(
        pl.kernel, mesh=mesh, out_type=jax.ShapeDtypeStruct((n, d), F32),
        compiler_params=pltpu.CompilerParams(needs_layout_passes=False),
        scratch_types=[pltpu.VMEM((TB, n_sel), I32), pltpu.VMEM((TB, n_sel), F32),
                       pltpu.VMEM((2, d), F32), pltpu.VMEM((2, d), F32),
                       pltpu.VMEM((2, G, d), I32), pltpu.VMEM((G, 2 * L), F32), pltpu.VMEM((2 * L,), F32),
                       pltpu.SemaphoreType.DMA((2,)), pltpu.SemaphoreType.DMA((2,)), pltpu.SemaphoreType.DMA((2,))],
        name="peer_experts_sc")
    def k(idx_hbm, x_hbm, g_hbm, tab_hbm, o_hbm, idx_v, gate_v, x_v, o_v, buf, acc_v, coef_v, sem, xsem, osem):
        out_base = (lax.axis_index("s") * SC_CORES + lax.axis_index("c")) * per_w
        base = start + out_base
        lane = lax.iota(I32, L)
        zero = jnp.zeros((L,), F32)

        def gather(t, g, slot):
            return pltpu.make_async_copy(tab_hbm.at[idx_v.at[t, pl.ds(g * G, G)]], buf.at[slot], sem.at[slot])

        def x_copy(j, par):
            return pltpu.make_async_copy(x_hbm.at[base + j], x_v.at[par], xsem.at[par])

        def o_copy(j, par):
            return pltpu.make_async_copy(o_v.at[par], o_hbm.at[out_base + j], osem.at[par])

        x_copy(0, 0).start()

        @pl.loop(0, per_w // TB)
        def _(blk):
            tok0 = base + blk * TB
            pltpu.sync_copy(idx_hbm.at[pl.ds(tok0, TB)], idx_v)
            pltpu.sync_copy(g_hbm.at[pl.ds(tok0, TB)], gate_v)

            @pl.loop(0, TB)
            def _(t):
                j = blk * TB + t
                par = j % 2
                gather(t, 0, 0).start()

                @pl.when(j + 1 < per_w)
                def _():
                    x_copy(j + 1, 1 - par).start()

                x_copy(j, par).wait()

                @pl.when(j >= 2)
                def _():
                    o_copy(j - 2, par).wait()

                for g in range(n_groups):
                    slot = g % 2
                    if g + 1 < n_groups:
                        gather(t, g + 1, 1 - slot).start()
                    gather(t, g, slot).wait()

                    def u_body(ch, accs):
                        xv = x_v[par, pl.ds(ch * L, L)]
                        return tuple(accs[e] + _unpack_u(buf[slot, e, pl.ds(ch * L, L)]) * xv for e in range(G))

                    accs = plsc.parallel_loop(0, n_ch, carry=tuple(zero for _ in range(G)))(u_body)
                    for e in range(G):
                        acc_v[e, pl.ds(L, L)] = accs[e]
                    act = zero
                    for l in range(L):
                        act = act + plsc.load_gather(acc_v, [lane, jnp.full((L,), L + l, I32)])
                    y = 0.7978845608028654 * (act + 0.044715 * (act * act * act))
                    coef_v[pl.ds(L, L)] = act / (1.0 + jnp.exp(-2.0 * y)) * gate_v[t, pl.ds(g * G, G)]
                    cs = [plsc.load_gather(coef_v, [jnp.full((L,), L + e, I32)]) for e in range(G)]

                    @plsc.parallel_loop(0, n_ch)
                    def _(ch):
                        o = zero if g == 0 else o_v[par, pl.ds(ch * L, L)]
                        for e in range(G):
                            o = o + cs[e] * _unpack_v(buf[slot, e, pl.ds(ch * L, L)])
                        o_v[par, pl.ds(ch * L, L)] = o

                o_copy(j, par).start()

        o_copy(per_w - 2, 0).wait()
        o_copy(per_w - 1, 1).wait()

    return k(idx, x, gates, table)


PEER_TC_TOKENS = 64
CTX_PARTS = 4
PEER_TC_SLOTS = 3
PEER_TC_CTX = 512
PEER_TC_LATENT = 3584


def _peer_tc_kernel(idx_ref, x_ref, g_ref, uv_ref, o_ref, buf, sem):
    slots, rows, n_sel = buf.shape[0], buf.shape[1], buf.shape[2]
    tokens = g_ref.shape[0]
    ahead = slots - 1

    def issue(t):
        slot = t % slots

        def body(e, carry):
            src = pl.multiple_of(idx_ref[t, e] * rows, rows)
            pltpu.make_async_copy(uv_ref.at[pl.ds(src, rows), :], buf.at[slot, :, e, :], sem.at[slot]).start()
            return carry
        lax.fori_loop(0, n_sel, body, 0, unroll=8)

    def wait(slot):
        pltpu.make_async_copy(buf.at[(slot + 1) % slots], buf.at[slot], sem.at[slot]).wait()

    for t0 in range(ahead):
        issue(t0)
    g_t = g_ref[...].T
    g_hi, g_mid = _split(g_t)
    g_lo = (g_t - g_hi.astype(F32) - g_mid.astype(F32)).astype(BF16)
    tok_iota = lax.broadcasted_iota(I32, (tokens, LANES), 0)
    dot = functools.partial(jnp.dot, preferred_element_type=F32)

    def token(t, carry):
        slot = t % slots

        @pl.when(t + ahead < tokens)
        def _():
            issue(t + ahead)

        wait(slot)
        pick = jnp.where(tok_iota == t, 1.0, 0.0).astype(BF16)
        gate = dot(g_hi, pick) + (dot(g_mid, pick) + dot(g_lo, pick))
        xt = x_ref[pl.ds(t, 1), :]
        acc = _unpack_u(buf[slot, 0]) * xt[:, 0:LANES]
        for c in range(1, rows):
            acc = acc + _unpack_u(buf[slot, c]) * xt[:, c * LANES:(c + 1) * LANES]
        act = jnp.sum(acc, axis=-1, keepdims=True)
        coef = _gelu(act) * gate
        o_ref[pl.ds(t, 1), :] = jnp.concatenate(
            [jnp.sum(coef * _unpack_v(buf[slot, c]), axis=0, keepdims=True) for c in range(rows)], axis=1)
        return carry

    lax.fori_loop(0, tokens, token, 0)


def _peer_experts_tc(idx, x, gates, table_rows, m):
    d = x.shape[1]
    n_sel = idx.shape[1]
    tt = PEER_TC_TOKENS
    return pl.pallas_call(
        _peer_tc_kernel,
        out_shape=jax.ShapeDtypeStruct((m, d), F32),
        grid=(m // tt,),
        in_specs=[pl.BlockSpec((tt, n_sel), lambda i: (i, 0), memory_space=pltpu.SMEM),
                  pl.BlockSpec((tt, d), lambda i: (i, 0)),
                  pl.BlockSpec((tt, n_sel), lambda i: (i, 0)),
                  pl.BlockSpec(memory_space=pl.ANY)],
        out_specs=pl.BlockSpec((tt, d), lambda i: (i, 0)),
        scratch_shapes=[pltpu.VMEM((PEER_TC_SLOTS, d // LANES, n_sel, LANES), I32),
                        pltpu.SemaphoreType.DMA((PEER_TC_SLOTS,))],
        compiler_params=pltpu.CompilerParams(dimension_semantics=("arbitrary",),
                                             vmem_limit_bytes=VMEM_LIMIT, disable_bounds_checks=True),
        name="peer_experts_tc",
    )(idx, x, gates, table_rows)


def _final_kernel(x1_ref, ptc_ref, psc_ref, mod_ref, g_ref, o_ref, *, tc_tiles):
    p = jnp.where(pl.program_id(0) < tc_tiles, ptc_ref[...], psc_ref[...])
    o_ref[...] = x1_ref[...] + mod_ref[0, 5:6, :] * _rms(p, g_ref[...])


def _final(x1, peer_tc, peer_sc, modm, g_post2, group0, rows_per_group):
    n, d = x1.shape
    tm = 512
    tc_tiles = peer_tc.shape[0] // tm
    row = lambda i: (i, 0)
    return pl.pallas_call(
        functools.partial(_final_kernel, tc_tiles=tc_tiles),
        out_shape=jax.ShapeDtypeStruct((n, d), F32),
        grid=(n // tm,),
        in_specs=[pl.BlockSpec((tm, d), row),
                  pl.BlockSpec((tm, d), lambda i: (jnp.minimum(i, tc_tiles - 1), 0)),
                  pl.BlockSpec((tm, d), lambda i: (jnp.maximum(i - tc_tiles, 0), 0)),
                  pl.BlockSpec((1, 6, d), lambda i: (group0 + (i * tm) // rows_per_group, 0, 0)),
                  pl.BlockSpec((1, d), lambda i: (0, 0))],
        out_specs=pl.BlockSpec((tm, d), row),
        compiler_params=_params(("parallel",)),
        name="final_residual",
    )(x1, peer_tc, peer_sc, modm, g_post2)


def _block_diag_state(s):
    b, h = s.shape[:2]
    s = s.reshape(b, h // 2, 2, HEAD_A, HEAD_A)
    z = jnp.zeros_like(s[:, :, 0])
    return jnp.concatenate([jnp.concatenate([s[:, :, 0], z], axis=-1),
                            jnp.concatenate([z, s[:, :, 1]], axis=-1)], axis=-2)


def _head_states(s2):
    b, p = s2.shape[:2]
    return jnp.stack([s2[:, :, :HEAD_A, :HEAD_A], s2[:, :, HEAD_A:, HEAD_A:]],
                     axis=2).reshape(b, 2 * p, HEAD_A, HEAD_A)


def _layer(x3, modm, group0, grid_mode, s0f, s0b, w, tc_tokens):
    b, t, d = x3.shape
    n = b * t
    x = x3.reshape(n, d)
    rows_per_group = t if grid_mode else n
    z = _in_proj(x, modm, w["g_pre1"], w["w_in"], group0, rows_per_group)
    zs = _shift_mix(z, w["mu_shift"], w["n_shift"], w["lora_block"], grid_mode, t)
    zero_init = s0f is None
    if zero_init:
        s0f = s0b = jnp.zeros((b, w["w0"].shape[1] // LANES, LANES, LANES), F32)
    yf, yb, sf, sb = _rwkv_scan(zs.reshape(b, t, -1), w["w0"], w["w2"], w["a0"], w["a2"], w["k_k"], w["k_a"],
                                s0f, s0b, zero_init)
    d_a = yf.shape[-1]
    yag = _post_mix(yf.reshape(n, d_a), yb.reshape(n, d_a), zs, z, w["a0"], w["a2"], w["k_a"], w["r_k"],
                    w["ln_x_w"], w["ln_x_b"], w["ln_v_w"], w["ln_v_b"], w["w_s"], w["b_st"])
    x1, h2 = _out_proj(yag, w["w_out"], x, modm, w["g_post1"], w["g_pre2"], group0, rows_per_group)
    scores = _query_scores(h2, w["w_query"], w["sub_keys"])
    idx_t, gates_t = _peer_topk(scores)
    idx, gates = idx_t.T, gates_t.T
    peer_tc = _peer_experts_tc(idx, h2, gates, w["expert_rows"], tc_tokens)
    peer_sc = _peer_experts_sc(idx, h2, gates, w["experts"], tc_tokens)
    out = _final(x1, peer_tc, peer_sc, modm, w["g_post2"], group0, rows_per_group)
    return out.reshape(b, t, d), sf, sb


def kernel(x_prompt, x_sample, c, state_fwd, state_bwd, c_ctx, w_ada, b_ada, g_pre1, g_post1, g_pre2, g_post2,
           w_in, mu_shift, w0, w2, a0, a2, k_k, k_a, r_k, ln_x_w, ln_x_b, ln_v_w, ln_v_b, w_s, b_s, w_out,
           w_query, sub_keys, expert_u, expert_v):
    depth = w_in.shape[0]
    d = x_prompt.shape[-1]
    d_a = w0.shape[-1]
    n_shift = mu_shift.shape[-1]
    dec_b = x_sample.shape[0]
    cvec = jnp.concatenate([c_ctx[None], c, jnp.zeros((8 - 1 - dec_b, d), F32)], axis=0)
    xp, xs = x_prompt, x_sample
    new_f, new_b = [], []
    for l in range(depth):
        wl_in = w_in[l]
        row = lambda a: a[l].reshape(1, -1)
        w = {
            "w_in": jnp.concatenate([wl_in[:, :4 * d_a], wl_in[:, n_shift:], wl_in[:, 4 * d_a:n_shift]],
                                    axis=1).astype(BF16),
            "n_shift": n_shift,
            "lora_block": (wl_in.shape[1] - (n_shift - 4 * d_a)) // (2 * LANES),
            "mu_shift": row(mu_shift),
            "g_pre1": row(g_pre1), "g_post1": row(g_post1), "g_pre2": row(g_pre2), "g_post2": row(g_post2),
            "w0": w0[l], "w2": w2[l].reshape(-1, d_a), "a0": a0[l], "a2": a2[l].reshape(-1, d_a),
            "k_k": row(k_k), "k_a": row(k_a), "r_k": row(r_k),
            "ln_x_w": row(ln_x_w), "ln_x_b": row(ln_x_b), "ln_v_w": row(ln_v_w), "ln_v_b": row(ln_v_b),
            "w_s": w_s[l].astype(BF16), "b_st": b_s[l].T,
            "w_out": w_out[l].astype(BF16), "w_query": w_query[l].astype(BF16),
            "sub_keys": jnp.swapaxes(sub_keys[l], 0, 1).reshape(-1, N_KEYS, sub_keys.shape[-1]).astype(BF16),
            "experts": _pack_experts(expert_u[l], expert_v[l]),
        }
        w["expert_rows"] = w["experts"].reshape(-1, LANES)
        modm = _modulation(cvec, w_ada[l], b_ada[l].reshape(1, -1)).reshape(8, 6, d)
        piece = xp.shape[0] // CTX_PARTS
        parts = [_layer(xp[i * piece:(i + 1) * piece], modm, 0, False, None, None, w, PEER_TC_CTX)
                 for i in range(CTX_PARTS)]
        xp = jnp.concatenate([p[0] for p in parts], axis=0)
        new_f.append(_head_states(jnp.concatenate([p[1] for p in parts], axis=0)))
        new_b.append(_head_states(jnp.concatenate([p[2] for p in parts], axis=0)))
        xs, _, _ = _layer(xs, modm, 1, True, _block_diag_state(state_fwd[:, l]),
                          _block_diag_state(state_bwd[:, l]), w, PEER_TC_LATENT)
    return (xp, xs, jnp.stack(new_f, axis=1), jnp.stack(new_b, axis=1))
```

```python
import functools

import jax
import jax.numpy as jnp
from jax import lax
from jax.experimental import pallas as pl
from jax.experimental.pallas import tpu as pltpu
from jax.experimental.pallas import tpu_sc as plsc

F32 = jnp.float32
BF16 = jnp.bfloat16
I32 = jnp.int32

EPS = 1e-6
GN_EPS = 64e-5
HEAD_A = 64
LANES = 128
GRID_W = 64
GMLP_CHUNK = 128
PK_HEADS = 8
N_KEYS = 128
TOPK = 16
SCAN_CHUNK = 64
SCAN_PAIRS = 8
DECAY_SCALE = 0.6065306597126334
VMEM_LIMIT = 48 * 1024 * 1024


def _params(sem):
    return pltpu.CompilerParams(dimension_semantics=sem, vmem_limit_bytes=VMEM_LIMIT)


def _sigmoid(x):
    return 1.0 / (1.0 + jnp.exp(-x))


def _gelu(x):
    return 0.5 * x * (1.0 + jnp.tanh(0.7978845608028654 * (x + 0.044715 * (x * x * x))))


def _dot(a, b):
    return jnp.dot(a.astype(BF16), b.astype(BF16), preferred_element_type=F32)


def _dot_nt(a, b):
    return lax.dot_general(a.astype(BF16), b.astype(BF16), (((1,), (1,)), ((), ())),
                           preferred_element_type=F32)


def _dot_tn(a, b):
    return lax.dot_general(a.astype(BF16), b.astype(BF16), (((0,), (0,)), ((), ())),
                           preferred_element_type=F32)


def _split(x):
    hi = x.astype(BF16)
    return hi, (x - hi.astype(F32)).astype(BF16)


def _dot_x3(a, b):
    a_hi, a_lo = _split(a)
    b_hi, b_lo = _split(b)
    dot = functools.partial(jnp.dot, preferred_element_type=F32)
    return dot(a_hi, b_hi) + (dot(a_lo, b_hi) + dot(a_hi, b_lo))


def _dot_split_rhs(a_bf, b):
    b_hi, b_mid = _split(b)
    b_lo = (b - b_hi.astype(F32) - b_mid.astype(F32)).astype(BF16)
    dot = functools.partial(jnp.dot, preferred_element_type=F32)
    return dot(a_bf, b_hi) + (dot(a_bf, b_mid) + dot(a_bf, b_lo))


def _head_sum(x, first_head):
    s_a = jnp.sum(jnp.where(first_head, x, 0.0), axis=-1, keepdims=True)
    s_b = jnp.sum(jnp.where(first_head, 0.0, x), axis=-1, keepdims=True)
    return jnp.where(first_head, s_a, s_b)


def _mod_kernel(c_ref, w_ref, b_ref, o_ref):
    c = c_ref[...]
    o_ref[...] = _dot(c * _sigmoid(c), w_ref[...]) + b_ref[...]


def _modulation(cvec, w_ada, b_ada):
    rows, d = cvec.shape
    n = w_ada.shape[1]
    tn = 1024
    return pl.pallas_call(
        _mod_kernel,
        out_shape=jax.ShapeDtypeStruct((rows, n), F32),
        grid=(n // tn,),
        in_specs=[pl.BlockSpec((rows, d), lambda j: (0, 0)),
                  pl.BlockSpec((d, tn), lambda j: (0, j)),
                  pl.BlockSpec((1, tn), lambda j: (0, j))],
        out_specs=pl.BlockSpec((rows, tn), lambda j: (0, j)),
        compiler_params=_params(("parallel",)),
        name="adaln_mod",
    )(cvec, w_ada, b_ada)


def _in_proj_kernel(x_ref, mod_ref, g_ref, w_ref, o_ref, h_ref):
    @pl.when(pl.program_id(1) == 0)
    def _():
        x = x_ref[...]
        y = x * lax.rsqrt(jnp.mean(x * x, axis=-1, keepdims=True) + EPS) * g_ref[...]
        h_ref[...] = (y * (1.0 + mod_ref[0, 1:2, :]) + mod_ref[0, 0:1, :]).astype(BF16)

    o_ref[...] = jnp.dot(h_ref[...], w_ref[...], preferred_element_type=F32)


def _in_proj(x, modm, g_pre, w_bf, group0, rows_per_group):
    n, d = x.shape
    p = w_bf.shape[1]
    tm, tn = 512, 1280
    grp = lambda i, j: (group0 + (i * tm) // rows_per_group, 0, 0)
    return pl.pallas_call(
        _in_proj_kernel,
        out_shape=jax.ShapeDtypeStruct((n, p), F32),
        grid=(n // tm, p // tn),
        in_specs=[pl.BlockSpec((tm, d), lambda i, j: (i, 0)),
                  pl.BlockSpec((1, 6, d), grp),
                  pl.BlockSpec((1, d), lambda i, j: (0, 0)),
                  pl.BlockSpec((d, tn), lambda i, j: (0, j))],
        out_specs=pl.BlockSpec((tm, tn), lambda i, j: (i, j)),
        scratch_shapes=[pltpu.VMEM((tm, d), BF16)],
        compiler_params=_params(("parallel", "arbitrary")),
        name="in_proj",
    )(x, modm, g_pre, w_bf)


def _shift_kernel(z_ref, mu_ref, o_ref, *, grid_mode, period):
    z = z_ref[...]
    rows = z.shape[0]
    t = lax.broadcasted_iota(I32, (rows, 1), 0) % period
    prev = jnp.where(t % (GRID_W if grid_mode else period) != 0, pltpu.roll(z, 1, 0), 0.0)
    nxt = jnp.where(t % (GRID_W if grid_mode else period) != (GRID_W if grid_mode else period) - 1,
                    pltpu.roll(z, rows - 1, 0), 0.0)
    if grid_mode:
        up = jnp.where(t >= GRID_W, pltpu.roll(z, GRID_W, 0), 0.0)
        down = jnp.where(t < period - GRID_W, pltpu.roll(z, rows - GRID_W, 0), 0.0)
        nb = 0.25 * (up + down + prev + nxt)
    else:
        nb = 0.5 * (prev + nxt)
    o_ref[...] = z + mu_ref[...] * (nb - z)


def _shift_mix(z, mu, n_shift, lora_block, grid_mode, period):
    n = z.shape[0]
    tr, tc = 2048, 256
    main_blocks = (n_shift // tc) - 1
    col = lambda i, j: (i, jnp.where(j < main_blocks, j, lora_block))
    return pl.pallas_call(
        functools.partial(_shift_kernel, grid_mode=grid_mode, period=period),
        out_shape=jax.ShapeDtypeStruct((n, n_shift), F32),
        grid=(n // tr, n_shift // tc),
        in_specs=[pl.BlockSpec((tr, tc), col),
                  pl.BlockSpec((1, tc), lambda i, j: (0, j))],
        out_specs=pl.BlockSpec((tr, tc), lambda i, j: (i, j)),
        compiler_params=_params(("parallel", "parallel")),
        name="token_shift",
    )(z, mu)


def _scan_chunks(chains):
    c = chains[0][0].shape[0]
    c2 = 2 * c
    n = len(chains)
    fwd = [ch[9] for ch in chains]
    first_head = lax.broadcasted_iota(I32, (1, LANES), 1) < HEAD_A
    row = lax.broadcasted_iota(I32, (c2, c2), 0)
    col = lax.broadcasted_iota(I32, (c2, c2), 1)
    eye = jnp.where(row == col, 1.0, 0.0)

    def stack(x):
        return jnp.concatenate([jnp.where(first_head, x, 0.0), jnp.where(first_head, 0.0, x)],
                               axis=0).astype(BF16)

    lhs, rhs, v2, total = [], [], [], []
    for r, k, v, logw, cl, a, k_k, k_a, _, forward in chains:
        kkr = k * k_k
        kk = kkr / jnp.maximum(jnp.sqrt(_head_sum(kkr * kkr, first_head)), 1e-12)
        kd = k * (1.0 + (a - 1.0) * k_a)
        gi = jnp.exp(-cl)
        lhs.append(jnp.concatenate([stack(kk * jnp.exp(cl - logw)), stack(r * jnp.exp(cl))], axis=0))
        rhs.append(jnp.concatenate([stack(kk * a * gi), stack(kd * gi)], axis=0))
        v2.append(stack(v))
        total.append(cl[c - 1:c, :] if forward else cl[0:1, :])

    res = [_dot_nt(lhs[i], rhs[i]) for i in range(n)]
    pr = [_dot_nt(lhs[i], chains[i][8]) for i in range(n)]
    strict = [(row > col) if f else (row < col) for f in fwd]
    incl = [(row >= col) if f else (row <= col) for f in fwd]
    ab = [jnp.where(strict[i], res[i][:c2, :c2], 0.0) for i in range(n)]
    ak = [jnp.where(strict[i], res[i][:c2, c2:], 0.0) for i in range(n)]
    gb_gk = [jnp.concatenate([jnp.where(incl[i], res[i][c2:, :c2], 0.0),
                              jnp.where(incl[i], res[i][c2:, c2:], 0.0)], axis=1).astype(BF16) for i in range(n)]
    akv = [_dot(ak[i], v2[i]) for i in range(n)]

    tinv = [eye - ab[i] for i in range(n)]
    pw = [-ab[i] for i in range(n)]
    for _ in range(c.bit_length() - 2):
        pw = [_dot(pw[i], pw[i]) for i in range(n)]
        tinv = [tinv[i] + _dot(tinv[i], pw[i]) for i in range(n)]

    u2 = [_dot(tinv[i], -pr[i][:c2, :] - akv[i]) for i in range(n)]
    uv = [jnp.concatenate([u2[i].astype(BF16), v2[i]], axis=0) for i in range(n)]
    y2 = [pr[i][c2:, :] + _dot(gb_gk[i], uv[i]) for i in range(n)]
    s_new = [(chains[i][8] + _dot_tn(uv[i], rhs[i])) * jnp.exp(total[i]) for i in range(n)]
    return [(y2[i][:c, :] + y2[i][c:, :], s_new[i]) for i in range(n)]


def _scan_direction(r_ref, k_ref, v_ref, l_ref, d, s_ref, w0, w2, a0, a2, kk, ka, forward):
    lora = l_ref[0]
    c = lora.shape[0]
    dir_rows = (lax.broadcasted_iota(I32, (LANES, 1), 0) // HEAD_A) == d
    wl = w0[d:d + 1, :] + _dot_x3(jnp.tanh(lora[:, :LANES]), jnp.where(dir_rows, w2[...], 0.0))
    logw = -DECAY_SCALE * _sigmoid(wl)
    a = _sigmoid(a0[d:d + 1, :] + _dot_x3(lora[:, LANES:], jnp.where(dir_rows, a2[...], 0.0)))
    ti = lax.broadcasted_iota(I32, (c, c), 0)
    tj = lax.broadcasted_iota(I32, (c, c), 1)
    cum = jnp.where((tj <= ti) if forward else (tj >= ti), 1.0, 0.0).astype(BF16)
    cl = _dot_split_rhs(cum, logw)
    chains = []
    for p in range(s_ref.shape[0]):
        sl = slice(p * LANES, (p + 1) * LANES)
        chains.append((r_ref[0, :, sl], k_ref[0, :, sl], v_ref[0, :, sl], logw[:, sl], cl[:, sl],
                       a[:, sl], kk[:, sl], ka[:, sl], s_ref[p], forward))
    return chains


def _scan_store(out, y_ref, s_ref):
    for p, (y, s_new) in enumerate(out):
        y_ref[0, :, p * LANES:(p + 1) * LANES] = y
        s_ref[p] = s_new


def _scan_kernel(rf, kf, vf, lf, rb, kb, vb, lb, w0, w2, a0, a2, kk, ka, s0f, s0b,
                 yf, yb, sf, sb, s2f, s2b, *, zero_init):
    c = pl.program_id(2)

    @pl.when(c == 0)
    def _():
        if zero_init:
            s2f[...] = jnp.zeros_like(s2f)
            s2b[...] = jnp.zeros_like(s2b)
        else:
            s2f[...] = s0f[0]
            s2b[...] = s0b[0]

    chains_f = _scan_direction(rf, kf, vf, lf, 0, s2f, w0, w2, a0, a2, kk, ka, True)
    chains_b = _scan_direction(rb, kb, vb, lb, 1, s2b, w0, w2, a0, a2, kk, ka, False)
    out = _scan_chunks(chains_f + chains_b)
    _scan_store(out[:len(chains_f)], yf, s2f)
    _scan_store(out[len(chains_f):], yb, s2b)

    @pl.when(c == pl.num_programs(2) - 1)
    def _():
        sf[0] = s2f[...]
        sb[0] = s2b[...]


def _rwkv_scan(zs, w0, w2r, a0, a2r, k_k, k_a, s0f, s0b, zero_init):
    b, t, _ = zs.shape
    d_a = w0.shape[1]
    pp = SCAN_PAIRS
    wide = pp * LANES
    groups = d_a // wide
    c = SCAN_CHUNK
    nc = t // c
    lora_blk = (4 * d_a) // (2 * LANES)
    fw = lambda off: (lambda i, q, j: (i, j, off + q))
    bw = lambda off: (lambda i, q, j: (i, nc - 1 - j, off + q))
    par = lambda i, q, j: (0, q)
    st = lambda i, q, j: (i, q, 0, 0)
    blk = (1, c, wide)
    lblk = (1, c, 2 * LANES)
    sblk = (1, pp, LANES, LANES)
    yshape = jax.ShapeDtypeStruct((b, t, d_a), F32)
    sshape = jax.ShapeDtypeStruct((b, d_a // LANES, LANES, LANES), F32)
    return pl.pallas_call(
        functools.partial(_scan_kernel, zero_init=zero_init),
        out_shape=(yshape, yshape, sshape, sshape),
        grid=(b, groups, nc),
        in_specs=[pl.BlockSpec(blk, fw(0)), pl.BlockSpec(blk, fw(groups)), pl.BlockSpec(blk, fw(2 * groups)),
                  pl.BlockSpec(lblk, lambda i, q, j: (i, j, lora_blk)),
                  pl.BlockSpec(blk, bw(0)), pl.BlockSpec(blk, bw(groups)), pl.BlockSpec(blk, bw(2 * groups)),
                  pl.BlockSpec(lblk, lambda i, q, j: (i, nc - 1 - j, lora_blk)),
                  pl.BlockSpec((2, wide), par), pl.BlockSpec((LANES, wide), par),
                  pl.BlockSpec((2, wide), par), pl.BlockSpec((LANES, wide), par),
                  pl.BlockSpec((1, wide), par), pl.BlockSpec((1, wide), par),
                  pl.BlockSpec(sblk, st), pl.BlockSpec(sblk, st)],
        out_specs=(pl.BlockSpec(blk, fw(0)), pl.BlockSpec(blk, bw(0)),
                   pl.BlockSpec(sblk, st), pl.BlockSpec(sblk, st)),
        scratch_shapes=[pltpu.VMEM((pp, LANES, LANES), F32), pltpu.VMEM((pp, LANES, LANES), F32)],
        compiler_params=_params(("parallel", "parallel", "arbitrary")),
        name="rwkv7_scan",
    )(zs, zs, zs, zs, zs, zs, zs, zs, w0, w2r, a0, a2r, k_k, k_a, s0f, s0b)


def _post_kernel(yf, yb, r, k, v, g, lora, u, vg, a0, a2, ka, rk, lxw, lxb, lvw, lvb, ws, bst, o_ref):
    tm = yf.shape[0]
    d_a = yf.shape[1]
    first_head = lax.broadcasted_iota(I32, (1, LANES), 1) < HEAD_A
    dir_row = lax.broadcasted_iota(I32, (LANES, 1), 0) // HEAD_A
    la = lora[:, LANES:]
    a_sum = jnp.zeros((tm, d_a), F32)
    for d in range(2):
        a_sum = a_sum + _sigmoid(a0[d:d + 1, :] + _dot_x3(la, jnp.where(dir_row == d, a2[...], 0.0)))
    rkk = r[...] * k[...] * (2.0 + (a_sum - 2.0) * ka[...]) * rk[...]
    y = yf[...] + yb[...]
    inv = 1.0 / HEAD_A
    for j in range(d_a // LANES):
        sl = slice(j * LANES, (j + 1) * LANES)
        yj = y[:, sl]
        mu = _head_sum(yj, first_head) * inv
        dl = yj - mu
        var = _head_sum(dl * dl, first_head) * inv
        yn = dl * lax.rsqrt(var + GN_EPS) * lxw[:, sl] + lxb[:, sl]
        bonus = _head_sum(rkk[:, sl], first_head) * v[:, sl]
        o_ref[:, sl] = ((yn + bonus) * _sigmoid(g[:, sl])).astype(BF16)

    uu = _gelu(u[...])
    vv = _gelu(vg[...])
    mu = jnp.mean(vv, axis=-1, keepdims=True)
    dv = vv - mu
    vn = dv * lax.rsqrt(jnp.mean(dv * dv, axis=-1, keepdims=True) + EPS) * lvw[...] + lvb[...]
    for ch in range(tm // GMLP_CHUNK):
        rows = slice(ch * GMLP_CHUNK, (ch + 1) * GMLP_CHUNK)
        for h in range(ws.shape[0]):
            cols = slice(h * LANES, (h + 1) * LANES)
            sp = _dot(ws[h], vn[rows, cols]) + bst[:, h:h + 1]
            o_ref[rows, d_a + h * LANES:d_a + (h + 1) * LANES] = (uu[rows, cols] * sp).astype(BF16)


def _post_mix(yf, yb, zs, z, a0, a2r, k_a, r_k, lxw, lxb, lvw, lvb, w_s, b_st):
    n, d_a = yf.shape
    tm = 256
    wide = lambda j: (lambda i: (i, j))
    full = lambda shape: pl.BlockSpec(shape, lambda i: (0,) * len(shape))
    lora_blk = (4 * d_a) // (2 * LANES)
    return pl.pallas_call(
        _post_kernel,
        out_shape=jax.ShapeDtypeStruct((n, 2 * d_a), BF16),
        grid=(n // tm,),
        in_specs=[pl.BlockSpec((tm, d_a), wide(0)), pl.BlockSpec((tm, d_a), wide(0)),
                  pl.BlockSpec((tm, d_a), wide(0)), pl.BlockSpec((tm, d_a), wide(1)),
                  pl.BlockSpec((tm, d_a), wide(2)), pl.BlockSpec((tm, d_a), wide(3)),
                  pl.BlockSpec((tm, 2 * LANES), wide(lora_blk)),
                  pl.BlockSpec((tm, d_a), wide(4)), pl.BlockSpec((tm, d_a), wide(5)),
                  full((2, d_a)), full((LANES, d_a)), full((1, d_a)), full((1, d_a)),
                  full((1, d_a)), full((1, d_a)), full((1, d_a)), full((1, d_a)),
                  full(w_s.shape), full(b_st.shape)],
        out_specs=pl.BlockSpec((tm, 2 * d_a), wide(0)),
        compiler_params=_params(("parallel",)),
        name="mix_post",
    )(yf, yb, zs, zs, zs, zs, zs, z, z, a0, a2r, k_a, r_k, lxw, lxb, lvw, lvb, w_s, b_st)


def _rms(x, g):
    return x * lax.rsqrt(jnp.mean(x * x, axis=-1, keepdims=True) + EPS) * g


def _out_proj_kernel(a_ref, w_ref, x_ref, mod_ref, gpost_ref, gpre_ref, x1_ref, h2_ref):
    o = jnp.dot(a_ref[...], w_ref[...], preferred_element_type=F32)
    x1 = x_ref[...] + mod_ref[0, 2:3, :] * _rms(o, gpost_ref[...])
    x1_ref[...] = x1
    h2_ref[...] = _rms(x1, gpre_ref[...]) * (1.0 + mod_ref[0, 4:5, :]) + mod_ref[0, 3:4, :]


def _out_proj(yag, w_bf, x, modm, g_post, g_pre2, group0, rows_per_group):
    n, d = x.shape
    tm = 256
    grp = lambda i: (group0 + (i * tm) // rows_per_group, 0, 0)
    row = lambda i: (i, 0)
    fix = lambda i: (0, 0)
    shp = jax.ShapeDtypeStruct((n, d), F32)
    return pl.pallas_call(
        _out_proj_kernel,
        out_shape=(shp, shp),
        grid=(n // tm,),
        in_specs=[pl.BlockSpec((tm, yag.shape[1]), row), pl.BlockSpec(w_bf.shape, fix),
                  pl.BlockSpec((tm, d), row), pl.BlockSpec((1, 6, d), grp),
                  pl.BlockSpec((1, d), fix), pl.BlockSpec((1, d), fix)],
        out_specs=(pl.BlockSpec((tm, d), row), pl.BlockSpec((tm, d), row)),
        compiler_params=_params(("parallel",)),
        name="out_proj",
    )(yag, w_bf, x, modm, g_post, g_pre2)


def _query_kernel(h_ref, wq_ref, sk_ref, o_ref):
    q = jnp.dot(h_ref[...].astype(BF16), wq_ref[...], preferred_element_type=F32)
    for g in range(sk_ref.shape[0]):
        o_ref[g * N_KEYS:(g + 1) * N_KEYS, :] = _dot_nt(sk_ref[g], q[:, g * LANES:(g + 1) * LANES])


def _query_scores(h2, wq_bf, sk):
    n, d = h2.shape
    tm = 256
    groups = sk.shape[0]
    return pl.pallas_call(
        _query_kernel,
        out_shape=jax.ShapeDtypeStruct((groups * N_KEYS, n), F32),
        grid=(n // tm,),
        in_specs=[pl.BlockSpec((tm, d), lambda i: (i, 0)),
                  pl.BlockSpec(wq_bf.shape, lambda i: (0, 0)),
                  pl.BlockSpec(sk.shape, lambda i: (0, 0, 0))],
        out_specs=pl.BlockSpec((groups * N_KEYS, tm), lambda i: (0, i)),
        compiler_params=_params(("parallel",)),
        name="peer_query",
    )(h2, wq_bf, sk)


def _top16(s, pos=None):
    if pos is None:
        pos = lax.broadcasted_iota(I32, s.shape, 0)
    big = jnp.int32(2 ** 30)
    vals, idxs = [], []
    for _ in range(TOPK):
        m = jnp.max(s, axis=0, keepdims=True)
        i = jnp.min(jnp.where(s == m, pos, big), axis=0, keepdims=True)
        vals.append(m)
        idxs.append(i)
        s = jnp.where(pos == i, -jnp.inf, s)
    return jnp.concatenate(vals, axis=0), jnp.concatenate(idxs, axis=0)


def _pair_candidates(v1, v2):
    r8 = lax.broadcasted_iota(I32, (8, 1), 0)
    r16 = lax.broadcasted_iota(I32, (TOPK, 1), 0)
    ninf = -jnp.inf
    blocks = [
        (v1[0:1, :] + v2, r16),
        (v1[1:2, :] + v2[0:8, :], TOPK + r8),
        (jnp.where(r8 < 5, v1[2:3, :] + v2[0:8, :], ninf), 2 * TOPK + r8),
        (jnp.where(r8 < 4, v1[3:4, :] + v2[0:8, :], ninf), 3 * TOPK + r8),
        (jnp.where(r16 >= 4, v1 + v2[0:1, :], ninf), r16 * TOPK),
        (jnp.where(r8 >= 4, v1[0:8, :] + v2[1:2, :], ninf), r8 * TOPK + 1),
        (jnp.where(r8 == 4, v1[0:8, :] + v2[2:3, :], ninf), r8 * TOPK + 2),
    ]
    cand = jnp.concatenate([b for b, _ in blocks], axis=0)
    pos = jnp.concatenate([p for _, p in blocks], axis=0)
    return cand, jnp.broadcast_to(pos, cand.shape)


def _topk_kernel(s_ref, idx_ref, gate_ref):
    def head(h, carry):
        base = pl.multiple_of(h * (2 * N_KEYS), 2 * N_KEYS)
        v1, i1 = _top16(s_ref[pl.ds(base, N_KEYS), :])
        v2, i2 = _top16(s_ref[pl.ds(base + N_KEYS, N_KEYS), :])
        top_s, pos = _top16(*_pair_candidates(v1, v2))
        pi = pos // TOPK
        pj = pos % TOPK
        e1 = jnp.zeros_like(pos)
        e2 = jnp.zeros_like(pos)
        for i in range(TOPK):
            e1 = e1 + jnp.where(pi == i, i1[i:i + 1, :], 0)
            e2 = e2 + jnp.where(pj == i, i2[i:i + 1, :], 0)
        ex = jnp.exp(top_s - top_s[0:1, :])
        out = pl.multiple_of(h * TOPK, TOPK)
        idx_ref[pl.ds(out, TOPK), :] = e1 * N_KEYS + e2
        gate_ref[pl.ds(out, TOPK), :] = ex / jnp.sum(ex, axis=0, keepdims=True)
        return carry

    lax.fori_loop(0, PK_HEADS, head, 0)


def _peer_topk(scores):
    rows, n = scores.shape
    tt = 256
    out_rows = PK_HEADS * TOPK
    return pl.pallas_call(
        _topk_kernel,
        out_shape=(jax.ShapeDtypeStruct((out_rows, n), I32), jax.ShapeDtypeStruct((out_rows, n), F32)),
        grid=(n // tt,),
        in_specs=[pl.BlockSpec((rows, tt), lambda i: (0, i))],
        out_specs=(pl.BlockSpec((out_rows, tt), lambda i: (0, i)),
                   pl.BlockSpec((out_rows, tt), lambda i: (0, i))),
        compiler_params=_params(("parallel",)),
        name="peer_topk",
    )(scores)


SC_CORES = 2
SC_SUBCORES = 16
SC_LANES = 16
PEER_GROUP = 16
PEER_BLOCK = 16
U_MASK = -65536


def _pack_experts(eu, ev):
    hi = lax.bitcast_convert_type(eu.astype(BF16), jnp.uint16).astype(jnp.uint32)
    lo = lax.bitcast_convert_type(ev.astype(BF16), jnp.uint16).astype(jnp.uint32)
    return lax.bitcast_convert_type((hi << 16) | lo, I32)


def _unpack_u(w):
    return lax.bitcast_convert_type(w & U_MASK, F32)


def _unpack_v(w):
    return lax.bitcast_convert_type(w << 16, F32)


def _peer_experts_sc(idx, x, gates, table, start):
    d = x.shape[1]
    n = x.shape[0] - start
    n_sel = idx.shape[1]
    per_w = n // (SC_CORES * SC_SUBCORES)
    L, G, TB = SC_LANES, PEER_GROUP, PEER_BLOCK
    n_groups = n_sel // G
    n_ch = d // L
    mesh = plsc.VectorSubcoreMesh(core_axis_name="c", subcore_axis_name="s")

    @functools.partial(
        pl.kernel, mesh=mesh, out_type=jax.ShapeDtypeStruct((n, d), F32),
        compiler_params=pltpu.CompilerParams(needs_layout_passes=False),
        scratch_types=[pltpu.VMEM((TB, n_sel), I32), pltpu.VMEM((TB, n_sel), F32),
                       pltpu.VMEM((2, d), F32), pltpu.VMEM((2, d), F32),
                       pltpu.VMEM((2, G, d), I32), pltpu.VMEM((G, 2 * L), F32), pltpu.VMEM((2 * L,), F32),
                       pltpu.SemaphoreType.DMA((2,)), pltpu.SemaphoreType.DMA((2,)), pltpu.SemaphoreType.DMA((2,))],
        name="peer_experts_sc")
    def k(idx_hbm, x_hbm, g_hbm, tab_hbm, o_hbm, idx_v, gate_v, x_v, o_v, buf, acc_v, coef_v, sem, xsem, osem):
        out_base = (lax.axis_index("s") * SC_CORES + lax.axis_index("c")) * per_w
        base = start + out_base
        lane = lax.iota(I32, L)
        zero = jnp.zeros((L,), F32)

        def gather(t, g, slot):
            return pltpu.make_async_copy(tab_hbm.at[idx_v.at[t, pl.ds(g * G, G)]], buf.at[slot], sem.at[slot])

        def x_copy(j, par):
            return pltpu.make_async_copy(x_hbm.at[base + j], x_v.at[par], xsem.at[par])

        def o_copy(j, par):
            return pltpu.make_async_copy(o_v.at[par], o_hbm.at[out_base + j], osem.at[par])

        x_copy(0, 0).start()

        @pl.loop(0, per_w // TB)
        def _(blk):
            tok0 = base + blk * TB
            pltpu.sync_copy(idx_hbm.at[pl.ds(tok0, TB)], idx_v)
            pltpu.sync_copy(g_hbm.at[pl.ds(tok0, TB)], gate_v)

            @pl.loop(0, TB)
            def _(t):
                j = blk * TB + t
                par = j % 2
                gather(t, 0, 0).start()

                @pl.when(j + 1 < per_w)
                def _():
                    x_copy(j + 1, 1 - par).start()

                x_copy(j, par).wait()

                @pl.when(j >= 2)
                def _():
                    o_copy(j - 2, par).wait()

                for g in range(n_groups):
                    slot = g % 2
                    if g + 1 < n_groups:
                        gather(t, g + 1, 1 - slot).start()
                    gather(t, g, slot).wait()

                    def u_body(ch, accs):
                        xv = x_v[par, pl.ds(ch * L, L)]
                        return tuple(accs[e] + _unpack_u(buf[slot, e, pl.ds(ch * L, L)]) * xv for e in range(G))

                    accs = plsc.parallel_loop(0, n_ch, carry=tuple(zero for _ in range(G)))(u_body)
                    for e in range(G):
                        acc_v[e, pl.ds(L, L)] = accs[e]
                    act = zero
                    for l in range(L):
                        act = act + plsc.load_gather(acc_v, [lane, jnp.full((L,), L + l, I32)])
                    y = 0.7978845608028654 * (act + 0.044715 * (act * act * act))
                    coef_v[pl.ds(L, L)] = act / (1.0 + jnp.exp(-2.0 * y)) * gate_v[t, pl.ds(g * G, G)]
                    cs = [plsc.load_gather(coef_v, [jnp.full((L,), L + e, I32)]) for e in range(G)]

                    @plsc.parallel_loop(0, n_ch)
                    def _(ch):
                        o = zero if g == 0 else o_v[par, pl.ds(ch * L, L)]
                        for e in range(G):
                            o = o + cs[e] * _unpack_v(buf[slot, e, pl.ds(ch * L, L)])
                        o_v[par, pl.ds(ch * L, L)] = o

                o_copy(j, par).start()

        o_copy(per_w - 2, 0).wait()
        o_copy(per_w - 1, 1).wait()

    return k(idx, x, gates, table)


PEER_TC_TOKENS = 64
CTX_PARTS = 4
PEER_TC_SLOTS = 3
PEER_TC_CTX = 512
PEER_TC_LATENT = 3584


def _peer_tc_kernel(idx_ref, x_ref, g_ref, uv_ref, o_ref, buf, sem):
    slots, rows, n_sel = buf.shape[0], buf.shape[1], buf.shape[2]
    tokens = g_ref.shape[0]
    ahead = slots - 1

    def issue(t):
        slot = t % slots

        def body(i, carry):
            for prio in range(2):
                e = 2 * i + prio
                src = pl.multiple_of(idx_ref[t, e] * rows, rows)
                pltpu.make_async_copy(uv_ref.at[pl.ds(src, rows), :], buf.at[slot, :, e, :],
                                      sem.at[slot]).start(priority=prio)
            return carry
        lax.fori_loop(0, n_sel // 2, body, 0, unroll=4)

    def wait(slot):
        pltpu.make_async_copy(buf.at[(slot + 1) % slots], buf.at[slot], sem.at[slot]).wait()

    for t0 in range(ahead):
        issue(t0)
    g_t = g_ref[...].T
    g_hi, g_mid = _split(g_t)
    g_lo = (g_t - g_hi.astype(F32) - g_mid.astype(F32)).astype(BF16)
    tok_iota = lax.broadcasted_iota(I32, (tokens, LANES), 0)
    dot = functools.partial(jnp.dot, preferred_element_type=F32)

    def token(t, carry):
        slot = t % slots

        @pl.when(t + ahead < tokens)
        def _():
            issue(t + ahead)

        wait(slot)
        pick = jnp.where(tok_iota == t, 1.0, 0.0).astype(BF16)
        gate = dot(g_hi, pick) + (dot(g_mid, pick) + dot(g_lo, pick))
        xt = x_ref[pl.ds(t, 1), :]
        acc = _unpack_u(buf[slot, 0]) * xt[:, 0:LANES]
        for c in range(1, rows):
            acc = acc + _unpack_u(buf[slot, c]) * xt[:, c * LANES:(c + 1) * LANES]
        act = jnp.sum(acc, axis=-1, keepdims=True)
        coef = _gelu(act) * gate
        o_ref[pl.ds(t, 1), :] = jnp.concatenate(
            [jnp.sum(coef * _unpack_v(buf[slot, c]), axis=0, keepdims=True) for c in range(rows)], axis=1)
        return carry

    lax.fori_loop(0, tokens, token, 0)


def _peer_experts_tc(idx, x, gates, table_rows, m):
    d = x.shape[1]
    n_sel = idx.shape[1]
    tt = PEER_TC_TOKENS
    return pl.pallas_call(
        _peer_tc_kernel,
        out_shape=jax.ShapeDtypeStruct((m, d), F32),
        grid=(m // tt,),
        in_specs=[pl.BlockSpec((tt, n_sel), lambda i: (i, 0), memory_space=pltpu.SMEM),
                  pl.BlockSpec((tt, d), lambda i: (i, 0)),
                  pl.BlockSpec((tt, n_sel), lambda i: (i, 0)),
                  pl.BlockSpec(memory_space=pl.ANY)],
        out_specs=pl.BlockSpec((tt, d), lambda i: (i, 0)),
        scratch_shapes=[pltpu.VMEM((PEER_TC_SLOTS, d // LANES, n_sel, LANES), I32),
                        pltpu.SemaphoreType.DMA((PEER_TC_SLOTS,))],
        compiler_params=pltpu.CompilerParams(dimension_semantics=("arbitrary",),
                                             vmem_limit_bytes=VMEM_LIMIT, disable_bounds_checks=True),
        name="peer_experts_tc",
    )(idx, x, gates, table_rows)


def _final_kernel(x1_ref, ptc_ref, psc_ref, mod_ref, g_ref, o_ref, *, tc_tiles):
    p = jnp.where(pl.program_id(0) < tc_tiles, ptc_ref[...], psc_ref[...])
    o_ref[...] = x1_ref[...] + mod_ref[0, 5:6, :] * _rms(p, g_ref[...])


def _final(x1, peer_tc, peer_sc, modm, g_post2, group0, rows_per_group):
    n, d = x1.shape
    tm = 512
    tc_tiles = peer_tc.shape[0] // tm
    row = lambda i: (i, 0)
    return pl.pallas_call(
        functools.partial(_final_kernel, tc_tiles=tc_tiles),
        out_shape=jax.ShapeDtypeStruct((n, d), F32),
        grid=(n // tm,),
        in_specs=[pl.BlockSpec((tm, d), row),
                  pl.BlockSpec((tm, d), lambda i: (jnp.minimum(i, tc_tiles - 1), 0)),
                  pl.BlockSpec((tm, d), lambda i: (jnp.maximum(i - tc_tiles, 0), 0)),
                  pl.BlockSpec((1, 6, d), lambda i: (group0 + (i * tm) // rows_per_group, 0, 0)),
                  pl.BlockSpec((1, d), lambda i: (0, 0))],
        out_specs=pl.BlockSpec((tm, d), row),
        compiler_params=_params(("parallel",)),
        name="final_residual",
    )(x1, peer_tc, peer_sc, modm, g_post2)


def _block_diag_state(s):
    b, h = s.shape[:2]
    s = s.reshape(b, h // 2, 2, HEAD_A, HEAD_A)
    z = jnp.zeros_like(s[:, :, 0])
    return jnp.concatenate([jnp.concatenate([s[:, :, 0], z], axis=-1),
                            jnp.concatenate([z, s[:, :, 1]], axis=-1)], axis=-2)


def _head_states(s2):
    b, p = s2.shape[:2]
    return jnp.stack([s2[:, :, :HEAD_A, :HEAD_A], s2[:, :, HEAD_A:, HEAD_A:]],
                     axis=2).reshape(b, 2 * p, HEAD_A, HEAD_A)


def _layer(x3, modm, group0, grid_mode, s0f, s0b, w, tc_tokens):
    b, t, d = x3.shape
    n = b * t
    x = x3.reshape(n, d)
    rows_per_group = t if grid_mode else n
    z = _in_proj(x, modm, w["g_pre1"], w["w_in"], group0, rows_per_group)
    zs = _shift_mix(z, w["mu_shift"], w["n_shift"], w["lora_block"], grid_mode, t)
    zero_init = s0f is None
    if zero_init:
        s0f = s0b = jnp.zeros((b, w["w0"].shape[1] // LANES, LANES, LANES), F32)
    yf, yb, sf, sb = _rwkv_scan(zs.reshape(b, t, -1), w["w0"], w["w2"], w["a0"], w["a2"], w["k_k"], w["k_a"],
                                s0f, s0b, zero_init)
    d_a = yf.shape[-1]
    yag = _post_mix(yf.reshape(n, d_a), yb.reshape(n, d_a), zs, z, w["a0"], w["a2"], w["k_a"], w["r_k"],
                    w["ln_x_w"], w["ln_x_b"], w["ln_v_w"], w["ln_v_b"], w["w_s"], w["b_st"])
    x1, h2 = _out_proj(yag, w["w_out"], x, modm, w["g_post1"], w["g_pre2"], group0, rows_per_group)
    scores = _query_scores(h2, w["w_query"], w["sub_keys"])
    idx_t, gates_t = _peer_topk(scores)
    idx, gates = idx_t.T, gates_t.T
    peer_tc = _peer_experts_tc(idx, h2, gates, w["expert_rows"], tc_tokens)
    peer_sc = _peer_experts_sc(idx, h2, gates, w["experts"], tc_tokens)
    out = _final(x1, peer_tc, peer_sc, modm, w["g_post2"], group0, rows_per_group)
    return out.reshape(b, t, d), sf, sb


def kernel(x_prompt, x_sample, c, state_fwd, state_bwd, c_ctx, w_ada, b_ada, g_pre1, g_post1, g_pre2, g_post2,
           w_in, mu_shift, w0, w2, a0, a2, k_k, k_a, r_k, ln_x_w, ln_x_b, ln_v_w, ln_v_b, w_s, b_s, w_out,
           w_query, sub_keys, expert_u, expert_v):
    depth = w_in.shape[0]
    d = x_prompt.shape[-1]
    d_a = w0.shape[-1]
    n_shift = mu_shift.shape[-1]
    dec_b = x_sample.shape[0]
    cvec = jnp.concatenate([c_ctx[None], c, jnp.zeros((8 - 1 - dec_b, d), F32)], axis=0)
    xp, xs = x_prompt, x_sample
    new_f, new_b = [], []
    for l in range(depth):
        wl_in = w_in[l]
        row = lambda a: a[l].reshape(1, -1)
        w = {
            "w_in": jnp.concatenate([wl_in[:, :4 * d_a], wl_in[:, n_shift:], wl_in[:, 4 * d_a:n_shift]],
                                    axis=1).astype(BF16),
            "n_shift": n_shift,
            "lora_block": (wl_in.shape[1] - (n_shift - 4 * d_a)) // (2 * LANES),
            "mu_shift": row(mu_shift),
            "g_pre1": row(g_pre1), "g_post1": row(g_post1), "g_pre2": row(g_pre2), "g_post2": row(g_post2),
            "w0": w0[l], "w2": w2[l].reshape(-1, d_a), "a0": a0[l], "a2": a2[l].reshape(-1, d_a),
            "k_k": row(k_k), "k_a": row(k_a), "r_k": row(r_k),
            "ln_x_w": row(ln_x_w), "ln_x_b": row(ln_x_b), "ln_v_w": row(ln_v_w), "ln_v_b": row(ln_v_b),
            "w_s": w_s[l].astype(BF16), "b_st": b_s[l].T,
            "w_out": w_out[l].astype(BF16), "w_query": w_query[l].astype(BF16),
            "sub_keys": jnp.swapaxes(sub_keys[l], 0, 1).reshape(-1, N_KEYS, sub_keys.shape[-1]).astype(BF16),
            "experts": _pack_experts(expert_u[l], expert_v[l]),
        }
        w["expert_rows"] = w["experts"].reshape(-1, LANES)
        modm = _modulation(cvec, w_ada[l], b_ada[l].reshape(1, -1)).reshape(8, 6, d)
        piece = xp.shape[0] // CTX_PARTS
        parts = [_layer(xp[i * piece:(i + 1) * piece], modm, 0, False, None, None, w, PEER_TC_CTX)
                 for i in range(CTX_PARTS)]
        xp = jnp.concatenate([p[0] for p in parts], axis=0)
        new_f.append(_head_states(jnp.concatenate([p[1] for p in parts], axis=0)))
        new_b.append(_head_states(jnp.concatenate([p[2] for p in parts], axis=0)))
        xs, _, _ = _layer(xs, modm, 1, True, _block_diag_state(state_fwd[:, l]),
                          _block_diag_state(state_bwd[:, l]), w, PEER_TC_LATENT)
    return (xp, xs, jnp.stack(new_f, axis=1), jnp.stack(new_b, axis=1))
```
